```python
import jax, jax.numpy as jnp
from jax import lax
import numpy as np

D_MODEL = 4096
BATCH = 1
SEQ = 8192
DEPTH = 2

GRID_W = 64
CTX_LEN = 256
HEAD_DIM = 128
FNET_WIDTH = D_MODEL // 4
FNET_GROUPS = FNET_WIDTH // HEAD_DIM
NA_WIDTH = 3 * D_MODEL // 8
NA_HEADS = NA_WIDTH // HEAD_DIM
HGRN_WIDTH = 3 * D_MODEL // 8
HGRN_HEADS = HGRN_WIDTH // HEAD_DIM
MIX_WIDTH = FNET_WIDTH + NA_WIDTH + HGRN_WIDTH
PROJ_SPLITS = (FNET_WIDTH, NA_WIDTH, NA_WIDTH, NA_WIDTH, HGRN_WIDTH, HGRN_WIDTH, HGRN_WIDTH, HGRN_WIDTH, HGRN_WIDTH)
PROJ_WIDTH = sum(PROJ_SPLITS)
NA_KR = 8
NA_KC = 16
NA_QROWS = 2
ROPE_THETA = 10000.0
ROPE_AXIS_PAIRS = HEAD_DIM // 4
HGRN_CHUNK = 64
N_EXPERTS = 32
TOP_K = 4
EXPERT_FF = 640
SWIGLU_LIMIT = 7.0
SWIGLU_ALPHA = 1.702
MOE_BLOCK = 128
RMS_EPS = 1e-6
MASK_VALUE = -1e30
GATE_FLOOR = 1e-30

kernel_name = 'hybrid_fnet_natten_hgrn2_moe_dit'


def rms_norm(x, g):
    x32 = x.astype(jnp.float32)
    y = x32 * lax.rsqrt(jnp.mean(x32 * x32, axis=-1, keepdims=True) + RMS_EPS)
    return (y * g.astype(jnp.float32)).astype(x.dtype)


def adaln(cond, w, b):
    return jnp.split(jax.nn.silu(cond) @ w + b, 6, axis=-1)


def split_proj(p):
    return jnp.split(p, [int(i) for i in np.cumsum(PROJ_SPLITS)[:-1]], axis=-1)


def axial_rope_tables(n_tok):
    t = jnp.arange(n_tok, dtype=jnp.int32)
    row = (t // GRID_W).astype(jnp.float32)
    col = (t % GRID_W).astype(jnp.float32)
    inv_freq = ROPE_THETA ** (-jnp.arange(ROPE_AXIS_PAIRS, dtype=jnp.float32) / ROPE_AXIS_PAIRS)
    ang = jnp.concatenate([row[:, None] * inv_freq, col[:, None] * inv_freq], axis=-1)
    return jnp.cos(ang), jnp.sin(ang)


def apply_axial_rope(x, cos, sin):
    xe, xo = x[..., 0::2], x[..., 1::2]
    c = cos[None, :, None, :]
    s = sin[None, :, None, :]
    return jnp.stack([xe * c - xo * s, xe * s + xo * c], axis=-1).reshape(x.shape)


def fourier_mix(u, w, b):
    B, T, _ = u.shape
    ug = u.astype(jnp.float32).reshape(B, T, FNET_GROUPS, FNET_WIDTH // FNET_GROUPS)
    z = jnp.fft.fft2(ug, axes=(1, 3), norm='ortho').real.reshape(B, T, FNET_WIDTH)
    return z.astype(u.dtype) @ w + b


def na_index_tables(rows):
    kr = min(NA_KR, rows)
    nb = min(kr + NA_QROWS - 1, rows)
    nqb = rows // NA_QROWS
    r0 = np.arange(nqb) * NA_QROWS
    band0 = np.minimum(np.clip(r0 - kr // 2, 0, rows - kr), rows - nb)
    band_rows = band0[:, None] + np.arange(nb)[None, :]
    q_row = np.repeat(r0[:, None] + np.arange(NA_QROWS)[None, :], GRID_W, axis=1)
    q_col = np.tile(np.arange(GRID_W), NA_QROWS)
    k_row = np.repeat(band_rows, GRID_W, axis=1)
    k_col = np.tile(np.arange(GRID_W), nb)
    win_r = np.clip(q_row - kr // 2, 0, rows - kr)[:, :, None]
    win_c = np.clip(q_col - NA_KC // 2, 0, GRID_W - NA_KC)[:, None]
    kro = k_row[:, None, :]
    col_ok = (k_col[None, :] >= win_c) & (k_col[None, :] < win_c + NA_KC)
    mask = (kro >= win_r) & (kro < win_r + kr) & col_ok[None]
    d_row = np.clip(kro - q_row[:, :, None] + NA_KR - 1, 0, 2 * NA_KR - 2)
    d_col = np.clip(k_col[None, :] - q_col[:, None] + NA_KC - 1, 0, 2 * NA_KC - 2)
    return band_rows, mask, d_row, d_col


def neighborhood_attention(q, k, v, k_ctx, v_ctx, rpb, rows):
    B, L, H, dh = q.shape
    band_rows, mask, d_row, d_col = na_index_tables(rows)
    nqb, qb, nk = mask.shape
    scale = HEAD_DIM ** -0.5
    q_b = q.reshape(B, nqb, qb, H, dh)
    k_g = k.reshape(B, rows, GRID_W, H, dh)[:, band_rows].reshape(B, nqb, nk, H, dh)
    v_g = v.reshape(B, rows, GRID_W, H, dh)[:, band_rows].reshape(B, nqb, nk, H, dh)
    bias = rpb.astype(jnp.float32)[:, d_row, d_col[None]].transpose(1, 0, 2, 3)
    s_loc = jnp.einsum('bnqhd,bnkhd->bnhqk', q_b, k_g) * scale + bias[None]
    s_loc = jnp.where(mask[None, :, None], s_loc, MASK_VALUE)
    s_ctx = jnp.einsum('bnqhd,bchd->bnhqc', q_b, k_ctx) * scale
    p = jax.nn.softmax(jnp.concatenate([s_loc, s_ctx], axis=-1), axis=-1)
    o = (jnp.einsum('bnhqk,bnkhd->bnqhd', p[..., :nk], v_g)
         + jnp.einsum('bnhqc,bchd->bnqhd', p[..., nk:], v_ctx))
    return o.reshape(B, L, H * dh)


def context_attention(q, k, v):
    s = jnp.einsum('bqhd,bkhd->bhqk', q, k) * HEAD_DIM ** -0.5
    o = jnp.einsum('bhqk,bkhd->bqhd', jax.nn.softmax(s, axis=-1), v)
    return o.reshape(q.shape[0], q.shape[1], -1)


def hgrn_gates(z, lb):
    lb32 = lb.astype(jnp.float32)
    f = lb32 + (1.0 - lb32) * jax.nn.sigmoid(z)
    log_f = jnp.log(jnp.maximum(f, GATE_FLOOR))
    k = (1.0 - lb32) * jax.nn.sigmoid(-z)
    return log_f, k


def chunk_gated_scan(q, k, v, log_f, s0):
    B, T, H, dk = q.shape
    dv = v.shape[-1]
    C = HGRN_CHUNK
    nc = T // C

    def chunks(a):
        return a.reshape(B, nc, C, H, a.shape[-1]).transpose(1, 0, 3, 2, 4)

    incl = jnp.asarray(np.tril(np.ones((C, C), dtype=bool)))[:, :, None]

    def step(S, inp):
        qc, kc, vc, lc = inp
        b = jnp.cumsum(lc, axis=2)
        o_inter = jnp.einsum('bhtk,bhkv->bhtv', qc * jnp.exp(b), S)
        diff = b[:, :, :, None, :] - b[:, :, None, :, :]
        decay = jnp.where(incl, jnp.exp(jnp.where(incl, diff, 0.0)), 0.0)
        scores = jnp.einsum('bhtk,bhtsk,bhsk->bhts', qc, decay, kc)
        o_intra = jnp.einsum('bhts,bhsv->bhtv', scores, vc)
        b_last = b[:, :, -1, :]
        S_new = (jnp.exp(b_last)[..., None] * S
                 + jnp.einsum('bhsk,bhsv->bhkv', kc * jnp.exp(b_last[:, :, None, :] - b), vc))
        return S_new, o_inter + o_intra

    s_fin, o = lax.scan(step, s0, (chunks(q), chunks(k), chunks(v), chunks(log_f)))
    return o.transpose(1, 0, 3, 2, 4).reshape(B, T, H, dv), s_fin


def hgrn2_bidir(q_raw, i_raw, zf_raw, zb_raw, s0f, s0b, lb):
    B, T, _ = q_raw.shape

    def heads(a):
        return a.astype(jnp.float32).reshape(B, T, HGRN_HEADS, HEAD_DIM)

    def rev(a):
        return jnp.flip(a, axis=1)

    q = jax.nn.silu(heads(q_raw))
    v = heads(i_raw)
    lf_f, k_f = hgrn_gates(heads(zf_raw), lb[0].reshape(HGRN_HEADS, HEAD_DIM))
    lf_b, k_b = hgrn_gates(heads(zb_raw), lb[1].reshape(HGRN_HEADS, HEAD_DIM))
    o_f, s_f = chunk_gated_scan(q, k_f, v, lf_f, s0f)
    o_b, s_b = chunk_gated_scan(rev(q), rev(k_b), rev(v), rev(lf_b), s0b)
    return o_f + rev(o_b), s_f, s_b


def hgrn2_readout(o, g_raw, g_norm):
    B, T = g_raw.shape[:2]
    o = rms_norm(o, g_norm).reshape(B, T, HGRN_WIDTH)
    return (o * jax.nn.silu(g_raw.astype(jnp.float32))).astype(g_raw.dtype)


def hybrid_mixer(hx, hc, w_in, w_fnet, b_fnet, rpb, lb, g_hn, w_out, cos, sin, rows, with_ctx_out):
    B, L, _ = hx.shape
    Lc = hc.shape[1]
    ax, qx, kx, vx, hqx, hix, hfx, hbx, hgx = split_proj(hx @ w_in)
    ac, qc, kc, vc, hqc, hic, hfc, hbc, hgc = split_proj(hc @ w_in)

    def heads(a, t):
        return a.astype(jnp.float32).reshape(B, t, NA_HEADS, HEAD_DIM)

    kc_h, vc_h = heads(kc, Lc), heads(vc, Lc)
    na_x = neighborhood_attention(apply_axial_rope(heads(qx, L), cos, sin),
                                  apply_axial_rope(heads(kx, L), cos, sin),
                                  heads(vx, L), kc_h, vc_h, rpb, rows)
    zeros = jnp.zeros((B, HGRN_HEADS, HEAD_DIM, HEAD_DIM), jnp.float32)
    oc, s_fc, s_bc = hgrn2_bidir(hqc, hic, hfc, hbc, zeros, zeros, lb)
    ox, _, _ = hgrn2_bidir(hqx, hix, hfx, hbx, s_fc, s_bc, lb)
    yx = jnp.concatenate([fourier_mix(ax, w_fnet, b_fnet), na_x.astype(hx.dtype),
                          hgrn2_readout(ox, hgx, g_hn)], axis=-1) @ w_out
    if not with_ctx_out:
        return yx, None
    na_c = context_attention(heads(qc, Lc), kc_h, vc_h)
    yc = jnp.concatenate([fourier_mix(ac, w_fnet, b_fnet), na_c.astype(hc.dtype),
                          hgrn2_readout(oc, hgc, g_hn)], axis=-1) @ w_out
    return yx, yc


def routed_experts(h, w_router, b_router, w_gu, b_gu, w_dn, b_dn):
    N, D = h.shape
    logits = (h @ w_router + b_router).astype(jnp.float32)
    top_logits, top_idx = lax.top_k(logits, TOP_K)
    top_w = jax.nn.softmax(top_logits, axis=-1)
    n_pairs = N * TOP_K
    flat_e = top_idx.reshape(n_pairs)
    flat_tok = jnp.repeat(jnp.arange(N, dtype=jnp.int32), TOP_K)
    flat_w = top_w.reshape(n_pairs)
    order = jnp.argsort(flat_e)
    e_sorted = flat_e[order]
    counts = jnp.zeros((N_EXPERTS,), jnp.int32).at[flat_e].add(1)
    padded = (counts + MOE_BLOCK - 1) // MOE_BLOCK * MOE_BLOCK
    pad_end = jnp.cumsum(padded)
    pad_start = pad_end - padded
    first = jnp.cumsum(counts) - counts
    dest = pad_start[e_sorted] + (jnp.arange(n_pairs, dtype=jnp.int32) - first[e_sorted])
    n_blocks = -(-(n_pairs + N_EXPERTS * (MOE_BLOCK - 1)) // MOE_BLOCK)
    n_rows = n_blocks * MOE_BLOCK
    row_tok = jnp.full((n_rows,), N, jnp.int32).at[dest].set(flat_tok[order])
    row_w = jnp.zeros((n_rows,), jnp.float32).at[dest].set(flat_w[order])
    block_e = jnp.minimum(jnp.searchsorted(pad_end, jnp.arange(n_blocks, dtype=jnp.int32) * MOE_BLOCK,
                                           side='right'), N_EXPERTS - 1)
    h_pad = jnp.concatenate([h, jnp.zeros((1, D), h.dtype)], axis=0)

    def expert_block(args):
        tok, e, wts = args
        gu = h_pad[tok] @ w_gu[e] + b_gu[e]
        gate = jnp.minimum(gu[:, 0::2], SWIGLU_LIMIT)
        up = jnp.clip(gu[:, 1::2], -SWIGLU_LIMIT, SWIGLU_LIMIT)
        act = gate * jax.nn.sigmoid(SWIGLU_ALPHA * gate) * (up + 1.0)
        return (act @ w_dn[e] + b_dn[e]) * wts[:, None].astype(act.dtype)

    y = lax.map(expert_block, (row_tok.reshape(n_blocks, MOE_BLOCK), block_e,
                               row_w.reshape(n_blocks, MOE_BLOCK)))
    out = jax.ops.segment_sum(y.reshape(n_rows, D), row_tok, num_segments=N + 1)
    return out[:N].astype(h.dtype)


def setup_inputs(seed: int = 0) -> dict:
    key = jax.random.key(seed)
    ks = jax.random.split(key, 23)
    f32 = jnp.float32

    def nrm(k, shape, s):
        return jax.random.normal(k, shape, f32) * s

    return {
        'x': nrm(ks[0], (BATCH, SEQ, D_MODEL), 1.0),
        'c': nrm(ks[1], (BATCH, D_MODEL), 1.0),
        'ctx': nrm(ks[2], (BATCH, CTX_LEN, D_MODEL), 1.0),
        'c_ctx': nrm(ks[3], (D_MODEL,), 1.0),
        'w_ada': nrm(ks[4], (DEPTH, D_MODEL, 6 * D_MODEL), 0.5 * D_MODEL ** -0.5),
        'b_ada': nrm(ks[5], (DEPTH, 6 * D_MODEL), 0.02),
        'norm_pre_mix': 1.0 + nrm(ks[6], (DEPTH, D_MODEL), 0.05),
        'norm_post_mix': 1.0 + nrm(ks[7], (DEPTH, D_MODEL), 0.05),
        'norm_pre_ffn': 1.0 + nrm(ks[8], (DEPTH, D_MODEL), 0.05),
        'norm_post_ffn': 1.0 + nrm(ks[9], (DEPTH, D_MODEL), 0.05),
        'w_in': nrm(ks[10], (DEPTH, D_MODEL, PROJ_WIDTH), D_MODEL ** -0.5),
        'w_fnet': nrm(ks[11], (DEPTH, FNET_WIDTH, FNET_WIDTH), FNET_WIDTH ** -0.5),
        'b_fnet': nrm(ks[12], (DEPTH, FNET_WIDTH), 0.02),
        'na_rpb': nrm(ks[13], (DEPTH, NA_HEADS, 2 * NA_KR - 1, 2 * NA_KC - 1), 0.5),
        'hgrn_lb_logits': nrm(ks[14], (DEPTH, 2, HGRN_WIDTH), 1.0),
        'hgrn_out_norm': 1.0 + nrm(ks[15], (DEPTH, HEAD_DIM), 0.05),
        'w_out': nrm(ks[16], (DEPTH, MIX_WIDTH, D_MODEL), MIX_WIDTH ** -0.5),
        'w_router': nrm(ks[17], (DEPTH, D_MODEL, N_EXPERTS), D_MODEL ** -0.5),
        'b_router': nrm(ks[18], (DEPTH, N_EXPERTS), 0.01),
        'w_gate_up': nrm(ks[19], (DEPTH, N_EXPERTS, D_MODEL, 2 * EXPERT_FF), D_MODEL ** -0.5),
        'b_gate_up': nrm(ks[20], (DEPTH, N_EXPERTS, 2 * EXPERT_FF), 0.01),
        'w_down': nrm(ks[21], (DEPTH, N_EXPERTS, EXPERT_FF, D_MODEL), EXPERT_FF ** -0.5),
        'b_down': nrm(ks[22], (DEPTH, N_EXPERTS, D_MODEL), 0.01),
    }


def reference(x, c, ctx, c_ctx, w_ada, b_ada, norm_pre_mix, norm_post_mix, norm_pre_ffn, norm_post_ffn,
              w_in, w_fnet, b_fnet, na_rpb, hgrn_lb_logits, hgrn_out_norm, w_out,
              w_router, b_router, w_gate_up, b_gate_up, w_down, b_down):
    B, L, D = x.shape
    rows = L // GRID_W
    cos, sin = axial_rope_tables(L)
    p_lb = jax.nn.softmax(hgrn_lb_logits.astype(jnp.float32), axis=0)
    lower_bounds = jnp.cumsum(p_lb, axis=0) - p_lb[0]
    for layer in range(DEPTH):
        last = layer == DEPTH - 1
        sh1, sc1, g1, sh2, sc2, g2 = [m[:, None, :] for m in adaln(c, w_ada[layer], b_ada[layer])]
        csh1, csc1, cg1, csh2, csc2, cg2 = adaln(c_ctx, w_ada[layer], b_ada[layer])
        hx = rms_norm(x, norm_pre_mix[layer]) * (1.0 + sc1) + sh1
        hc = rms_norm(ctx, norm_pre_mix[layer]) * (1.0 + csc1) + csh1
        yx, yc = hybrid_mixer(hx, hc, w_in[layer], w_fnet[layer], b_fnet[layer], na_rpb[layer],
                              lower_bounds[layer], hgrn_out_norm[layer], w_out[layer], cos, sin, rows,
                              not last)
        x = x + g1 * rms_norm(yx, norm_post_mix[layer])
        hx2 = (rms_norm(x, norm_pre_ffn[layer]) * (1.0 + sc2) + sh2).reshape(B * L, D)
        if last:
            fx = routed_experts(hx2, w_router[layer], b_router[layer], w_gate_up[layer], b_gate_up[layer],
                                w_down[layer], b_down[layer])
        else:
            ctx = ctx + cg1 * rms_norm(yc, norm_post_mix[layer])
            Lc = ctx.shape[1]
            hc2 = (rms_norm(ctx, norm_pre_ffn[layer]) * (1.0 + csc2) + csh2).reshape(B * Lc, D)
            f_all = routed_experts(jnp.concatenate([hx2, hc2], axis=0), w_router[layer], b_router[layer],
                                   w_gate_up[layer], b_gate_up[layer], w_down[layer], b_down[layer])
            fx = f_all[:B * L]
            ctx = ctx + cg2 * rms_norm(f_all[B * L:].reshape(B, Lc, D), norm_post_ffn[layer])
        x = x + g2 * rms_norm(fx.reshape(B, L, D), norm_post_ffn[layer])
    return x
```

```python
import functools

import numpy as np
import jax
import jax.numpy as jnp
from jax import lax
from jax.experimental import pallas as pl
from jax.experimental.pallas import tpu as pltpu

F32 = jnp.float32
BF16 = jnp.bfloat16
I32 = jnp.int32

GRID_W = 64
HEAD_DIM = 128
NA_KR = 8
NA_KC = 16
NA_QROWS = 2
ROPE_THETA = 10000.0
N_EXPERTS = 32
TOP_K = 4
SWIGLU_LIMIT = 7.0
SWIGLU_ALPHA = 1.702
RMS_EPS = 1e-6
MASK_VALUE = -1e30
GATE_FLOOR = 1e-30

V7X_VMEM_BYTES = 64 * 1024 * 1024
V7X_LANES = 128
V7X_MXU_DIM = 256

ROW_TILE = 256
MOE_TILE = 256
HGRN_CHUNK = 64
HGRN_SUB = 16
HGRN_SAFE_DECAY = 80.0


def _cparams(vmem_bytes, n_grid):
    limit = int(min(max(vmem_bytes * 5 // 4 + (4 << 20), 32 << 20), V7X_VMEM_BYTES - (4 << 20)))
    return pltpu.CompilerParams(dimension_semantics=("arbitrary",) * n_grid, vmem_limit_bytes=limit)


def _dot(a, b):
    return jnp.dot(a, b, preferred_element_type=F32)


def _dot_nt(a, b):
    return lax.dot_general(a, b, (((1,), (1,)), ((), ())), preferred_element_type=F32)


def _dot_tn(a, b):
    return lax.dot_general(a, b, (((0,), (0,)), ((), ())), preferred_element_type=F32)


def _silu(x):
    return x * jax.nn.sigmoid(x)


def _rms(x, g):
    return x * lax.rsqrt(jnp.mean(x * x, axis=-1, keepdims=True) + RMS_EPS) * g


def _ada_kernel(c_ref, w_ref, b_ref, o_ref):
    a = _silu(c_ref[...])
    a_hi = a.astype(BF16)
    a_lo = (a - a_hi.astype(F32)).astype(BF16)
    w = w_ref[0].astype(BF16)
    o_ref[0] = _dot(a_hi, w) + _dot(a_lo, w) + b_ref[0]


def _ada(cvec, w_ada, b_ada):
    depth, d, n = w_ada.shape
    tn = 512
    return pl.pallas_call(
        _ada_kernel,
        out_shape=jax.ShapeDtypeStruct((depth, 8, n), F32),
        grid=(depth, n // tn),
        in_specs=[pl.BlockSpec((8, d), lambda l, j: (0, 0)),
                  pl.BlockSpec((1, d, tn), lambda l, j: (l, 0, j)),
                  pl.BlockSpec((1, 1, tn), lambda l, j: (l, 0, j))],
        out_specs=pl.BlockSpec((1, 8, tn), lambda l, j: (l, 0, j)),
        compiler_params=_cparams(2 * d * tn * 4 + d * tn * 2, 2),
        name="adaln",
    )(cvec, w_ada, b_ada.reshape(depth, 1, n))


def _pre_kernel(x_ref, mod_ref, g_ref, h_ref):
    y = _rms(x_ref[...], g_ref[...])
    h_ref[...] = (y * (1.0 + mod_ref[0, 1:2, :]) + mod_ref[0, 0:1, :]).astype(h_ref.dtype)


def _mid_kernel(x_ref, y_ref, mod_ref, gpost_ref, gpre_ref, wr_ref, br_ref,
                x1_ref, h2_ref, idx_ref, tw_ref):
    x1 = x_ref[...] + mod_ref[0, 2:3, :] * _rms(y_ref[...].astype(F32), gpost_ref[...])
    x1_ref[...] = x1
    h2 = _rms(x1, gpre_ref[...]) * (1.0 + mod_ref[0, 4:5, :]) + mod_ref[0, 3:4, :]
    h2_ref[...] = h2
    logits = jnp.dot(h2, wr_ref[...], precision=lax.Precision.HIGHEST,
                     preferred_element_type=F32) + br_ref[...]
    lane = lax.broadcasted_iota(I32, logits.shape, 1)
    idx_acc = jnp.zeros(logits.shape, I32)
    top_acc = jnp.full(logits.shape, -jnp.inf, F32)
    work = logits
    for r in range(TOP_K):
        m = jnp.max(work, axis=-1, keepdims=True)
        sel = jnp.min(jnp.where(work == m, lane, V7X_LANES), axis=-1, keepdims=True)
        idx_acc = jnp.where(lane == r, sel, idx_acc)
        top_acc = jnp.where(lane == r, m, top_acc)
        work = jnp.where(lane == sel, -jnp.inf, work)
    e = jnp.exp(top_acc - jnp.max(top_acc, axis=-1, keepdims=True))
    idx_ref[...] = idx_acc
    tw_ref[...] = e / jnp.sum(e, axis=-1, keepdims=True)


def _end_kernel(x1_ref, f_ref, mod_ref, gpost_ref, modn_ref, gpren_ref, x2_ref, h_ref):
    x2 = x1_ref[...] + mod_ref[0, 5:6, :] * _rms(f_ref[...], gpost_ref[...])
    x2_ref[...] = x2
    y = _rms(x2, gpren_ref[...])
    h_ref[...] = (y * (1.0 + modn_ref[0, 1:2, :]) + modn_ref[0, 0:1, :]).astype(h_ref.dtype)


def _final_kernel(x1_ref, f_ref, mod_ref, gpost_ref, x2_ref):
    x2_ref[...] = x1_ref[...] + mod_ref[0, 5:6, :] * _rms(f_ref[...], gpost_ref[...])


def _row_specs(rows, d, n_lat_tiles):
    row = lambda w: pl.BlockSpec((ROW_TILE, w), lambda i: (i, 0))
    mod = pl.BlockSpec((1, 8, d), lambda i: (jnp.minimum(i // n_lat_tiles, 1), 0, 0))
    vec = lambda w: pl.BlockSpec((1, w), lambda i: (0, 0))
    return row, mod, vec


def _pre(x, mods, g, n_lat):
    rows, d = x.shape
    row, mod, vec = _row_specs(rows, d, n_lat // ROW_TILE)
    return pl.pallas_call(
        _pre_kernel, out_shape=jax.ShapeDtypeStruct((rows, d), BF16),
        grid=(rows // ROW_TILE,), in_specs=[row(d), mod, vec(d)], out_specs=row(d),
        compiler_params=_cparams(ROW_TILE * d * 24, 1), name="pre_norm",
    )(x, mods, g.reshape(1, d))


def _mid(x, y, mods, gpost, gpre, w_router, b_router, n_lat):
    rows, d = x.shape
    row, mod, vec = _row_specs(rows, d, n_lat // ROW_TILE)
    wr = jnp.zeros((d, V7X_LANES), F32).at[:, :N_EXPERTS].set(w_router)
    br = jnp.full((1, V7X_LANES), -jnp.inf, F32).at[0, :N_EXPERTS].set(b_router)
    return pl.pallas_call(
        _mid_kernel,
        out_shape=(jax.ShapeDtypeStruct((rows, d), F32), jax.ShapeDtypeStruct((rows, d), F32),
                   jax.ShapeDtypeStruct((rows, V7X_LANES), I32), jax.ShapeDtypeStruct((rows, V7X_LANES), F32)),
        grid=(rows // ROW_TILE,),
        in_specs=[row(d), row(d), mod, vec(d), vec(d),
                  pl.BlockSpec((d, V7X_LANES), lambda i: (0, 0)), vec(V7X_LANES)],
        out_specs=(row(d), row(d), row(V7X_LANES), row(V7X_LANES)),
        compiler_params=_cparams(ROW_TILE * d * 48 + d * V7X_LANES * 8, 1), name="post_mix_router",
    )(x, y, mods, gpost.reshape(1, d), gpre.reshape(1, d), wr, br)


def _end(x1, f, mods, gpost, mods_next, gpre_next, n_lat):
    rows, d = x1.shape
    row, mod, vec = _row_specs(rows, d, n_lat // ROW_TILE)
    return pl.pallas_call(
        _end_kernel,
        out_shape=(jax.ShapeDtypeStruct((rows, d), F32), jax.ShapeDtypeStruct((rows, d), BF16)),
        grid=(rows // ROW_TILE,),
        in_specs=[row(d), row(d), mod, vec(d), mod, vec(d)], out_specs=(row(d), row(d)),
        compiler_params=_cparams(ROW_TILE * d * 40, 1), name="post_ffn_pre_norm",
    )(x1, f, mods, gpost.reshape(1, d), mods_next, gpre_next.reshape(1, d))


def _final(x1, f, mods, gpost, n_lat):
    rows, d = x1.shape
    row, mod, vec = _row_specs(rows, d, n_lat // ROW_TILE)
    return pl.pallas_call(
        _final_kernel, out_shape=jax.ShapeDtypeStruct((rows, d), F32),
        grid=(rows // ROW_TILE,), in_specs=[row(d), row(d), mod, vec(d)], out_specs=row(d),
        compiler_params=_cparams(ROW_TILE * d * 32, 1), name="post_ffn",
    )(x1, f, mods, gpost.reshape(1, d))


def _mm_kernel(a_ref, b_ref, o_ref):
    o_ref[...] = _dot(a_ref[...], b_ref[...]).astype(o_ref.dtype)


def _pick_tile(n, prefs):
    for t in prefs:
        if n % t == 0:
            return t
    return n


def _matmul(a, b, out_dtype):
    m, k = a.shape
    n = b.shape[1]
    tm = _pick_tile(m, (768, 512, 256, 128))
    tn = _pick_tile(n, (1024, 512, 256, 128))
    vm = 2 * (tm * k * a.dtype.itemsize + k * tn * b.dtype.itemsize + tm * tn * 4) + tm * tn * 4
    return pl.pallas_call(
        _mm_kernel, out_shape=jax.ShapeDtypeStruct((m, n), out_dtype),
        grid=(n // tn, m // tm),
        in_specs=[pl.BlockSpec((tm, k), lambda j, i: (i, 0)), pl.BlockSpec((k, tn), lambda j, i: (0, j))],
        out_specs=pl.BlockSpec((tm, tn), lambda j, i: (i, j)),
        compiler_params=_cparams(vm, 2), name="matmul",
    )(a, b)


def _dft_cos_sin(n):
    i = jnp.arange(n, dtype=I32)
    ang = ((i[:, None] * i[None, :]) % n).astype(F32) * (2.0 * np.pi / n)
    s = 1.0 / np.sqrt(n)
    return jnp.cos(ang) * s, jnp.sin(ang) * s


def _fft1_kernel(x_ref, f_ref, twr_ref, twi_ref, o_ref):
    n1 = f_ref.shape[1]
    tb, _, w = o_ref.shape[1:]
    res = _dot(f_ref[...], x_ref[...])
    for j in range(tb):
        ar = res[:n1, j * w:(j + 1) * w]
        ai = res[n1:, j * w:(j + 1) * w]
        wr = jnp.tile(twr_ref[j], (1, w // V7X_LANES))
        wi = jnp.tile(twi_ref[j], (1, w // V7X_LANES))
        o_ref[0, j] = (ar * wr - ai * wi).astype(o_ref.dtype)
        o_ref[1, j] = (ar * wi + ai * wr).astype(o_ref.dtype)


def _fft_apply_kernel(m_ref, a_ref, o_ref):
    o_ref[...] = _dot(m_ref[...], a_ref[...]).astype(o_ref.dtype)


def _fft3_kernel(x_ref, cb_ref, sb_ref, wf_ref, bf_ref, o_ref):
    z = _dot(x_ref[0], cb_ref[...]) + _dot(x_ref[1], sb_ref[...])
    o_ref[...] = (_dot(z.astype(BF16), wf_ref[...]) + bf_ref[...]).astype(o_ref.dtype)


def _fft_apply(mat, a, tc):
    r, k = mat.shape
    n = a.shape[1]
    return pl.pallas_call(
        _fft_apply_kernel, out_shape=jax.ShapeDtypeStruct((r, n), BF16),
        grid=(n // tc,),
        in_specs=[pl.BlockSpec((r, k), lambda j: (0, 0)), pl.BlockSpec((k, tc), lambda j: (0, j))],
        out_specs=pl.BlockSpec((r, tc), lambda j: (0, j)),
        compiler_params=_cparams(2 * (r * k * 2 + k * tc * 2 + r * tc * 2) + r * tc * 4, 1), name="dft_apply",
    )(mat, a)


def _fourier_mix(ax, w_fnet, b_fnet, n_lat):
    rows, w = ax.shape
    n_ctx = rows - n_lat
    n1 = 64
    n2 = n_lat // n1
    tb = 8
    c1, s1 = _dft_cos_sin(n1)
    f1 = jnp.concatenate([c1, -s1], axis=0).astype(BF16)
    bi = jnp.arange(n2, dtype=I32)[:, None]
    ci = jnp.arange(n1, dtype=I32)[None, :]
    tang = ((bi * ci) % n_lat).astype(F32) * (2.0 * np.pi / n_lat)
    twr = jnp.broadcast_to(jnp.cos(tang)[:, :, None], (n2, n1, V7X_LANES))
    twi = jnp.broadcast_to(-jnp.sin(tang)[:, :, None], (n2, n1, V7X_LANES))
    x2d = ax[:n_lat].reshape(n1, n2 * w)
    a = pl.pallas_call(
        _fft1_kernel, out_shape=jax.ShapeDtypeStruct((2, n2, n1, w), BF16),
        grid=(n2 // tb,),
        in_specs=[pl.BlockSpec((n1, tb * w), lambda j: (0, j)),
                  pl.BlockSpec((2 * n1, n1), lambda j: (0, 0)),
                  pl.BlockSpec((tb, n1, V7X_LANES), lambda j: (j, 0, 0)),
                  pl.BlockSpec((tb, n1, V7X_LANES), lambda j: (j, 0, 0))],
        out_specs=pl.BlockSpec((2, tb, n1, w), lambda j: (0, j, 0, 0)),
        compiler_params=_cparams(2 * (n1 * tb * w * 2 + 2 * tb * n1 * w * 2) + 3 * n1 * tb * w * 8, 1),
        name="fft_stage1",
    )(x2d, f1, twr, twi)
    c2, s2 = _dft_cos_sin(n2)
    m2 = jnp.concatenate([jnp.concatenate([c2, s2], axis=1),
                          jnp.concatenate([-s2, c2], axis=1)], axis=0).astype(BF16)
    xl = _fft_apply(m2, a.reshape(2 * n2, n1 * w), 4096).reshape(2, n_lat, w)
    cc, sc = _dft_cos_sin(n_ctx)
    mc = jnp.concatenate([cc, -sc], axis=0).astype(BF16)
    xc = _fft_apply(mc, ax[n_lat:], w).reshape(2, n_ctx, w)
    xall = jnp.concatenate([xl, xc], axis=1)
    cg, sg = _dft_cos_sin(HEAD_DIM)
    eye = jnp.eye(w // HEAD_DIM, dtype=F32)
    cb = jnp.kron(eye, cg).astype(BF16)
    sb = jnp.kron(eye, sg).astype(BF16)
    full = lambda r, c: pl.BlockSpec((r, c), lambda i: (0, 0))
    return pl.pallas_call(
        _fft3_kernel, out_shape=jax.ShapeDtypeStruct((rows, w), BF16),
        grid=(rows // ROW_TILE,),
        in_specs=[pl.BlockSpec((2, ROW_TILE, w), lambda i: (0, i, 0)),
                  full(w, w), full(w, w), full(w, w), full(1, w)],
        out_specs=pl.BlockSpec((ROW_TILE, w), lambda i: (i, 0)),
        compiler_params=_cparams(2 * (3 * w * w * 2 + 3 * ROW_TILE * w * 2) + ROW_TILE * w * 12, 1),
        name="fft_channel_linear",
    )(xall, cb, sb, w_fnet.astype(BF16), b_fnet.reshape(1, w))


def _na_tables(rows):
    kr = min(NA_KR, rows)
    nb = min(kr + NA_QROWS - 1, rows)
    nqb = rows // NA_QROWS
    r0 = np.arange(nqb) * NA_QROWS
    band0 = np.minimum(np.clip(r0 - kr // 2, 0, rows - kr), rows - nb)
    band_rows = band0[:, None] + np.arange(nb)[None, :]
    q_row = np.repeat(r0[:, None] + np.arange(NA_QROWS)[None, :], GRID_W, axis=1)
    q_col = np.tile(np.arange(GRID_W), NA_QROWS)
    k_row = np.repeat(band_rows, GRID_W, axis=1)
    k_col = np.tile(np.arange(GRID_W), nb)
    win_r = np.clip(q_row - kr // 2, 0, rows - kr)[:, :, None]
    win_c = np.clip(q_col - NA_KC // 2, 0, GRID_W - NA_KC)[:, None]
    kro = k_row[:, None, :]
    col_ok = (k_col[None, :] >= win_c) & (k_col[None, :] < win_c + NA_KC)
    mask = (kro >= win_r) & (kro < win_r + kr) & col_ok[None]
    d_row = np.clip(kro - q_row[:, :, None] + NA_KR - 1, 0, 2 * NA_KR - 2)
    d_col = np.clip(k_col[None, :] - q_col[:, None] + NA_KC - 1, 0, 2 * NA_KC - 2)
    pats, pid = [], np.zeros(nqb, np.int32)
    for n in range(nqb):
        for p, (m0, d0) in enumerate(pats):
            if np.array_equal(m0, mask[n]) and np.array_equal(d0, d_row[n]):
                pid[n] = p
                break
        else:
            pid[n] = len(pats)
            pats.append((mask[n], d_row[n]))
    pmask = np.stack([p[0] for p in pats])
    pdrow = np.stack([p[1] for p in pats])
    return (band0 * GRID_W).astype(np.int32), pid, pmask, pdrow, d_col


def _na_kernel(pid_ref, st_ref, q_ref, k_ref, v_ref, cos_ref, sin_ref, bias_ref, o_ref, qs, ks,
               *, n_lat, n_ctx, nqb, qb, nk):
    scale = HEAD_DIM ** -0.5
    rt = 512 if n_lat % 512 == 0 else qb

    def rope_body(i, carry):
        r = pl.multiple_of(i * rt, rt)
        c = cos_ref[pl.ds(r, rt), :]
        s = sin_ref[pl.ds(r, rt), :]
        even = (lax.broadcasted_iota(I32, (rt, HEAD_DIM), 1) % 2) == 0

        def rot(x):
            xs = jnp.where(even, pltpu.roll(x, HEAD_DIM - 1, 1), pltpu.roll(x, 1, 1))
            return x * c + xs * s

        qs[pl.ds(r, rt), :] = (rot(q_ref[pl.ds(r, rt), :].astype(F32)) * scale).astype(BF16)
        ks[pl.ds(r, rt), :] = rot(k_ref[pl.ds(r, rt), :].astype(F32)).astype(BF16)
        return carry

    lax.fori_loop(0, n_lat // rt, rope_body, 0)
    kc = k_ref[n_lat:n_lat + n_ctx, :]
    vc = v_ref[n_lat:n_lat + n_ctx, :]

    def block_body(n, carry):
        r = pl.multiple_of(n * qb, qb)
        st = pl.multiple_of(st_ref[n], GRID_W)
        qblk = qs[pl.ds(r, qb), :]
        s_loc = _dot_nt(qblk, ks[pl.ds(st, nk), :]) + bias_ref[0, pid_ref[n]]
        s_ctx = _dot_nt(qblk, kc)
        m = jnp.maximum(jnp.max(s_loc, axis=-1, keepdims=True), jnp.max(s_ctx, axis=-1, keepdims=True))
        p_loc = jnp.exp(s_loc - m)
        p_ctx = jnp.exp(s_ctx - m)
        denom = jnp.sum(p_loc, axis=-1, keepdims=True) + jnp.sum(p_ctx, axis=-1, keepdims=True)
        o = _dot(p_loc.astype(BF16), v_ref[pl.ds(st, nk), :]) + _dot(p_ctx.astype(BF16), vc)
        o_ref[pl.ds(r, qb), :] = (o / denom).astype(o_ref.dtype)
        return carry

    lax.fori_loop(0, nqb, block_body, 0)
    s = _dot_nt(q_ref[n_lat:n_lat + n_ctx, :], kc) * scale
    p = jnp.exp(s - jnp.max(s, axis=-1, keepdims=True))
    o = _dot(p.astype(BF16), vc) / jnp.sum(p, axis=-1, keepdims=True)
    o_ref[n_lat:n_lat + n_ctx, :] = o.astype(o_ref.dtype)


def _na(p, col0, n_heads, rpb, cos_rep, sin_sgn, n_lat):
    rows = p.shape[0]
    n_ctx = rows - n_lat
    grid_rows = n_lat // GRID_W
    st, pid, pmask, pdrow, d_col = _na_tables(grid_rows)
    n_pat, qb, nk = pmask.shape
    nqb = grid_rows // NA_QROWS
    bias = rpb.astype(F32)[:, pdrow, d_col[None]]
    bias = jnp.where(pmask[None], bias, MASK_VALUE)
    head = lambda off: pl.BlockSpec((rows, HEAD_DIM), lambda h, *_: (0, col0 + off * n_heads + h))
    kern = functools.partial(_na_kernel, n_lat=n_lat, n_ctx=n_ctx, nqb=nqb, qb=qb, nk=nk)
    vm = 2 * (4 * rows * HEAD_DIM * 2 + n_pat * qb * nk * 4) + 2 * n_lat * HEAD_DIM * 4 \
        + 2 * n_lat * HEAD_DIM * 2 + 8 * qb * (nk + n_ctx) * 4
    return pl.pallas_call(
        kern, out_shape=jax.ShapeDtypeStruct((rows, n_heads * HEAD_DIM), BF16),
        grid_spec=pltpu.PrefetchScalarGridSpec(
            num_scalar_prefetch=2, grid=(n_heads,),
            in_specs=[head(0), head(1), head(2),
                      pl.BlockSpec(memory_space=pltpu.VMEM), pl.BlockSpec(memory_space=pltpu.VMEM),
                      pl.BlockSpec((1, n_pat, qb, nk), lambda h, *_: (h, 0, 0, 0))],
            out_specs=pl.BlockSpec((rows, HEAD_DIM), lambda h, *_: (0, h)),
            scratch_shapes=[pltpu.VMEM((n_lat, HEAD_DIM), BF16), pltpu.VMEM((n_lat, HEAD_DIM), BF16)]),
        compiler_params=_cparams(vm, 1), name="neighborhood_attention",
    )(jnp.asarray(pid), jnp.asarray(st), p, p, p, cos_rep, sin_sgn, bias)


def _hgrn_gates(z, lbv):
    f = lbv + (1.0 - lbv) * jax.nn.sigmoid(z)
    lf = jnp.log(jnp.maximum(f, GATE_FLOOR))
    kk = (1.0 - lbv) * jax.nn.sigmoid(-z)
    return lf, kk


def _hgrn_kernel(hq_ref, hi_ref, hf_ref, hb_ref, hg_ref, lb_ref, gn_ref, o_ref, acc, stf, stb,
                 *, n_lat, n_ctx):
    c_rows, sub = HGRN_CHUNK, HGRN_SUB
    n_sub = c_rows // sub
    dk = HEAD_DIM
    rows = n_lat + n_ctx
    lbf = lb_ref[0, 0:1, :]
    lbb = lb_ref[0, 1:2, :]

    def decay_body(i, mn):
        r = pl.multiple_of(i * c_rows, c_rows)
        lf_f, _ = _hgrn_gates(hf_ref[pl.ds(r, c_rows), :].astype(F32), lbf)
        lf_b, _ = _hgrn_gates(hb_ref[pl.ds(r, c_rows), :].astype(F32), lbb)
        for j in range(n_sub):
            mn = jnp.minimum(mn, jnp.sum(lf_f[j * sub:(j + 1) * sub], axis=0, keepdims=True))
            mn = jnp.minimum(mn, jnp.sum(lf_b[j * sub:(j + 1) * sub], axis=0, keepdims=True))
        return mn

    mn = lax.fori_loop(0, rows // c_rows, decay_body, jnp.zeros((1, dk), F32))
    safe = jnp.min(mn) >= -HGRN_SAFE_DECAY

    acc[...] = jnp.zeros(acc.shape, F32)
    stf[...] = jnp.zeros(stf.shape, F32)
    stb[...] = jnp.zeros(stb.shape, F32)

    t_i = lax.broadcasted_iota(I32, (c_rows, c_rows), 0)
    s_i = lax.broadcasted_iota(I32, (c_rows, c_rows), 1)
    row_i = lax.broadcasted_iota(I32, (c_rows, dk), 0)
    mt = lax.broadcasted_iota(I32, (c_rows, n_sub * c_rows), 0)
    mj = lax.broadcasted_iota(I32, (c_rows, n_sub * c_rows), 1)

    def chunk(r0, z_ref, lbv, st_ref, rev):
        lf, kk = _hgrn_gates(z_ref[pl.ds(r0, c_rows), :].astype(F32), lbv)
        q = _silu(hq_ref[pl.ds(r0, c_rows), :].astype(F32))
        v = hi_ref[pl.ds(r0, c_rows), :]
        hi = lf.astype(BF16)
        r1 = lf - hi.astype(F32)
        mid = r1.astype(BF16)
        lo = (r1 - mid.astype(F32)).astype(BF16)
        tri = jnp.where((s_i >= t_i) if rev else (s_i <= t_i), 1.0, 0.0).astype(BF16)
        b3 = _dot(tri, jnp.concatenate([hi, mid, lo], axis=1))
        b = b3[:, :dk] + b3[:, dk:2 * dk] + b3[:, 2 * dk:]
        zero = jnp.zeros((1, dk), F32)
        refs = []
        for i in range(n_sub):
            if rev:
                refs.append(b[(i + 1) * sub:(i + 1) * sub + 1] if i < n_sub - 1 else zero)
            else:
                refs.append(b[i * sub - 1:i * sub] if i > 0 else zero)
        ref_rows = refs[n_sub - 1]
        for i in range(n_sub - 2, -1, -1):
            ref_rows = jnp.where(row_i < (i + 1) * sub, refs[i], ref_rows)
        qd = (q * jnp.exp(b - ref_rows)).astype(BF16)
        ks = []
        for i in range(n_sub):
            valid = (row_i >= i * sub) if rev else (row_i < (i + 1) * sub)
            ks.append((kk * jnp.exp(jnp.where(valid, refs[i] - b, 0.0))).astype(BF16))
        scores = _dot_nt(qd, jnp.concatenate(ks, axis=0))
        js = mj % c_rows
        keep = ((mj // c_rows) == (mt // sub)) & ((js >= mt) if rev else (js <= mt))
        pm = jnp.where(keep, scores, 0.0).astype(BF16)
        st = st_ref[...]
        o = _dot(pm, jnp.concatenate([v] * n_sub, axis=0)) + _dot_nt((q * jnp.exp(b)).astype(BF16),
                                                                      st.astype(BF16))
        blast = b[0:1] if rev else b[c_rows - 1:c_rows]
        ke = (kk * jnp.exp(blast - b)).astype(BF16)
        st_ref[...] = st * jnp.exp(blast) + _dot_tn(v, ke)
        return o

    def fast_path():
        def run(base, n):
            def body(c, carry):
                rf = pl.multiple_of(base + c * c_rows, c_rows)
                rb = pl.multiple_of(base + (n - 1 - c) * c_rows, c_rows)
                o_f = chunk(rf, hf_ref, lbf, stf, False)
                acc[pl.ds(rf, c_rows), :] = acc[pl.ds(rf, c_rows), :] + o_f
                o_b = chunk(rb, hb_ref, lbb, stb, True)
                acc[pl.ds(rb, c_rows), :] = acc[pl.ds(rb, c_rows), :] + o_b
                return carry
            lax.fori_loop(0, n, body, 0)
        run(n_lat, n_ctx // c_rows)
        run(0, n_lat // c_rows)

    def slow_path():
        pack = 16
        sub_i = lax.broadcasted_iota(I32, (pack, dk), 0)
        row8 = lax.broadcasted_iota(I32, (8, dk), 0)

        def load_row(ref, t):
            r = pl.multiple_of((t // pack) * pack, pack)
            blk = ref[pl.ds(r, pack), :].astype(F32)
            return jnp.sum(jnp.where(sub_i == t - r, blk, 0.0), axis=0, keepdims=True)

        def run(base, n, z_ref, lbv, st_ref, rev):
            def body(i, carry):
                t = base + ((n - 1 - i) if rev else i)
                lf, kk = _hgrn_gates(load_row(z_ref, t), lbv)
                q = _silu(load_row(hq_ref, t))
                v = load_row(hi_ref, t)
                v8 = jnp.where(row8 == 0, v, 0.0).astype(BF16)
                k8 = jnp.where(row8 == 0, kk, 0.0).astype(BF16)
                st = st_ref[...] * jnp.exp(lf) + _dot_tn(v8, k8)
                st_ref[...] = st
                q8 = jnp.broadcast_to(q, (8, dk)).astype(BF16)
                o = _dot_nt(q8, st.astype(BF16))
                acc[pl.ds(t, 1), :] = acc[pl.ds(t, 1), :] + o[0:1]
                return carry
            lax.fori_loop(0, n, body, 0)
        run(n_lat, n_ctx, hf_ref, lbf, stf, False)
        run(0, n_lat, hf_ref, lbf, stf, False)
        run(n_lat, n_ctx, hb_ref, lbb, stb, True)
        run(0, n_lat, hb_ref, lbb, stb, True)

    lax.cond(safe, fast_path, slow_path)

    rt = ROW_TILE

    def readout(i, carry):
        r = pl.multiple_of(i * rt, rt)
        y = _rms(acc[pl.ds(r, rt), :], gn_ref[...])
        o_ref[pl.ds(r, rt), :] = (y * _silu(hg_ref[pl.ds(r, rt), :].astype(F32))).astype(o_ref.dtype)
        return carry

    lax.fori_loop(0, rows // rt, readout, 0)


def _hgrn(p, col0, n_heads, lb, g_norm, n_lat):
    rows = p.shape[0]
    n_ctx = rows - n_lat
    head = lambda off: pl.BlockSpec((rows, HEAD_DIM), lambda h: (0, col0 + off * n_heads + h))
    lbh = lb.reshape(2, n_heads, HEAD_DIM).transpose(1, 0, 2)
    kern = functools.partial(_hgrn_kernel, n_lat=n_lat, n_ctx=n_ctx)
    vm = 2 * 6 * rows * HEAD_DIM * 2 + rows * HEAD_DIM * 4 + 64 * HGRN_CHUNK * HEAD_DIM * 4
    return pl.pallas_call(
        kern, out_shape=jax.ShapeDtypeStruct((rows, n_heads * HEAD_DIM), BF16),
        grid=(n_heads,),
        in_specs=[head(0), head(1), head(2), head(3), head(4),
                  pl.BlockSpec((1, 2, HEAD_DIM), lambda h: (h, 0, 0)),
                  pl.BlockSpec((1, HEAD_DIM), lambda h: (0, 0))],
        out_specs=pl.BlockSpec((rows, HEAD_DIM), lambda h: (0, h)),
        scratch_shapes=[pltpu.VMEM((rows, HEAD_DIM), F32), pltpu.VMEM((HEAD_DIM, HEAD_DIM), F32),
                        pltpu.VMEM((HEAD_DIM, HEAD_DIM), F32)],
        compiler_params=_cparams(vm, 1), name="hgrn2_bidir",
    )(p, p, p, p, p, lbh, g_norm.reshape(1, HEAD_DIM))


def _expert_kernel(be_ref, nv_ref, x_ref, wgu_ref, bgu_ref, wdn_ref, bdn_ref, y_ref):
    ff = wdn_ref.shape[1]
    live = pl.program_id(0) < nv_ref[0]

    @pl.when(live)
    def _():
        gu = _dot(x_ref[...].astype(BF16), wgu_ref[0]) + bgu_ref[0]
        gate = jnp.minimum(gu[:, :ff], SWIGLU_LIMIT)
        up = jnp.clip(gu[:, ff:], -SWIGLU_LIMIT, SWIGLU_LIMIT)
        act = gate * jax.nn.sigmoid(SWIGLU_ALPHA * gate) * (up + 1.0)
        y_ref[...] = _dot(act.astype(BF16), wdn_ref[0]) + bdn_ref[0]

    @pl.when(jnp.logical_not(live))
    def _():
        y_ref[...] = jnp.zeros(y_ref.shape, y_ref.dtype)


def _experts(xg, block_e, n_valid, wgu, bgu, wdn, bdn):
    n_rows, d = xg.shape
    n_blocks = n_rows // MOE_TILE
    ff = wdn.shape[1]
    vm = 2 * (d * 2 * ff * 2 + ff * d * 2 + MOE_TILE * d * 8) + MOE_TILE * (2 * ff * 8 + d * 6)
    return pl.pallas_call(
        _expert_kernel, out_shape=jax.ShapeDtypeStruct((n_rows, d), F32),
        grid_spec=pltpu.PrefetchScalarGridSpec(
            num_scalar_prefetch=2, grid=(n_blocks,),
            in_specs=[pl.BlockSpec((MOE_TILE, d), lambda b, be, nv: (jnp.minimum(b, nv[0] - 1), 0)),
                      pl.BlockSpec((1, d, 2 * ff), lambda b, be, nv: (be[b], 0, 0)),
                      pl.BlockSpec((1, 1, 2 * ff), lambda b, be, nv: (be[b], 0, 0)),
                      pl.BlockSpec((1, ff, d), lambda b, be, nv: (be[b], 0, 0)),
                      pl.BlockSpec((1, 1, d), lambda b, be, nv: (be[b], 0, 0))],
            out_specs=pl.BlockSpec((MOE_TILE, d), lambda b, be, nv: (b, 0))),
        compiler_params=_cparams(vm, 1), name="expert_ffn",
    )(block_e, n_valid, xg, wgu, bgu, wdn, bdn)


def _route(top_idx, n_tok):
    flat_e = top_idx[:n_tok, :TOP_K].reshape(-1)
    n_pairs = flat_e.shape[0]
    onehot = (flat_e[:, None] == jnp.arange(N_EXPERTS, dtype=I32)[None, :]).astype(I32)
    csum = jnp.cumsum(onehot, axis=0)
    rank = jnp.take_along_axis(csum, flat_e[:, None], axis=1)[:, 0] - 1
    counts = csum[-1]
    padded = (counts + MOE_TILE - 1) // MOE_TILE * MOE_TILE
    pend = jnp.cumsum(padded)
    dest = (pend - padded)[flat_e] + rank
    n_blocks = -(-(n_pairs + N_EXPERTS * (MOE_TILE - 1)) // MOE_TILE)
    row_tok = jnp.full((n_blocks * MOE_TILE,), n_tok, I32).at[dest].set(jnp.arange(n_pairs, dtype=I32) // TOP_K)
    n_valid = (pend[-1] // MOE_TILE).astype(I32).reshape(1)
    blk = jnp.minimum(jnp.arange(n_blocks, dtype=I32), n_valid[0] - 1) * MOE_TILE
    block_e = jnp.minimum(jnp.searchsorted(pend, blk, side='right'), N_EXPERTS - 1).astype(I32)
    return dest.reshape(n_tok, TOP_K), row_tok, block_e, n_valid


def _moe(h2, top_idx, top_w, n_tok, wgu, bgu, wdn, bdn):
    dest, row_tok, block_e, n_valid = _route(top_idx, n_tok)
    xg = jnp.take(h2, row_tok, axis=0, mode='fill', fill_value=0)
    y = _experts(xg, block_e, n_valid, wgu, bgu, wdn, bdn)
    yg = jnp.take(y, dest.reshape(-1), axis=0).reshape(n_tok, TOP_K, -1)
    fx = jnp.sum(yg * top_w[:n_tok, :TOP_K, None], axis=1)
    pad = h2.shape[0] - n_tok
    return jnp.pad(fx, ((0, pad), (0, 0))) if pad else fx


def _rope_tables(n_tok):
    t = jnp.arange(n_tok, dtype=I32)
    row = (t // GRID_W).astype(F32)
    col = (t % GRID_W).astype(F32)
    pairs = HEAD_DIM // 4
    inv_freq = ROPE_THETA ** (-jnp.arange(pairs, dtype=F32) / pairs)
    ang = jnp.concatenate([row[:, None] * inv_freq, col[:, None] * inv_freq], axis=-1)
    cos_rep = jnp.repeat(jnp.cos(ang), 2, axis=-1)
    sin_sgn = jnp.stack([-jnp.sin(ang), jnp.sin(ang)], axis=-1).reshape(n_tok, HEAD_DIM)
    return cos_rep, sin_sgn


def kernel(x, c, ctx, c_ctx, w_ada, b_ada, norm_pre_mix, norm_post_mix, norm_pre_ffn, norm_post_ffn,
           w_in, w_fnet, b_fnet, na_rpb, hgrn_lb_logits, hgrn_out_norm, w_out,
           w_router, b_router, w_gate_up, b_gate_up, w_down, b_down):
    batch, n_lat, d = x.shape
    n_ctx = ctx.shape[1]
    assert batch == 1 and c.shape[0] == 1
    depth = w_ada.shape[0]
    fnet_w = w_fnet.shape[1]
    n_heads = na_rpb.shape[1]
    head_w = n_heads * HEAD_DIM
    ff = w_down.shape[2]

    cvec = jnp.zeros((8, d), F32).at[0].set(c[0]).at[1].set(c_ctx)
    ada = _ada(cvec, w_ada, b_ada)
    mods = jnp.pad(ada[:, :2].reshape(depth, 2, 6, d), ((0, 0), (0, 0), (0, 2), (0, 0)))
    cos_rep, sin_sgn = _rope_tables(n_lat)
    p_lb = jax.nn.softmax(hgrn_lb_logits.astype(F32), axis=0)
    lower_bounds = jnp.cumsum(p_lb, axis=0) - p_lb[0]

    xs = jnp.concatenate([x[0], ctx[0]], axis=0)
    h = _pre(xs, mods[0], norm_pre_mix[0], n_lat)
    for layer in range(depth):
        last = layer == depth - 1
        w_in_b = w_in[layer].astype(BF16)
        ax = _matmul(h, w_in_b[:, :fnet_w], BF16)
        p = _matmul(h, w_in_b[:, fnet_w:], BF16)
        yf = _fourier_mix(ax, w_fnet[layer], b_fnet[layer], n_lat)
        na = _na(p, 0, n_heads, na_rpb[layer], cos_rep, sin_sgn, n_lat)
        hg = _hgrn(p, 3 * n_heads, n_heads, lower_bounds[layer], hgrn_out_norm[layer], n_lat)
        mix = jnp.concatenate([yf, na, hg], axis=1)
        y = _matmul(mix, w_out[layer].astype(BF16), BF16)
        x1, h2, top_idx, top_w = _mid(xs, y, mods[layer], norm_post_mix[layer], norm_pre_ffn[layer],
                                      w_router[layer], b_router[layer], n_lat)
        wgu = jnp.concatenate([w_gate_up[layer][:, :, 0::2], w_gate_up[layer][:, :, 1::2]], axis=-1).astype(BF16)
        bgu = jnp.concatenate([b_gate_up[layer][:, 0::2], b_gate_up[layer][:, 1::2]], axis=-1)
        fx = _moe(h2, top_idx, top_w, n_lat if last else n_lat + n_ctx, wgu,
                  bgu.reshape(N_EXPERTS, 1, 2 * ff), w_down[layer].astype(BF16),
                  b_down[layer].reshape(N_EXPERTS, 1, d))
        if last:
            xs = _final(x1, fx, mods[layer], norm_post_ffn[layer], n_lat)
        else:
            xs, h = _end(x1, fx, mods[layer], norm_post_ffn[layer], mods[layer + 1],
                         norm_pre_mix[layer + 1], n_lat)
    return xs[:n_lat].reshape(batch, n_lat, d)
```

```python
import functools

import numpy as np
import jax
import jax.numpy as jnp
from jax import lax
from jax.experimental import pallas as pl
from jax.experimental.pallas import tpu as pltpu

F32 = jnp.float32
BF16 = jnp.bfloat16
I32 = jnp.int32

GRID_W = 64
HEAD_DIM = 128
NA_KR = 8
NA_KC = 16
NA_QROWS = 2
ROPE_THETA = 10000.0
N_EXPERTS = 32
TOP_K = 4
SWIGLU_LIMIT = 7.0
SWIGLU_ALPHA = 1.702
RMS_EPS = 1e-6
MASK_VALUE = -1e30
GATE_FLOOR = 1e-30

V7X_VMEM_BYTES = 64 * 1024 * 1024
V7X_LANES = 128
V7X_MXU_DIM = 256
BF16_SUBLANES = 16

ROW_TILE = 256
COMBINE_TILE = 128
MOE_TILE = 256
HGRN_CHUNK = 64
HGRN_SUB = 16
HGRN_SAFE_DECAY = 80.0
FFT_N1 = 64


def _cparams(vmem_bytes, n_grid):
    limit = int(min(max(vmem_bytes * 5 // 4 + (4 << 20), 32 << 20), V7X_VMEM_BYTES - (4 << 20)))
    return pltpu.CompilerParams(dimension_semantics=("arbitrary",) * n_grid, vmem_limit_bytes=limit)


def _dot(a, b):
    return jnp.dot(a, b, preferred_element_type=F32)


def _dot_nt(a, b):
    return lax.dot_general(a, b, (((1,), (1,)), ((), ())), preferred_element_type=F32)


def _dot_tn(a, b):
    return lax.dot_general(a, b, (((0,), (0,)), ((), ())), preferred_element_type=F32)


def _silu(x):
    return x * jax.nn.sigmoid(x)


def _rms(x, g):
    return x * lax.rsqrt(jnp.mean(x * x, axis=-1, keepdims=True) + RMS_EPS) * g


def _ada_kernel(c_ref, w_ref, b_ref, o_ref):
    a = _silu(c_ref[...])
    a_hi = a.astype(BF16)
    a_lo = (a - a_hi.astype(F32)).astype(BF16)
    w = w_ref[0].astype(BF16)
    o_ref[0] = _dot(a_hi, w) + _dot(a_lo, w) + b_ref[0]


def _ada(cvec, w_ada, b_ada):
    depth, d, n = w_ada.shape
    tn = 512
    return pl.pallas_call(
        _ada_kernel,
        out_shape=jax.ShapeDtypeStruct((depth, 8, n), F32),
        grid=(depth, n // tn),
        in_specs=[pl.BlockSpec((8, d), lambda l, j: (0, 0)),
                  pl.BlockSpec((1, d, tn), lambda l, j: (l, 0, j)),
                  pl.BlockSpec((1, 1, tn), lambda l, j: (l, 0, j))],
        out_specs=pl.BlockSpec((1, 8, tn), lambda l, j: (l, 0, j)),
        compiler_params=_cparams(2 * d * tn * 4 + d * tn * 2, 2),
        name="adaln",
    )(cvec, w_ada, b_ada.reshape(depth, 1, n))


def _pre_kernel(x_ref, mod_ref, g_ref, h_ref):
    y = _rms(x_ref[...], g_ref[...])
    h_ref[...] = (y * (1.0 + mod_ref[0, 1:2, :]) + mod_ref[0, 0:1, :]).astype(h_ref.dtype)


def _mid_kernel(x_ref, y_ref, mod_ref, gpost_ref, gpre_ref, wr_ref, br_ref,
                x1_ref, h2_ref, idx_ref, tw_ref):
    x1 = x_ref[...] + mod_ref[0, 2:3, :] * _rms(y_ref[...].astype(F32), gpost_ref[...])
    x1_ref[...] = x1
    h2 = _rms(x1, gpre_ref[...]) * (1.0 + mod_ref[0, 4:5, :]) + mod_ref[0, 3:4, :]
    h2_ref[...] = h2
    logits = jnp.dot(h2, wr_ref[...], precision=lax.Precision.HIGHEST,
                     preferred_element_type=F32) + br_ref[...]
    lane = lax.broadcasted_iota(I32, logits.shape, 1)
    idx_acc = jnp.zeros(logits.shape, I32)
    top_acc = jnp.full(logits.shape, -jnp.inf, F32)
    work = logits
    for r in range(TOP_K):
        m = jnp.max(work, axis=-1, keepdims=True)
        sel = jnp.min(jnp.where(work == m, lane, V7X_LANES), axis=-1, keepdims=True)
        idx_acc = jnp.where(lane == r, sel, idx_acc)
        top_acc = jnp.where(lane == r, m, top_acc)
        work = jnp.where(lane == sel, -jnp.inf, work)
    e = jnp.exp(top_acc - jnp.max(top_acc, axis=-1, keepdims=True))
    idx_ref[...] = idx_acc
    tw_ref[...] = e / jnp.sum(e, axis=-1, keepdims=True)


def _combine(yg_ref, tw_ref):
    tw = tw_ref[...]
    fx = tw[:, 0:1] * yg_ref[0]
    for k in range(1, TOP_K):
        fx = fx + tw[:, k:k + 1] * yg_ref[k]
    return fx


def _end_kernel(x1_ref, yg_ref, tw_ref, mod_ref, gpost_ref, modn_ref, gpren_ref, x2_ref, h_ref):
    x2 = x1_ref[...] + mod_ref[0, 5:6, :] * _rms(_combine(yg_ref, tw_ref), gpost_ref[...])
    x2_ref[...] = x2
    y = _rms(x2, gpren_ref[...])
    h_ref[...] = (y * (1.0 + modn_ref[0, 1:2, :]) + modn_ref[0, 0:1, :]).astype(h_ref.dtype)


def _final_kernel(x1_ref, yg_ref, tw_ref, mod_ref, gpost_ref, x2_ref):
    x2_ref[...] = x1_ref[...] + mod_ref[0, 5:6, :] * _rms(_combine(yg_ref, tw_ref), gpost_ref[...])


def _row_specs(tile, d, n_lat):
    row = lambda w: pl.BlockSpec((tile, w), lambda i: (i, 0))
    mod = pl.BlockSpec((1, 8, d), lambda i: (jnp.minimum(i // (n_lat // tile), 1), 0, 0))
    vec = lambda w: pl.BlockSpec((1, w), lambda i: (0, 0))
    return row, mod, vec


def _pre(x, mods, g, n_lat):
    rows, d = x.shape
    row, mod, vec = _row_specs(ROW_TILE, d, n_lat)
    return pl.pallas_call(
        _pre_kernel, out_shape=jax.ShapeDtypeStruct((rows, d), BF16),
        grid=(rows // ROW_TILE,), in_specs=[row(d), mod, vec(d)], out_specs=row(d),
        compiler_params=_cparams(ROW_TILE * d * 24, 1), name="pre_norm",
    )(x, mods, g.reshape(1, d))


def _mid(x, y, mods, gpost, gpre, w_router, b_router, n_lat):
    rows, d = x.shape
    row, mod, vec = _row_specs(ROW_TILE, d, n_lat)
    wr = jnp.zeros((d, V7X_LANES), F32).at[:, :N_EXPERTS].set(w_router)
    br = jnp.full((1, V7X_LANES), -jnp.inf, F32).at[0, :N_EXPERTS].set(b_router)
    return pl.pallas_call(
        _mid_kernel,
        out_shape=(jax.ShapeDtypeStruct((rows, d), F32), jax.ShapeDtypeStruct((rows, d), F32),
                   jax.ShapeDtypeStruct((rows, V7X_LANES), I32), jax.ShapeDtypeStruct((rows, V7X_LANES), F32)),
        grid=(rows // ROW_TILE,),
        in_specs=[row(d), row(d), mod, vec(d), vec(d),
                  pl.BlockSpec((d, V7X_LANES), lambda i: (0, 0)), vec(V7X_LANES)],
        out_specs=(row(d), row(d), row(V7X_LANES), row(V7X_LANES)),
        compiler_params=_cparams(ROW_TILE * d * 48 + d * V7X_LANES * 8, 1), name="post_mix_router",
    )(x, y, mods, gpost.reshape(1, d), gpre.reshape(1, d), wr, br)


def _end(x1, yg, top_w, mods, gpost, mods_next, gpre_next, n_lat):
    rows, d = x1.shape
    t = COMBINE_TILE
    row, mod, vec = _row_specs(t, d, n_lat)
    return pl.pallas_call(
        _end_kernel,
        out_shape=(jax.ShapeDtypeStruct((rows, d), F32), jax.ShapeDtypeStruct((rows, d), BF16)),
        grid=(rows // t,),
        in_specs=[row(d), pl.BlockSpec((TOP_K, t, d), lambda i: (0, i, 0)), row(V7X_LANES),
                  mod, vec(d), mod, vec(d)],
        out_specs=(row(d), row(d)),
        compiler_params=_cparams(t * d * (2 * 4 * (TOP_K + 2) + 2 * 2 + 16), 1), name="post_ffn_pre_norm",
    )(x1, yg, top_w, mods, gpost.reshape(1, d), mods_next, gpre_next.reshape(1, d))


def _final(x1, yg, top_w, mods, gpost, n_lat):
    d = x1.shape[1]
    t = COMBINE_TILE
    row, mod, vec = _row_specs(t, d, n_lat)
    return pl.pallas_call(
        _final_kernel, out_shape=jax.ShapeDtypeStruct((n_lat, d), F32),
        grid=(n_lat // t,),
        in_specs=[row(d), pl.BlockSpec((TOP_K, t, d), lambda i: (0, i, 0)), row(V7X_LANES), mod, vec(d)],
        out_specs=row(d),
        compiler_params=_cparams(t * d * (2 * 4 * (TOP_K + 2) + 16), 1), name="post_ffn",
    )(x1, yg, top_w, mods, gpost.reshape(1, d))


def _mm_kernel(a_ref, b_ref, o_ref, bscr):
    @pl.when(pl.program_id(1) == 0)
    def _():
        bscr[...] = b_ref[0].astype(BF16)

    o_ref[...] = _dot(a_ref[...], bscr[...]).astype(o_ref.dtype)


def _pick_tile(n, prefs):
    for t in prefs:
        if n % t == 0:
            return t
    return n


def _matmul(a, w3, layer, out_dtype):
    m, k = a.shape
    n = w3.shape[2]
    tm = _pick_tile(m, (768, 512, 256, 128))
    tn = _pick_tile(n, (512, 256, 128))
    vm = 2 * (tm * k * 2 + k * tn * 4 + tm * tn * 4) + k * tn * 2 + tm * tn * 4
    return pl.pallas_call(
        _mm_kernel, out_shape=jax.ShapeDtypeStruct((m, n), out_dtype),
        grid=(n // tn, m // tm),
        in_specs=[pl.BlockSpec((tm, k), lambda j, i: (i, 0)),
                  pl.BlockSpec((1, k, tn), lambda j, i: (layer, 0, j))],
        out_specs=pl.BlockSpec((tm, tn), lambda j, i: (i, j)),
        scratch_shapes=[pltpu.VMEM((k, tn), BF16)],
        compiler_params=_cparams(vm, 2), name="matmul",
    )(a, w3)


def _dft_cos_sin(n):
    i = jnp.arange(n, dtype=I32)
    ang = ((i[:, None] * i[None, :]) % n).astype(F32) * (2.0 * np.pi / n)
    s = 1.0 / np.sqrt(n)
    return jnp.cos(ang) * s, jnp.sin(ang) * s


def _fft1_kernel(x_ref, f_ref, twr_ref, twi_ref, o_ref):
    n1 = f_ref.shape[1]
    tb, _, w = o_ref.shape[1:]
    xt = pltpu.einshape("abw->baw", x_ref[...])
    for j in range(tb):
        res = _dot(f_ref[...], xt[j])
        ar = res[:n1]
        ai = res[n1:]
        wr = jnp.tile(twr_ref[j], (1, w // V7X_LANES))
        wi = jnp.tile(twi_ref[j], (1, w // V7X_LANES))
        o_ref[0, j] = (ar * wr - ai * wi).astype(o_ref.dtype)
        o_ref[1, j] = (ar * wi + ai * wr).astype(o_ref.dtype)


def _fft2_kernel(m_ref, a_ref, o_ref, sr, si):
    n2, tc = a_ref.shape[1:3]
    ar = pltpu.einshape("bcw->cbw", a_ref[0])
    ai = pltpu.einshape("bcw->cbw", a_ref[1])
    for c in range(tc):
        res = _dot(m_ref[...], jnp.concatenate([ar[c], ai[c]], axis=0))
        sr[c] = res[:n2].astype(sr.dtype)
        si[c] = res[n2:].astype(si.dtype)
    o_ref[0] = pltpu.einshape("cdw->dcw", sr[...])
    o_ref[1] = pltpu.einshape("cdw->dcw", si[...])


def _dft_dense_kernel(m_ref, a_ref, o_ref):
    o_ref[...] = _dot(m_ref[...], a_ref[...]).astype(o_ref.dtype)


def _fft3_kernel(x_ref, cb_ref, sb_ref, wf_ref, bf_ref, o_ref):
    z = _dot(x_ref[0], cb_ref[...]) + _dot(x_ref[1], sb_ref[...])
    o_ref[...] = (_dot(z.astype(BF16), wf_ref[...]) + bf_ref[...]).astype(o_ref.dtype)


def _fourier_mix(p, w, w_fnet, b_fnet, n_lat):
    rows, cols = p.shape
    n_ctx = rows - n_lat
    n1 = FFT_N1
    n2 = n_lat // n1
    tb = BF16_SUBLANES
    c1, s1 = _dft_cos_sin(n1)
    f1 = jnp.concatenate([c1, -s1], axis=0).astype(BF16)
    bi = jnp.arange(n2, dtype=I32)[:, None]
    ci = jnp.arange(n1, dtype=I32)[None, :]
    tang = ((bi * ci) % n_lat).astype(F32) * (2.0 * np.pi / n_lat)
    twr = jnp.broadcast_to(jnp.cos(tang)[:, :, None], (n2, n1, V7X_LANES))
    twi = jnp.broadcast_to(-jnp.sin(tang)[:, :, None], (n2, n1, V7X_LANES))
    a = pl.pallas_call(
        _fft1_kernel, out_shape=jax.ShapeDtypeStruct((2, n2, n1, w), BF16),
        grid=(n2 // tb,),
        in_specs=[pl.BlockSpec((n1, tb, w), lambda j: (0, j, 0)),
                  pl.BlockSpec((2 * n1, n1), lambda j: (0, 0)),
                  pl.BlockSpec((tb, n1, V7X_LANES), lambda j: (j, 0, 0)),
                  pl.BlockSpec((tb, n1, V7X_LANES), lambda j: (j, 0, 0))],
        out_specs=pl.BlockSpec((2, tb, n1, w), lambda j: (0, j, 0, 0)),
        compiler_params=_cparams(2 * 3 * n1 * tb * w * 2 + 8 * n1 * w * 4, 1), name="fft_stage1",
    )(p.reshape(rows // n2, n2, cols), f1, twr, twi)
    c2, s2 = _dft_cos_sin(n2)
    m2 = jnp.concatenate([jnp.concatenate([c2, s2], axis=1),
                          jnp.concatenate([-s2, c2], axis=1)], axis=0).astype(BF16)
    tc = BF16_SUBLANES
    xl = pl.pallas_call(
        _fft2_kernel, out_shape=jax.ShapeDtypeStruct((2, n2, n1, w), BF16),
        grid=(n1 // tc,),
        in_specs=[pl.BlockSpec((2 * n2, 2 * n2), lambda j: (0, 0)),
                  pl.BlockSpec((2, n2, tc, w), lambda j: (0, 0, j, 0))],
        out_specs=pl.BlockSpec((2, n2, tc, w), lambda j: (0, 0, j, 0)),
        scratch_shapes=[pltpu.VMEM((tc, n2, w), BF16), pltpu.VMEM((tc, n2, w), BF16)],
        compiler_params=_cparams(5 * 2 * n2 * tc * w * 2 + 8 * n2 * w * 4, 1), name="fft_stage2",
    )(m2, a).reshape(2, n_lat, w)
    cc, sc = _dft_cos_sin(n_ctx)
    mc = jnp.concatenate([cc, -sc], axis=0).astype(BF16)
    xc = pl.pallas_call(
        _dft_dense_kernel, out_shape=jax.ShapeDtypeStruct((2 * n_ctx, w), BF16),
        grid=(1,),
        in_specs=[pl.BlockSpec((2 * n_ctx, n_ctx), lambda j: (0, 0)),
                  pl.BlockSpec((n_ctx, w), lambda j: (n_lat // n_ctx, 0))],
        out_specs=pl.BlockSpec((2 * n_ctx, w), lambda j: (0, 0)),
        compiler_params=_cparams(16 * n_ctx * w, 1), name="dft_context",
    )(mc, p).reshape(2, n_ctx, w)
    xall = jnp.concatenate([xl, xc], axis=1)
    cg, sg = _dft_cos_sin(HEAD_DIM)
    eye = jnp.eye(w // HEAD_DIM, dtype=F32)
    cb = jnp.kron(eye, cg).astype(BF16)
    sb = jnp.kron(eye, sg).astype(BF16)
    full = lambda r, c: pl.BlockSpec((r, c), lambda i: (0, 0))
    return pl.pallas_call(
        _fft3_kernel, out_shape=jax.ShapeDtypeStruct((rows, w), BF16),
        grid=(rows // ROW_TILE,),
        in_specs=[pl.BlockSpec((2, ROW_TILE, w), lambda i: (0, i, 0)),
                  full(w, w), full(w, w), full(w, w), full(1, w)],
        out_specs=pl.BlockSpec((ROW_TILE, w), lambda i: (i, 0)),
        compiler_params=_cparams(2 * (3 * w * w * 2 + 3 * ROW_TILE * w * 2) + ROW_TILE * w * 12, 1),
        name="fft_channel_linear",
    )(xall, cb, sb, w_fnet.astype(BF16), b_fnet.reshape(1, w))


def _na_tables(rows):
    kr = min(NA_KR, rows)
    nb = min(kr + NA_QROWS - 1, rows)
    nqb = rows // NA_QROWS
    r0 = np.arange(nqb) * NA_QROWS
    band0 = np.minimum(np.clip(r0 - kr // 2, 0, rows - kr), rows - nb)
    band_rows = band0[:, None] + np.arange(nb)[None, :]
    q_row = np.repeat(r0[:, None] + np.arange(NA_QROWS)[None, :], GRID_W, axis=1)
    q_col = np.tile(np.arange(GRID_W), NA_QROWS)
    k_row = np.repeat(band_rows, GRID_W, axis=1)
    k_col = np.tile(np.arange(GRID_W), nb)
    win_r = np.clip(q_row - kr // 2, 0, rows - kr)[:, :, None]
    win_c = np.clip(q_col - NA_KC // 2, 0, GRID_W - NA_KC)[:, None]
    kro = k_row[:, None, :]
    col_ok = (k_col[None, :] >= win_c) & (k_col[None, :] < win_c + NA_KC)
    mask = (kro >= win_r) & (kro < win_r + kr) & col_ok[None]
    d_row = np.clip(kro - q_row[:, :, None] + NA_KR - 1, 0, 2 * NA_KR - 2)
    d_col = np.clip(k_col[None, :] - q_col[:, None] + NA_KC - 1, 0, 2 * NA_KC - 2)
    pats, pid = [], np.zeros(nqb, np.int32)
    for n in range(nqb):
        for p, (m0, d0) in enumerate(pats):
            if np.array_equal(m0, mask[n]) and np.array_equal(d0, d_row[n]):
                pid[n] = p
                break
        else:
            pid[n] = len(pats)
            pats.append((mask[n], d_row[n]))
    pmask = np.stack([p[0] for p in pats])
    pdrow = np.stack([p[1] for p in pats])
    return (band0 * GRID_W).astype(np.int32), pid, pmask, pdrow, d_col, nb


def _na_bias(rpb, pmask, pdrow, d_col, nb):
    n_heads = rpb.shape[0]
    n_pat, qb, nk = pmask.shape
    tiles = rpb.astype(F32)[:, :, d_col[:GRID_W, :GRID_W]]
    drow_small = pdrow[:, ::GRID_W, ::GRID_W]
    b6 = tiles[:, drow_small]
    bias = b6.transpose(0, 1, 2, 4, 3, 5).reshape(n_heads, n_pat, qb, nk)
    return jnp.where(pmask[None], bias, MASK_VALUE)


def _na_kernel(pid_ref, st_ref, q_ref, k_ref, v_ref, cos_ref, sin_ref, bias_ref, o_ref, qs, ks,
               *, n_lat, n_ctx, nqb, qb, nk):
    scale = HEAD_DIM ** -0.5
    rt = 512 if n_lat % 512 == 0 else qb

    def rope_body(i, carry):
        r = pl.multiple_of(i * rt, rt)
        c = cos_ref[pl.ds(r, rt), :]
        s = sin_ref[pl.ds(r, rt), :]
        even = (lax.broadcasted_iota(I32, (rt, HEAD_DIM), 1) % 2) == 0

        def rot(x):
            xs = jnp.where(even, pltpu.roll(x, HEAD_DIM - 1, 1), pltpu.roll(x, 1, 1))
            return x * c + xs * s

        qs[pl.ds(r, rt), :] = (rot(q_ref[pl.ds(r, rt), :].astype(F32)) * scale).astype(BF16)
        ks[pl.ds(r, rt), :] = rot(k_ref[pl.ds(r, rt), :].astype(F32)).astype(BF16)
        return carry

    lax.fori_loop(0, n_lat // rt, rope_body, 0)
    kc = k_ref[n_lat:n_lat + n_ctx, :]
    vc = v_ref[n_lat:n_lat + n_ctx, :]

    def block_body(n, carry):
        r = pl.multiple_of(n * qb, qb)
        st = pl.multiple_of(st_ref[n], GRID_W)
        qblk = qs[pl.ds(r, qb), :]
        s_loc = _dot_nt(qblk, ks[pl.ds(st, nk), :]) + bias_ref[0, pid_ref[n]]
        s_ctx = _dot_nt(qblk, kc)
        m = jnp.maximum(jnp.max(s_loc, axis=-1, keepdims=True), jnp.max(s_ctx, axis=-1, keepdims=True))
        p_loc = jnp.exp(s_loc - m)
        p_ctx = jnp.exp(s_ctx - m)
        denom = jnp.sum(p_loc, axis=-1, keepdims=True) + jnp.sum(p_ctx, axis=-1, keepdims=True)
        o = _dot(p_loc.astype(BF16), v_ref[pl.ds(st, nk), :]) + _dot(p_ctx.astype(BF16), vc)
        o_ref[pl.ds(r, qb), :] = (o / denom).astype(o_ref.dtype)
        return carry

    lax.fori_loop(0, nqb, block_body, 0)
    s = _dot_nt(q_ref[n_lat:n_lat + n_ctx, :], kc) * scale
    p = jnp.exp(s - jnp.max(s, axis=-1, keepdims=True))
    o = _dot(p.astype(BF16), vc) / jnp.sum(p, axis=-1, keepdims=True)
    o_ref[n_lat:n_lat + n_ctx, :] = o.astype(o_ref.dtype)


def _na(p, col0, n_heads, rpb, cos_rep, sin_sgn, n_lat):
    rows = p.shape[0]
    n_ctx = rows - n_lat
    grid_rows = n_lat // GRID_W
    st, pid, pmask, pdrow, d_col, nb = _na_tables(grid_rows)
    n_pat, qb, nk = pmask.shape
    nqb = grid_rows // NA_QROWS
    bias = _na_bias(rpb, pmask, pdrow, d_col, nb)
    head = lambda off: pl.BlockSpec((rows, HEAD_DIM), lambda h, *_: (0, col0 + off * n_heads + h))
    kern = functools.partial(_na_kernel, n_lat=n_lat, n_ctx=n_ctx, nqb=nqb, qb=qb, nk=nk)
    vm = 2 * (4 * rows * HEAD_DIM * 2 + n_pat * qb * nk * 4) + 2 * n_lat * HEAD_DIM * 4 \
        + 2 * n_lat * HEAD_DIM * 2 + 8 * qb * (nk + n_ctx) * 4
    return pl.pallas_call(
        kern, out_shape=jax.ShapeDtypeStruct((rows, n_heads * HEAD_DIM), BF16),
        grid_spec=pltpu.PrefetchScalarGridSpec(
            num_scalar_prefetch=2, grid=(n_heads,),
            in_specs=[head(0), head(1), head(2),
                      pl.BlockSpec(memory_space=pltpu.VMEM), pl.BlockSpec(memory_space=pltpu.VMEM),
                      pl.BlockSpec((1, n_pat, qb, nk), lambda h, *_: (h, 0, 0, 0))],
            out_specs=pl.BlockSpec((rows, HEAD_DIM), lambda h, *_: (0, h)),
            scratch_shapes=[pltpu.VMEM((n_lat, HEAD_DIM), BF16), pltpu.VMEM((n_lat, HEAD_DIM), BF16)]),
        compiler_params=_cparams(vm, 1), name="neighborhood_attention",
    )(jnp.asarray(pid), jnp.asarray(st), p, p, p, cos_rep, sin_sgn, bias)


def _hgrn_gates(z, lbv):
    f = lbv + (1.0 - lbv) * jax.nn.sigmoid(z)
    lf = jnp.log(jnp.maximum(f, GATE_FLOOR))
    kk = (1.0 - lbv) * jax.nn.sigmoid(-z)
    return lf, kk


def _hgrn_kernel(hq_ref, hi_ref, hf_ref, hb_ref, hg_ref, lb_ref, gn_ref, o_ref, acc, stf, stb,
                 *, n_lat, n_ctx):
    c_rows, sub = HGRN_CHUNK, HGRN_SUB
    n_sub = c_rows // sub
    dk = HEAD_DIM
    rows = n_lat + n_ctx
    lbf = lb_ref[0, 0:1, :]
    lbb = lb_ref[0, 1:2, :]

    def decay_body(i, mn):
        r = pl.multiple_of(i * c_rows, c_rows)
        lf_f, _ = _hgrn_gates(hf_ref[pl.ds(r, c_rows), :].astype(F32), lbf)
        lf_b, _ = _hgrn_gates(hb_ref[pl.ds(r, c_rows), :].astype(F32), lbb)
        for j in range(n_sub):
            mn = jnp.minimum(mn, jnp.sum(lf_f[j * sub:(j + 1) * sub], axis=0, keepdims=True))
            mn = jnp.minimum(mn, jnp.sum(lf_b[j * sub:(j + 1) * sub], axis=0, keepdims=True))
        return mn

    mn = lax.fori_loop(0, rows // c_rows, decay_body, jnp.zeros((1, dk), F32))
    safe = jnp.min(mn) >= -HGRN_SAFE_DECAY

    acc[...] = jnp.zeros(acc.shape, F32)
    stf[...] = jnp.zeros(stf.shape, F32)
    stb[...] = jnp.zeros(stb.shape, F32)

    t_i = lax.broadcasted_iota(I32, (c_rows, c_rows), 0)
    s_i = lax.broadcasted_iota(I32, (c_rows, c_rows), 1)
    row_i = lax.broadcasted_iota(I32, (c_rows, dk), 0)
    mt = lax.broadcasted_iota(I32, (c_rows, n_sub * c_rows), 0)
    mj = lax.broadcasted_iota(I32, (c_rows, n_sub * c_rows), 1)

    def chunk(r0, z_ref, lbv, st_ref, rev):
        lf, kk = _hgrn_gates(z_ref[pl.ds(r0, c_rows), :].astype(F32), lbv)
        q = _silu(hq_ref[pl.ds(r0, c_rows), :].astype(F32))
        v = hi_ref[pl.ds(r0, c_rows), :]
        hi = lf.astype(BF16)
        r1 = lf - hi.astype(F32)
        mid = r1.astype(BF16)
        lo = (r1 - mid.astype(F32)).astype(BF16)
        tri = jnp.where((s_i >= t_i) if rev else (s_i <= t_i), 1.0, 0.0).astype(BF16)
        b3 = _dot(tri, jnp.concatenate([hi, mid, lo], axis=1))
        b = b3[:, :dk] + b3[:, dk:2 * dk] + b3[:, 2 * dk:]
        zero = jnp.zeros((1, dk), F32)
        refs = []
        for i in range(n_sub):
            if rev:
                refs.append(b[(i + 1) * sub:(i + 1) * sub + 1] if i < n_sub - 1 else zero)
            else:
                refs.append(b[i * sub - 1:i * sub] if i > 0 else zero)
        ref_rows = refs[n_sub - 1]
        for i in range(n_sub - 2, -1, -1):
            ref_rows = jnp.where(row_i < (i + 1) * sub, refs[i], ref_rows)
        qd = (q * jnp.exp(b - ref_rows)).astype(BF16)
        ks = []
        for i in range(n_sub):
            valid = (row_i >= i * sub) if rev else (row_i < (i + 1) * sub)
            ks.append((kk * jnp.exp(jnp.where(valid, refs[i] - b, 0.0))).astype(BF16))
        scores = _dot_nt(qd, jnp.concatenate(ks, axis=0))
        js = mj % c_rows
        keep = ((mj // c_rows) == (mt // sub)) & ((js >= mt) if rev else (js <= mt))
        pm = jnp.where(keep, scores, 0.0).astype(BF16)
        st = st_ref[...]
        o = _dot(pm, jnp.concatenate([v] * n_sub, axis=0)) + _dot_nt((q * jnp.exp(b)).astype(BF16),
                                                                      st.astype(BF16))
        blast = b[0:1] if rev else b[c_rows - 1:c_rows]
        ke = (kk * jnp.exp(blast - b)).astype(BF16)
        st_ref[...] = st * jnp.exp(blast) + _dot_tn(v, ke)
        return o

    def fast_path():
        def run(base, n):
            def body(c, carry):
                rf = pl.multiple_of(base + c * c_rows, c_rows)
                rb = pl.multiple_of(base + (n - 1 - c) * c_rows, c_rows)
                o_f = chunk(rf, hf_ref, lbf, stf, False)
                acc[pl.ds(rf, c_rows), :] = acc[pl.ds(rf, c_rows), :] + o_f
                o_b = chunk(rb, hb_ref, lbb, stb, True)
                acc[pl.ds(rb, c_rows), :] = acc[pl.ds(rb, c_rows), :] + o_b
                return carry
            lax.fori_loop(0, n, body, 0)
        run(n_lat, n_ctx // c_rows)
        run(0, n_lat // c_rows)

    def slow_path():
        pack = BF16_SUBLANES
        sub_i = lax.broadcasted_iota(I32, (pack, dk), 0)
        row8 = lax.broadcasted_iota(I32, (8, dk), 0)

        def load_row(ref, t):
            r = pl.multiple_of((t // pack) * pack, pack)
            blk = ref[pl.ds(r, pack), :].astype(F32)
            return jnp.sum(jnp.where(sub_i == t - r, blk, 0.0), axis=0, keepdims=True)

        def run(base, n, z_ref, lbv, st_ref, rev):
            def body(i, carry):
                t = base + ((n - 1 - i) if rev else i)
                lf, kk = _hgrn_gates(load_row(z_ref, t), lbv)
                q = _silu(load_row(hq_ref, t))
                v = load_row(hi_ref, t)
                v8 = jnp.where(row8 == 0, v, 0.0).astype(BF16)
                k8 = jnp.where(row8 == 0, kk, 0.0).astype(BF16)
                st = st_ref[...] * jnp.exp(lf) + _dot_tn(v8, k8)
                st_ref[...] = st
                q8 = jnp.broadcast_to(q, (8, dk)).astype(BF16)
                o = _dot_nt(q8, st.astype(BF16))
                acc[pl.ds(t, 1), :] = acc[pl.ds(t, 1), :] + o[0:1]
                return carry
            lax.fori_loop(0, n, body, 0)
        run(n_lat, n_ctx, hf_ref, lbf, stf, False)
        run(0, n_lat, hf_ref, lbf, stf, False)
        run(n_lat, n_ctx, hb_ref, lbb, stb, True)
        run(0, n_lat, hb_ref, lbb, stb, True)

    lax.cond(safe, fast_path, slow_path)

    rt = ROW_TILE

    def readout(i, carry):
        r = pl.multiple_of(i * rt, rt)
        y = _rms(acc[pl.ds(r, rt), :], gn_ref[...])
        o_ref[pl.ds(r, rt), :] = (y * _silu(hg_ref[pl.ds(r, rt), :].astype(F32))).astype(o_ref.dtype)
        return carry

    lax.fori_loop(0, rows // rt, readout, 0)


def _hgrn(p, col0, n_heads, lb, g_norm, n_lat):
    rows = p.shape[0]
    n_ctx = rows - n_lat
    head = lambda off: pl.BlockSpec((rows, HEAD_DIM), lambda h: (0, col0 + off * n_heads + h))
    lbh = lb.reshape(2, n_heads, HEAD_DIM).transpose(1, 0, 2)
    kern = functools.partial(_hgrn_kernel, n_lat=n_lat, n_ctx=n_ctx)
    vm = 2 * 6 * rows * HEAD_DIM * 2 + rows * HEAD_DIM * 4 + 64 * HGRN_CHUNK * HEAD_DIM * 4
    return pl.pallas_call(
        kern, out_shape=jax.ShapeDtypeStruct((rows, n_heads * HEAD_DIM), BF16),
        grid=(n_heads,),
        in_specs=[head(0), head(1), head(2), head(3), head(4),
                  pl.BlockSpec((1, 2, HEAD_DIM), lambda h: (h, 0, 0)),
                  pl.BlockSpec((1, HEAD_DIM), lambda h: (0, 0))],
        out_specs=pl.BlockSpec((rows, HEAD_DIM), lambda h: (0, h)),
        scratch_shapes=[pltpu.VMEM((rows, HEAD_DIM), F32), pltpu.VMEM((HEAD_DIM, HEAD_DIM), F32),
                        pltpu.VMEM((HEAD_DIM, HEAD_DIM), F32)],
        compiler_params=_cparams(vm, 1), name="hgrn2_bidir",
    )(p, p, p, p, p, lbh, g_norm.reshape(1, HEAD_DIM))


def _prep_gate_up_kernel(w_ref, perm_ref, o_ref):
    wb = w_ref[...].astype(BF16)
    g = V7X_MXU_DIM
    for j in range(w_ref.shape[1] // g):
        o_ref[:, j * g:(j + 1) * g] = _dot(wb[:, j * g:(j + 1) * g], perm_ref[...]).astype(o_ref.dtype)


def _deinterleave_perm():
    g = V7X_MXU_DIM
    src = np.arange(g)
    dst = np.where(src % 2 == 0, src // 2, g // 2 + src // 2)
    perm = np.zeros((g, g), np.float32)
    perm[src, dst] = 1.0
    return jnp.asarray(perm, BF16)


def _prep_gate_up(w4, layer):
    _, n_e, d, n = w4.shape
    tr = 512
    steps = d // tr
    out = pl.pallas_call(
        _prep_gate_up_kernel, out_shape=jax.ShapeDtypeStruct((n_e * d, n), BF16),
        grid=(n_e * steps,),
        in_specs=[pl.BlockSpec((None, None, tr, n), lambda i: (layer, i // steps, i % steps, 0)),
                  pl.BlockSpec((V7X_MXU_DIM, V7X_MXU_DIM), lambda i: (0, 0))],
        out_specs=pl.BlockSpec((tr, n), lambda i: (i, 0)),
        compiler_params=_cparams(2 * tr * n * 6 + tr * n * 6, 1), name="prep_gate_up",
    )(w4, _deinterleave_perm())
    return out.reshape(n_e, d, n)


def _cast_kernel(w_ref, o_ref):
    o_ref[...] = w_ref[...].astype(o_ref.dtype)


def _prep_down(w4, layer):
    _, n_e, ff, d = w4.shape
    return pl.pallas_call(
        _cast_kernel, out_shape=jax.ShapeDtypeStruct((n_e, ff, d), BF16),
        grid=(n_e,),
        in_specs=[pl.BlockSpec((None, 1, ff, d), lambda e: (layer, e, 0, 0))],
        out_specs=pl.BlockSpec((1, ff, d), lambda e: (e, 0, 0)),
        compiler_params=_cparams(2 * ff * d * 6, 1), name="prep_down",
    )(w4)


def _expert_kernel(be_ref, nv_ref, x_ref, wgu_ref, bgu_ref, wdn_ref, bdn_ref, y_ref):
    live = pl.program_id(0) < nv_ref[0]

    @pl.when(live)
    def _():
        gu = _dot(x_ref[...].astype(BF16), wgu_ref[0]) + bgu_ref[0]
        g, h = V7X_MXU_DIM, V7X_LANES
        groups = range(gu.shape[1] // g)
        gate = jnp.minimum(jnp.concatenate([gu[:, j * g:j * g + h] for j in groups], axis=1), SWIGLU_LIMIT)
        up = jnp.clip(jnp.concatenate([gu[:, j * g + h:(j + 1) * g] for j in groups], axis=1),
                      -SWIGLU_LIMIT, SWIGLU_LIMIT)
        act = gate * jax.nn.sigmoid(SWIGLU_ALPHA * gate) * (up + 1.0)
        y_ref[...] = _dot(act.astype(BF16), wdn_ref[0]) + bdn_ref[0]

    @pl.when(jnp.logical_not(live))
    def _():
        y_ref[...] = jnp.zeros(y_ref.shape, y_ref.dtype)


def _experts(xg, block_e, n_valid, wgu, bgu, wdn, bdn):
    n_rows, d = xg.shape
    n_blocks = n_rows // MOE_TILE
    ff = wdn.shape[1]
    vm = 2 * (d * 2 * ff * 2 + ff * d * 2 + MOE_TILE * d * 8) + MOE_TILE * (2 * ff * 8 + d * 6)
    return pl.pallas_call(
        _expert_kernel, out_shape=jax.ShapeDtypeStruct((n_rows, d), F32),
        grid_spec=pltpu.PrefetchScalarGridSpec(
            num_scalar_prefetch=2, grid=(n_blocks,),
            in_specs=[pl.BlockSpec((MOE_TILE, d), lambda b, be, nv: (jnp.minimum(b, nv[0] - 1), 0)),
                      pl.BlockSpec((1, d, 2 * ff), lambda b, be, nv: (be[b], 0, 0)),
                      pl.BlockSpec((1, 1, 2 * ff), lambda b, be, nv: (be[b], 0, 0)),
                      pl.BlockSpec((1, ff, d), lambda b, be, nv: (be[b], 0, 0)),
                      pl.BlockSpec((1, 1, d), lambda b, be, nv: (be[b], 0, 0))],
            out_specs=pl.BlockSpec((MOE_TILE, d), lambda b, be, nv: (b, 0))),
        compiler_params=_cparams(vm, 1), name="expert_ffn",
    )(block_e, n_valid, xg, wgu, bgu, wdn, bdn)


def _route(top_idx, n_tok):
    flat_e = top_idx[:n_tok, :TOP_K].reshape(-1)
    n_pairs = flat_e.shape[0]
    onehot = (flat_e[:, None] == jnp.arange(N_EXPERTS, dtype=I32)[None, :]).astype(I32)
    csum = jnp.cumsum(onehot, axis=0)
    rank = jnp.sum(onehot * csum, axis=1) - 1
    counts = csum[-1]
    padded = (counts + MOE_TILE - 1) // MOE_TILE * MOE_TILE
    pend = jnp.cumsum(padded)
    dest = jnp.sum(onehot * (pend - padded)[None, :], axis=1) + rank
    n_blocks = -(-(n_pairs + N_EXPERTS * (MOE_TILE - 1)) // MOE_TILE)
    row_tok = jnp.zeros((n_blocks * MOE_TILE,), I32).at[dest].set(jnp.arange(n_pairs, dtype=I32) // TOP_K)
    n_valid = (pend[-1] // MOE_TILE).astype(I32).reshape(1)
    blk = jnp.minimum(jnp.arange(n_blocks, dtype=I32), n_valid[0] - 1) * MOE_TILE
    block_e = jnp.minimum(jnp.sum((pend[None, :] <= blk[:, None]).astype(I32), axis=1), N_EXPERTS - 1)
    return dest.reshape(n_tok, TOP_K), row_tok, block_e, n_valid


def _moe(h2, top_idx, n_tok, wgu, bgu, wdn, bdn):
    dest, row_tok, block_e, n_valid = _route(top_idx, n_tok)
    xg = h2.at[row_tok].get(mode='promise_in_bounds')
    y = _experts(xg, block_e, n_valid, wgu, bgu, wdn, bdn)
    yg = y.at[dest.T.reshape(-1)].get(mode='promise_in_bounds')
    return yg.reshape(TOP_K, n_tok, -1)


def _rope_tables(n_tok):
    t = jnp.arange(n_tok, dtype=I32)
    row = (t // GRID_W).astype(F32)
    col = (t % GRID_W).astype(F32)
    pairs = HEAD_DIM // 4
    inv_freq = ROPE_THETA ** (-jnp.arange(pairs, dtype=F32) / pairs)
    ang = jnp.concatenate([row[:, None] * inv_freq, col[:, None] * inv_freq], axis=-1)
    cos_rep = jnp.repeat(jnp.cos(ang), 2, axis=-1)
    sin_sgn = jnp.stack([-jnp.sin(ang), jnp.sin(ang)], axis=-1).reshape(n_tok, HEAD_DIM)
    return cos_rep, sin_sgn


def kernel(x, c, ctx, c_ctx, w_ada, b_ada, norm_pre_mix, norm_post_mix, norm_pre_ffn, norm_post_ffn,
           w_in, w_fnet, b_fnet, na_rpb, hgrn_lb_logits, hgrn_out_norm, w_out,
           w_router, b_router, w_gate_up, b_gate_up, w_down, b_down):
    batch, n_lat, d = x.shape
    n_ctx = ctx.shape[1]
    assert batch == 1 and c.shape[0] == 1
    depth = w_ada.shape[0]
    fnet_w = w_fnet.shape[1]
    n_heads = na_rpb.shape[1]
    ff = w_down.shape[2]
    fnet_blocks = fnet_w // HEAD_DIM

    cvec = jnp.zeros((8, d), F32).at[0].set(c[0]).at[1].set(c_ctx)
    ada = _ada(cvec, w_ada, b_ada)
    mods = jnp.pad(ada[:, :2].reshape(depth, 2, 6, d), ((0, 0), (0, 0), (0, 2), (0, 0)))
    cos_rep, sin_sgn = _rope_tables(n_lat)
    p_lb = jax.nn.softmax(hgrn_lb_logits.astype(F32), axis=0)
    lower_bounds = jnp.cumsum(p_lb, axis=0) - p_lb[0]
    groups = 2 * ff // V7X_MXU_DIM

    xs = jnp.concatenate([x[0], ctx[0]], axis=0)
    h = _pre(xs, mods[0], norm_pre_mix[0], n_lat)
    for layer in range(depth):
        last = layer == depth - 1
        p = _matmul(h, w_in, layer, BF16)
        yf = _fourier_mix(p, fnet_w, w_fnet[layer], b_fnet[layer], n_lat)
        na = _na(p, fnet_blocks, n_heads, na_rpb[layer], cos_rep, sin_sgn, n_lat)
        hg = _hgrn(p, fnet_blocks + 3 * n_heads, n_heads, lower_bounds[layer], hgrn_out_norm[layer], n_lat)
        mix = jnp.concatenate([yf, na, hg], axis=1)
        y = _matmul(mix, w_out, layer, BF16)
        x1, h2, top_idx, top_w = _mid(xs, y, mods[layer], norm_post_mix[layer], norm_pre_ffn[layer],
                                      w_router[layer], b_router[layer], n_lat)
        bgu = b_gate_up[layer].reshape(N_EXPERTS, groups, V7X_LANES, 2).transpose(0, 1, 3, 2)
        yg = _moe(h2, top_idx, n_lat if last else n_lat + n_ctx,
                  _prep_gate_up(w_gate_up, layer), bgu.reshape(N_EXPERTS, 1, 2 * ff),
                  _prep_down(w_down, layer), b_down[layer].reshape(N_EXPERTS, 1, d))
        if last:
            xs = _final(x1, yg, top_w, mods[layer], norm_post_ffn[layer], n_lat)
        else:
            xs, h = _end(x1, yg, top_w, mods[layer], norm_post_ffn[layer], mods[layer + 1],
                         norm_pre_mix[layer + 1], n_lat)
    return xs[:n_lat].reshape(batch, n_lat, d)
```

```python
import functools

import numpy as np
import jax
import jax.numpy as jnp
from jax import lax
from jax.experimental import pallas as pl
from jax.experimental.pallas import tpu as pltpu

F32 = jnp.float32
BF16 = jnp.bfloat16
I32 = jnp.int32

GRID_W = 64
HEAD_DIM = 128
NA_KR = 8
NA_KC = 16
NA_QROWS = 2
ROPE_THETA = 10000.0
N_EXPERTS = 32
TOP_K = 4
SWIGLU_LIMIT = 7.0
SWIGLU_ALPHA = 1.702
RMS_EPS = 1e-6
MASK_VALUE = -1e30
GATE_FLOOR = 1e-30

V7X_VMEM_BYTES = 64 * 1024 * 1024
V7X_LANES = 128
V7X_MXU_DIM = 256
BF16_SUBLANES = 16

ROW_TILE = 256
COMBINE_TILE = 128
MOE_TILE = 256
HGRN_CHUNK = 128
HGRN_SUB = 16
HGRN_SAFE_DECAY = 80.0
HGRN_UNROLL = 4
NA_UNROLL = 2
FFT_N1 = 64


def _cparams(vmem_bytes, n_grid):
    limit = int(min(max(vmem_bytes * 5 // 4 + (4 << 20), 32 << 20), V7X_VMEM_BYTES - (4 << 20)))
    return pltpu.CompilerParams(dimension_semantics=("arbitrary",) * n_grid, vmem_limit_bytes=limit)


def _dot(a, b):
    return jnp.dot(a, b, preferred_element_type=F32)


def _dot_nt(a, b):
    return lax.dot_general(a, b, (((1,), (1,)), ((), ())), preferred_element_type=F32)


def _dot_tn(a, b):
    return lax.dot_general(a, b, (((0,), (0,)), ((), ())), preferred_element_type=F32)


def _silu(x):
    return x * jax.nn.sigmoid(x)


def _rms(x, g):
    return x * lax.rsqrt(jnp.mean(x * x, axis=-1, keepdims=True) + RMS_EPS) * g


def _ada_kernel(c_ref, w_ref, b_ref, o_ref):
    a = _silu(c_ref[...])
    a_hi = a.astype(BF16)
    a_lo = (a - a_hi.astype(F32)).astype(BF16)
    w = w_ref[0].astype(BF16)
    o_ref[0] = _dot(a_hi, w) + _dot(a_lo, w) + b_ref[0]


def _ada(cvec, w_ada, b_ada):
    depth, d, n = w_ada.shape
    tn = 512
    return pl.pallas_call(
        _ada_kernel,
        out_shape=jax.ShapeDtypeStruct((depth, 8, n), F32),
        grid=(depth, n // tn),
        in_specs=[pl.BlockSpec((8, d), lambda l, j: (0, 0)),
                  pl.BlockSpec((1, d, tn), lambda l, j: (l, 0, j)),
                  pl.BlockSpec((1, 1, tn), lambda l, j: (l, 0, j))],
        out_specs=pl.BlockSpec((1, 8, tn), lambda l, j: (l, 0, j)),
        compiler_params=_cparams(2 * d * tn * 4 + d * tn * 2, 2),
        name="adaln",
    )(cvec, w_ada, b_ada.reshape(depth, 1, n))


def _pre_kernel(x_ref, mod_ref, g_ref, h_ref):
    y = _rms(x_ref[...], g_ref[...])
    h_ref[...] = (y * (1.0 + mod_ref[0, 1:2, :]) + mod_ref[0, 0:1, :]).astype(h_ref.dtype)


def _mid_kernel(x_ref, y_ref, mod_ref, gpost_ref, gpre_ref, wr_ref, br_ref,
                x1_ref, h2_ref, idx_ref, tw_ref):
    x1 = x_ref[...] + mod_ref[0, 2:3, :] * _rms(y_ref[...].astype(F32), gpost_ref[...])
    x1_ref[...] = x1
    h2 = _rms(x1, gpre_ref[...]) * (1.0 + mod_ref[0, 4:5, :]) + mod_ref[0, 3:4, :]
    h2_ref[...] = h2.astype(h2_ref.dtype)
    logits = jnp.dot(h2, wr_ref[...], precision=lax.Precision.HIGHEST,
                     preferred_element_type=F32) + br_ref[...]
    lane = lax.broadcasted_iota(I32, logits.shape, 1)
    idx_acc = jnp.zeros(logits.shape, I32)
    top_acc = jnp.full(logits.shape, -jnp.inf, F32)
    work = logits
    for r in range(TOP_K):
        m = jnp.max(work, axis=-1, keepdims=True)
        sel = jnp.min(jnp.where(work == m, lane, V7X_LANES), axis=-1, keepdims=True)
        idx_acc = jnp.where(lane == r, sel, idx_acc)
        top_acc = jnp.where(lane == r, m, top_acc)
        work = jnp.where(lane == sel, -jnp.inf, work)
    e = jnp.exp(top_acc - jnp.max(top_acc, axis=-1, keepdims=True))
    idx_ref[...] = idx_acc
    tw_ref[...] = e / jnp.sum(e, axis=-1, keepdims=True)


def _combine(yg_ref, tw_ref):
    tw = tw_ref[...]
    fx = tw[:, 0:1] * yg_ref[0].astype(F32)
    for k in range(1, TOP_K):
        fx = fx + tw[:, k:k + 1] * yg_ref[k].astype(F32)
    return fx


def _end_kernel(x1_ref, yg_ref, tw_ref, mod_ref, gpost_ref, modn_ref, gpren_ref, x2_ref, h_ref):
    x2 = x1_ref[...] + mod_ref[0, 5:6, :] * _rms(_combine(yg_ref, tw_ref), gpost_ref[...])
    x2_ref[...] = x2
    y = _rms(x2, gpren_ref[...])
    h_ref[...] = (y * (1.0 + modn_ref[0, 1:2, :]) + modn_ref[0, 0:1, :]).astype(h_ref.dtype)


def _final_kernel(x1_ref, yg_ref, tw_ref, mod_ref, gpost_ref, x2_ref):
    x2_ref[...] = x1_ref[...] + mod_ref[0, 5:6, :] * _rms(_combine(yg_ref, tw_ref), gpost_ref[...])


def _row_specs(tile, d, n_lat):
    row = lambda w: pl.BlockSpec((tile, w), lambda i: (i, 0))
    mod = pl.BlockSpec((1, 8, d), lambda i: (jnp.minimum(i // (n_lat // tile), 1), 0, 0))
    vec = lambda w: pl.BlockSpec((1, w), lambda i: (0, 0))
    return row, mod, vec


def _pre(x, mods, g, n_lat):
    rows, d = x.shape
    row, mod, vec = _row_specs(ROW_TILE, d, n_lat)
    return pl.pallas_call(
        _pre_kernel, out_shape=jax.ShapeDtypeStruct((rows, d), BF16),
        grid=(rows // ROW_TILE,), in_specs=[row(d), mod, vec(d)], out_specs=row(d),
        compiler_params=_cparams(ROW_TILE * d * 24, 1), name="pre_norm",
    )(x, mods, g.reshape(1, d))


def _mid(x, y, mods, gpost, gpre, w_router, b_router, n_lat):
    rows, d = x.shape
    row, mod, vec = _row_specs(ROW_TILE, d, n_lat)
    wr = jnp.zeros((d, V7X_LANES), F32).at[:, :N_EXPERTS].set(w_router)
    br = jnp.full((1, V7X_LANES), -jnp.inf, F32).at[0, :N_EXPERTS].set(b_router)
    return pl.pallas_call(
        _mid_kernel,
        out_shape=(jax.ShapeDtypeStruct((rows, d), F32), jax.ShapeDtypeStruct((rows, d), BF16),
                   jax.ShapeDtypeStruct((rows, V7X_LANES), I32), jax.ShapeDtypeStruct((rows, V7X_LANES), F32)),
        grid=(rows // ROW_TILE,),
        in_specs=[row(d), row(d), mod, vec(d), vec(d),
                  pl.BlockSpec((d, V7X_LANES), lambda i: (0, 0)), vec(V7X_LANES)],
        out_specs=(row(d), row(d), row(V7X_LANES), row(V7X_LANES)),
        compiler_params=_cparams(ROW_TILE * d * 48 + d * V7X_LANES * 8, 1), name="post_mix_router",
    )(x, y, mods, gpost.reshape(1, d), gpre.reshape(1, d), wr, br)


def _end(x1, yg, top_w, mods, gpost, mods_next, gpre_next, n_lat):
    rows, d = x1.shape
    t = COMBINE_TILE
    row, mod, vec = _row_specs(t, d, n_lat)
    return pl.pallas_call(
        _end_kernel,
        out_shape=(jax.ShapeDtypeStruct((rows, d), F32), jax.ShapeDtypeStruct((rows, d), BF16)),
        grid=(rows // t,),
        in_specs=[row(d), pl.BlockSpec((TOP_K, t, d), lambda i: (0, i, 0)), row(V7X_LANES),
                  mod, vec(d), mod, vec(d)],
        out_specs=(row(d), row(d)),
        compiler_params=_cparams(t * d * (2 * 4 * (TOP_K + 2) + 2 * 2 + 16), 1), name="post_ffn_pre_norm",
    )(x1, yg, top_w, mods, gpost.reshape(1, d), mods_next, gpre_next.reshape(1, d))


def _final(x1, yg, top_w, mods, gpost, n_lat):
    d = x1.shape[1]
    t = COMBINE_TILE
    row, mod, vec = _row_specs(t, d, n_lat)
    return pl.pallas_call(
        _final_kernel, out_shape=jax.ShapeDtypeStruct((n_lat, d), F32),
        grid=(n_lat // t,),
        in_specs=[row(d), pl.BlockSpec((TOP_K, t, d), lambda i: (0, i, 0)), row(V7X_LANES), mod, vec(d)],
        out_specs=row(d),
        compiler_params=_cparams(t * d * (2 * 4 * (TOP_K + 2) + 16), 1), name="post_ffn",
    )(x1, yg, top_w, mods, gpost.reshape(1, d))


def _mm_kernel(a_ref, b_ref, o_ref, bscr):
    @pl.when(pl.program_id(1) == 0)
    def _():
        bscr[...] = b_ref[0].astype(BF16)

    o_ref[...] = _dot(a_ref[...], bscr[...]).astype(o_ref.dtype)


def _pick_tile(n, prefs):
    for t in prefs:
        if n % t == 0:
            return t
    return n


def _matmul(a, w3, layer, out_dtype):
    m, k = a.shape
    n = w3.shape[2]
    tm = _pick_tile(m, (384, 256, 128))
    tn = _pick_tile(n, (1024, 512, 256, 128))
    vm = 2 * (tm * k * 2 + k * tn * 4 + tm * tn * 4) + k * tn * 2 + tm * tn * 4
    return pl.pallas_call(
        _mm_kernel, out_shape=jax.ShapeDtypeStruct((m, n), out_dtype),
        grid=(n // tn, m // tm),
        in_specs=[pl.BlockSpec((tm, k), lambda j, i: (i, 0)),
                  pl.BlockSpec((1, k, tn), lambda j, i: (layer, 0, j))],
        out_specs=pl.BlockSpec((tm, tn), lambda j, i: (i, j)),
        scratch_shapes=[pltpu.VMEM((k, tn), BF16)],
        compiler_params=_cparams(vm, 2), name="matmul",
    )(a, w3)


def _dft_cos_sin(n):
    i = jnp.arange(n, dtype=I32)
    ang = ((i[:, None] * i[None, :]) % n).astype(F32) * (2.0 * np.pi / n)
    s = 1.0 / np.sqrt(n)
    return jnp.cos(ang) * s, jnp.sin(ang) * s


def _fft1_kernel(x_ref, f_ref, twr_ref, twi_ref, o_ref):
    n1 = f_ref.shape[1]
    tb, _, w = o_ref.shape[1:]
    xt = pltpu.einshape("abw->baw", x_ref[...])
    for j in range(tb):
        res = _dot(f_ref[...], xt[j])
        ar = res[:n1]
        ai = res[n1:]
        wr = jnp.tile(twr_ref[j], (1, w // V7X_LANES))
        wi = jnp.tile(twi_ref[j], (1, w // V7X_LANES))
        o_ref[0, j] = (ar * wr - ai * wi).astype(o_ref.dtype)
        o_ref[1, j] = (ar * wi + ai * wr).astype(o_ref.dtype)


def _fft2_kernel(m_ref, a_ref, o_ref, sr, si):
    n2, tc = a_ref.shape[1:3]
    ar = pltpu.einshape("bcw->cbw", a_ref[0])
    ai = pltpu.einshape("bcw->cbw", a_ref[1])
    for c in range(tc):
        res = _dot(m_ref[...], jnp.concatenate([ar[c], ai[c]], axis=0))
        sr[c] = res[:n2].astype(sr.dtype)
        si[c] = res[n2:].astype(si.dtype)
    o_ref[0] = pltpu.einshape("cdw->dcw", sr[...])
    o_ref[1] = pltpu.einshape("cdw->dcw", si[...])


def _dft_dense_kernel(m_ref, a_ref, o_ref):
    o_ref[...] = _dot(m_ref[...], a_ref[...]).astype(o_ref.dtype)


def _fft3_kernel(x_ref, cb_ref, sb_ref, wf_ref, bf_ref, o_ref):
    z = _dot(x_ref[0], cb_ref[...]) + _dot(x_ref[1], sb_ref[...])
    o_ref[...] = (_dot(z.astype(BF16), wf_ref[...]) + bf_ref[...]).astype(o_ref.dtype)


def _fourier_mix(p, w, w_fnet, b_fnet, n_lat):
    rows, cols = p.shape
    n_ctx = rows - n_lat
    n1 = FFT_N1
    n2 = n_lat // n1
    tb = BF16_SUBLANES
    c1, s1 = _dft_cos_sin(n1)
    f1 = jnp.concatenate([c1, -s1], axis=0).astype(BF16)
    bi = jnp.arange(n2, dtype=I32)[:, None]
    ci = jnp.arange(n1, dtype=I32)[None, :]
    tang = ((bi * ci) % n_lat).astype(F32) * (2.0 * np.pi / n_lat)
    twr = jnp.broadcast_to(jnp.cos(tang)[:, :, None], (n2, n1, V7X_LANES))
    twi = jnp.broadcast_to(-jnp.sin(tang)[:, :, None], (n2, n1, V7X_LANES))
    a = pl.pallas_call(
        _fft1_kernel, out_shape=jax.ShapeDtypeStruct((2, n2, n1, w), BF16),
        grid=(n2 // tb,),
        in_specs=[pl.BlockSpec((n1, tb, w), lambda j: (0, j, 0)),
                  pl.BlockSpec((2 * n1, n1), lambda j: (0, 0)),
                  pl.BlockSpec((tb, n1, V7X_LANES), lambda j: (j, 0, 0)),
                  pl.BlockSpec((tb, n1, V7X_LANES), lambda j: (j, 0, 0))],
        out_specs=pl.BlockSpec((2, tb, n1, w), lambda j: (0, j, 0, 0)),
        compiler_params=_cparams(2 * 3 * n1 * tb * w * 2 + 8 * n1 * w * 4, 1), name="fft_stage1",
    )(p.reshape(rows // n2, n2, cols), f1, twr, twi)
    c2, s2 = _dft_cos_sin(n2)
    m2 = jnp.concatenate([jnp.concatenate([c2, s2], axis=1),
                          jnp.concatenate([-s2, c2], axis=1)], axis=0).astype(BF16)
    tc = BF16_SUBLANES
    xl = pl.pallas_call(
        _fft2_kernel, out_shape=jax.ShapeDtypeStruct((2, n2, n1, w), BF16),
        grid=(n1 // tc,),
        in_specs=[pl.BlockSpec((2 * n2, 2 * n2), lambda j: (0, 0)),
                  pl.BlockSpec((2, n2, tc, w), lambda j: (0, 0, j, 0))],
        out_specs=pl.BlockSpec((2, n2, tc, w), lambda j: (0, 0, j, 0)),
        scratch_shapes=[pltpu.VMEM((tc, n2, w), BF16), pltpu.VMEM((tc, n2, w), BF16)],
        compiler_params=_cparams(5 * 2 * n2 * tc * w * 2 + 8 * n2 * w * 4, 1), name="fft_stage2",
    )(m2, a).reshape(2, n_lat, w)
    cc, sc = _dft_cos_sin(n_ctx)
    mc = jnp.concatenate([cc, -sc], axis=0).astype(BF16)
    xc = pl.pallas_call(
        _dft_dense_kernel, out_shape=jax.ShapeDtypeStruct((2 * n_ctx, w), BF16),
        grid=(1,),
        in_specs=[pl.BlockSpec((2 * n_ctx, n_ctx), lambda j: (0, 0)),
                  pl.BlockSpec((n_ctx, w), lambda j: (n_lat // n_ctx, 0))],
        out_specs=pl.BlockSpec((2 * n_ctx, w), lambda j: (0, 0)),
        compiler_params=_cparams(16 * n_ctx * w, 1), name="dft_context",
    )(mc, p).reshape(2, n_ctx, w)
    xall = jnp.concatenate([xl, xc], axis=1)
    cg, sg = _dft_cos_sin(HEAD_DIM)
    eye = jnp.eye(w // HEAD_DIM, dtype=F32)
    cb = jnp.kron(eye, cg).astype(BF16)
    sb = jnp.kron(eye, sg).astype(BF16)
    full = lambda r, c: pl.BlockSpec((r, c), lambda i: (0, 0))
    return pl.pallas_call(
        _fft3_kernel, out_shape=jax.ShapeDtypeStruct((rows, w), BF16),
        grid=(rows // ROW_TILE,),
        in_specs=[pl.BlockSpec((2, ROW_TILE, w), lambda i: (0, i, 0)),
                  full(w, w), full(w, w), full(w, w), full(1, w)],
        out_specs=pl.BlockSpec((ROW_TILE, w), lambda i: (i, 0)),
        compiler_params=_cparams(2 * (3 * w * w * 2 + 3 * ROW_TILE * w * 2) + ROW_TILE * w * 12, 1),
        name="fft_channel_linear",
    )(xall, cb, sb, w_fnet.astype(BF16), b_fnet.reshape(1, w))


def _na_tables(rows):
    kr = min(NA_KR, rows)
    nb = min(kr + NA_QROWS - 1, rows)
    nqb = rows // NA_QROWS
    r0 = np.arange(nqb) * NA_QROWS
    band0 = np.minimum(np.clip(r0 - kr // 2, 0, rows - kr), rows - nb)
    band_rows = band0[:, None] + np.arange(nb)[None, :]
    q_row = np.repeat(r0[:, None] + np.arange(NA_QROWS)[None, :], GRID_W, axis=1)
    q_col = np.tile(np.arange(GRID_W), NA_QROWS)
    k_row = np.repeat(band_rows, GRID_W, axis=1)
    k_col = np.tile(np.arange(GRID_W), nb)
    win_r = np.clip(q_row - kr // 2, 0, rows - kr)[:, :, None]
    win_c = np.clip(q_col - NA_KC // 2, 0, GRID_W - NA_KC)[:, None]
    kro = k_row[:, None, :]
    col_ok = (k_col[None, :] >= win_c) & (k_col[None, :] < win_c + NA_KC)
    mask = (kro >= win_r) & (kro < win_r + kr) & col_ok[None]
    d_row = np.clip(kro - q_row[:, :, None] + NA_KR - 1, 0, 2 * NA_KR - 2)
    d_col = np.clip(k_col[None, :] - q_col[:, None] + NA_KC - 1, 0, 2 * NA_KC - 2)
    pats, pid = [], np.zeros(nqb, np.int32)
    for n in range(nqb):
        for p, (m0, d0) in enumerate(pats):
            if np.array_equal(m0, mask[n]) and np.array_equal(d0, d_row[n]):
                pid[n] = p
                break
        else:
            pid[n] = len(pats)
            pats.append((mask[n], d_row[n]))
    pmask = np.stack([p[0] for p in pats])
    pdrow = np.stack([p[1] for p in pats])
    return (band0 * GRID_W).astype(np.int32), pid, pmask, pdrow, d_col, nb


def _na_bias(rpb, pmask, pdrow, d_col, nb):
    n_heads = rpb.shape[0]
    n_pat, qb, nk = pmask.shape
    tiles = rpb.astype(F32)[:, :, d_col[:GRID_W, :GRID_W]]
    drow_small = pdrow[:, ::GRID_W, ::GRID_W]
    b6 = tiles[:, drow_small]
    bias = b6.transpose(0, 1, 2, 4, 3, 5).reshape(n_heads, n_pat, qb, nk)
    return jnp.where(pmask[None], bias, MASK_VALUE)


def _na_kernel(pid_ref, st_ref, q_ref, k_ref, v_ref, cos_ref, sin_ref, bias_ref, o_ref, qs, ks,
               *, n_lat, n_ctx, nqb, qb, nk):
    scale = HEAD_DIM ** -0.5
    rt = 512 if n_lat % 512 == 0 else qb

    def rope_body(i, carry):
        r = pl.multiple_of(i * rt, rt)
        c = cos_ref[pl.ds(r, rt), :]
        s = sin_ref[pl.ds(r, rt), :]
        even = (lax.broadcasted_iota(I32, (rt, HEAD_DIM), 1) % 2) == 0

        def rot(x):
            xs = jnp.where(even, pltpu.roll(x, HEAD_DIM - 1, 1), pltpu.roll(x, 1, 1))
            return x * c + xs * s

        qs[pl.ds(r, rt), :] = (rot(q_ref[pl.ds(r, rt), :].astype(F32)) * scale).astype(BF16)
        ks[pl.ds(r, rt), :] = rot(k_ref[pl.ds(r, rt), :].astype(F32)).astype(BF16)
        return carry

    lax.fori_loop(0, n_lat // rt, rope_body, 0)
    kc = k_ref[n_lat:n_lat + n_ctx, :]
    vc = v_ref[n_lat:n_lat + n_ctx, :]

    def block_body(n, carry):
        r = pl.multiple_of(n * qb, qb)
        st = pl.multiple_of(st_ref[n], GRID_W)
        qblk = qs[pl.ds(r, qb), :]
        s_loc = _dot_nt(qblk, ks[pl.ds(st, nk), :]) + bias_ref[0, pid_ref[n]]
        s_ctx = _dot_nt(qblk, kc)
        m = jnp.maximum(jnp.max(s_loc, axis=-1, keepdims=True), jnp.max(s_ctx, axis=-1, keepdims=True))
        p_loc = jnp.exp(s_loc - m)
        p_ctx = jnp.exp(s_ctx - m)
        denom = jnp.sum(p_loc, axis=-1, keepdims=True) + jnp.sum(p_ctx, axis=-1, keepdims=True)
        o = _dot(p_loc.astype(BF16), v_ref[pl.ds(st, nk), :]) + _dot(p_ctx.astype(BF16), vc)
        o_ref[pl.ds(r, qb), :] = (o / denom).astype(o_ref.dtype)
        return carry

    lax.fori_loop(0, nqb, block_body, 0, unroll=NA_UNROLL)
    s = _dot_nt(q_ref[n_lat:n_lat + n_ctx, :], kc) * scale
    p = jnp.exp(s - jnp.max(s, axis=-1, keepdims=True))
    o = _dot(p.astype(BF16), vc) / jnp.sum(p, axis=-1, keepdims=True)
    o_ref[n_lat:n_lat + n_ctx, :] = o.astype(o_ref.dtype)


def _na(p, col0, n_heads, rpb, cos_rep, sin_sgn, n_lat):
    rows = p.shape[0]
    n_ctx = rows - n_lat
    grid_rows = n_lat // GRID_W
    st, pid, pmask, pdrow, d_col, nb = _na_tables(grid_rows)
    n_pat, qb, nk = pmask.shape
    nqb = grid_rows // NA_QROWS
    bias = _na_bias(rpb, pmask, pdrow, d_col, nb)
    head = lambda off: pl.BlockSpec((rows, HEAD_DIM), lambda h, *_: (0, col0 + off * n_heads + h))
    kern = functools.partial(_na_kernel, n_lat=n_lat, n_ctx=n_ctx, nqb=nqb, qb=qb, nk=nk)
    vm = 2 * (4 * rows * HEAD_DIM * 2 + n_pat * qb * nk * 4) + 2 * n_lat * HEAD_DIM * 4 \
        + 2 * n_lat * HEAD_DIM * 2 + 8 * qb * (nk + n_ctx) * 4
    return pl.pallas_call(
        kern, out_shape=jax.ShapeDtypeStruct((rows, n_heads * HEAD_DIM), BF16),
        grid_spec=pltpu.PrefetchScalarGridSpec(
            num_scalar_prefetch=2, grid=(n_heads,),
            in_specs=[head(0), head(1), head(2),
                      pl.BlockSpec(memory_space=pltpu.VMEM), pl.BlockSpec(memory_space=pltpu.VMEM),
                      pl.BlockSpec((1, n_pat, qb, nk), lambda h, *_: (h, 0, 0, 0))],
            out_specs=pl.BlockSpec((rows, HEAD_DIM), lambda h, *_: (0, h)),
            scratch_shapes=[pltpu.VMEM((n_lat, HEAD_DIM), BF16), pltpu.VMEM((n_lat, HEAD_DIM), BF16)]),
        compiler_params=_cparams(vm, 1), name="neighborhood_attention",
    )(jnp.asarray(pid), jnp.asarray(st), p, p, p, cos_rep, sin_sgn, bias)


def _hgrn_gates(z, lbv):
    f = lbv + (1.0 - lbv) * jax.nn.sigmoid(z)
    lf = jnp.log(jnp.maximum(f, GATE_FLOOR))
    kk = (1.0 - lbv) * jax.nn.sigmoid(-z)
    return lf, kk


def _hgrn_kernel(hq_ref, hi_ref, hf_ref, hb_ref, hg_ref, lb_ref, gn_ref, o_ref, acc, stf, stb,
                 *, n_lat, n_ctx):
    c_rows, sub = HGRN_CHUNK, HGRN_SUB
    n_sub = c_rows // sub
    dk = HEAD_DIM
    rows = n_lat + n_ctx
    lbf = lb_ref[0, 0:1, :]
    lbb = lb_ref[0, 1:2, :]

    def decay_body(i, mn):
        r = pl.multiple_of(i * c_rows, c_rows)
        lf_f, _ = _hgrn_gates(hf_ref[pl.ds(r, c_rows), :].astype(F32), lbf)
        lf_b, _ = _hgrn_gates(hb_ref[pl.ds(r, c_rows), :].astype(F32), lbb)
        for j in range(n_sub):
            mn = jnp.minimum(mn, jnp.sum(lf_f[j * sub:(j + 1) * sub], axis=0, keepdims=True))
            mn = jnp.minimum(mn, jnp.sum(lf_b[j * sub:(j + 1) * sub], axis=0, keepdims=True))
        return mn

    mn = lax.fori_loop(0, rows // c_rows, decay_body, jnp.zeros((1, dk), F32))
    safe = jnp.min(mn) >= -HGRN_SAFE_DECAY

    acc[...] = jnp.zeros(acc.shape, F32)
    stf[...] = jnp.zeros(stf.shape, F32)
    stb[...] = jnp.zeros(stb.shape, F32)

    row_i = lax.broadcasted_iota(I32, (c_rows, dk), 0)
    t_i = lax.broadcasted_iota(I32, (c_rows, c_rows), 0)
    s_i = lax.broadcasted_iota(I32, (c_rows, c_rows), 1)

    def chunk(r0, z_ref, lbv, st_ref, rev):
        lf, kk = _hgrn_gates(z_ref[pl.ds(r0, c_rows), :].astype(F32), lbv)
        q = _silu(hq_ref[pl.ds(r0, c_rows), :].astype(F32))
        v = hi_ref[pl.ds(r0, c_rows), :]
        b = lf
        step = 1
        while step < c_rows:
            if rev:
                b = b + jnp.where(row_i < c_rows - step, pltpu.roll(b, c_rows - step, 0), 0.0)
            else:
                b = b + jnp.where(row_i >= step, pltpu.roll(b, step, 0), 0.0)
            step *= 2
        zero = jnp.zeros((1, dk), F32)
        refs = []
        for i in range(n_sub):
            if rev:
                refs.append(b[(i + 1) * sub:(i + 1) * sub + 1] if i < n_sub - 1 else zero)
            else:
                refs.append(b[i * sub - 1:i * sub] if i > 0 else zero)
        ref_rows = refs[n_sub - 1]
        for i in range(n_sub - 2, -1, -1):
            ref_rows = jnp.where(row_i < (i + 1) * sub, refs[i], ref_rows)
        qd = (q * jnp.exp(b - ref_rows)).astype(BF16)
        ks = []
        for i in range(n_sub):
            lo, hi = (i * sub, c_rows) if rev else (0, (i + 1) * sub)
            part = (kk[lo:hi] * jnp.exp(refs[i] - b[lo:hi])).astype(BF16)
            if hi - lo < c_rows:
                pad = jnp.zeros((c_rows - (hi - lo), dk), BF16)
                part = jnp.concatenate([pad, part] if rev else [part, pad], axis=0)
            ks.append(part)
        scores = _dot_nt(qd, jnp.concatenate(ks, axis=0))
        own = jnp.concatenate([scores[i * sub:(i + 1) * sub, i * c_rows:(i + 1) * c_rows]
                               for i in range(n_sub)], axis=0)
        pm = jnp.where((s_i >= t_i) if rev else (s_i <= t_i), own, 0.0).astype(BF16)
        st = st_ref[...]
        o = _dot(pm, v) + _dot_nt((q * jnp.exp(b)).astype(BF16), st.astype(BF16))
        blast = b[0:1] if rev else b[c_rows - 1:c_rows]
        ke = (kk * jnp.exp(blast - b)).astype(BF16)
        st_ref[...] = st * jnp.exp(blast) + _dot_tn(v, ke)
        return o

    def fast_path():
        def run(base, n):
            def body(c, carry):
                rf = pl.multiple_of(base + c * c_rows, c_rows)
                rb = pl.multiple_of(base + (n - 1 - c) * c_rows, c_rows)
                o_f = chunk(rf, hf_ref, lbf, stf, False)
                acc[pl.ds(rf, c_rows), :] = acc[pl.ds(rf, c_rows), :] + o_f
                o_b = chunk(rb, hb_ref, lbb, stb, True)
                acc[pl.ds(rb, c_rows), :] = acc[pl.ds(rb, c_rows), :] + o_b
                return carry
            lax.fori_loop(0, n, body, 0, unroll=HGRN_UNROLL)
        run(n_lat, n_ctx // c_rows)
        run(0, n_lat // c_rows)

    def slow_path():
        pack = BF16_SUBLANES
        sub_i = lax.broadcasted_iota(I32, (pack, dk), 0)
        row8 = lax.broadcasted_iota(I32, (8, dk), 0)

        def load_row(ref, t):
            r = pl.multiple_of((t // pack) * pack, pack)
            blk = ref[pl.ds(r, pack), :].astype(F32)
            return jnp.sum(jnp.where(sub_i == t - r, blk, 0.0), axis=0, keepdims=True)

        def run(base, n, z_ref, lbv, st_ref, rev):
            def body(i, carry):
                t = base + ((n - 1 - i) if rev else i)
                lf, kk = _hgrn_gates(load_row(z_ref, t), lbv)
                q = _silu(load_row(hq_ref, t))
                v = load_row(hi_ref, t)
                v8 = jnp.where(row8 == 0, v, 0.0).astype(BF16)
                k8 = jnp.where(row8 == 0, kk, 0.0).astype(BF16)
                st = st_ref[...] * jnp.exp(lf) + _dot_tn(v8, k8)
                st_ref[...] = st
                q8 = jnp.broadcast_to(q, (8, dk)).astype(BF16)
                o = _dot_nt(q8, st.astype(BF16))
                acc[pl.ds(t, 1), :] = acc[pl.ds(t, 1), :] + o[0:1]
                return carry
            lax.fori_loop(0, n, body, 0)
        run(n_lat, n_ctx, hf_ref, lbf, stf, False)
        run(0, n_lat, hf_ref, lbf, stf, False)
        run(n_lat, n_ctx, hb_ref, lbb, stb, True)
        run(0, n_lat, hb_ref, lbb, stb, True)

    lax.cond(safe, fast_path, slow_path)

    rt = ROW_TILE

    def readout(i, carry):
        r = pl.multiple_of(i * rt, rt)
        y = _rms(acc[pl.ds(r, rt), :], gn_ref[...])
        o_ref[pl.ds(r, rt), :] = (y * _silu(hg_ref[pl.ds(r, rt), :].astype(F32))).astype(o_ref.dtype)
        return carry

    lax.fori_loop(0, rows // rt, readout, 0)


def _hgrn(p, col0, n_heads, lb, g_norm, n_lat):
    rows = p.shape[0]
    n_ctx = rows - n_lat
    head = lambda off: pl.BlockSpec((rows, HEAD_DIM), lambda h: (0, col0 + off * n_heads + h))
    lbh = lb.reshape(2, n_heads, HEAD_DIM).transpose(1, 0, 2)
    kern = functools.partial(_hgrn_kernel, n_lat=n_lat, n_ctx=n_ctx)
    n_keys = HGRN_CHUNK * HGRN_CHUNK // HGRN_SUB
    vm = 2 * 6 * rows * HEAD_DIM * 2 + rows * HEAD_DIM * 4 + 6 * HGRN_UNROLL * HGRN_CHUNK * n_keys * 4
    return pl.pallas_call(
        kern, out_shape=jax.ShapeDtypeStruct((rows, n_heads * HEAD_DIM), BF16),
        grid=(n_heads,),
        in_specs=[head(0), head(1), head(2), head(3), head(4),
                  pl.BlockSpec((1, 2, HEAD_DIM), lambda h: (h, 0, 0)),
                  pl.BlockSpec((1, HEAD_DIM), lambda h: (0, 0))],
        out_specs=pl.BlockSpec((rows, HEAD_DIM), lambda h: (0, h)),
        scratch_shapes=[pltpu.VMEM((rows, HEAD_DIM), F32), pltpu.VMEM((HEAD_DIM, HEAD_DIM), F32),
                        pltpu.VMEM((HEAD_DIM, HEAD_DIM), F32)],
        compiler_params=_cparams(vm, 1), name="hgrn2_bidir",
    )(p, p, p, p, p, lbh, g_norm.reshape(1, HEAD_DIM))


def _prep_gate_up_kernel(w_ref, perm_ref, o_ref):
    wb = w_ref[...].astype(BF16)
    g = V7X_MXU_DIM
    for j in range(w_ref.shape[1] // g):
        o_ref[:, j * g:(j + 1) * g] = _dot(wb[:, j * g:(j + 1) * g], perm_ref[...]).astype(o_ref.dtype)


def _deinterleave_perm():
    g = V7X_MXU_DIM
    src = np.arange(g)
    dst = np.where(src % 2 == 0, src // 2, g // 2 + src // 2)
    perm = np.zeros((g, g), np.float32)
    perm[src, dst] = 1.0
    return jnp.asarray(perm, BF16)


def _prep_gate_up(w4, layer):
    _, n_e, d, n = w4.shape
    tr = 512
    steps = d // tr
    out = pl.pallas_call(
        _prep_gate_up_kernel, out_shape=jax.ShapeDtypeStruct((n_e * d, n), BF16),
        grid=(n_e * steps,),
        in_specs=[pl.BlockSpec((None, None, tr, n), lambda i: (layer, i // steps, i % steps, 0)),
                  pl.BlockSpec((V7X_MXU_DIM, V7X_MXU_DIM), lambda i: (0, 0))],
        out_specs=pl.BlockSpec((tr, n), lambda i: (i, 0)),
        compiler_params=_cparams(2 * tr * n * 6 + tr * n * 6, 1), name="prep_gate_up",
    )(w4, _deinterleave_perm())
    return out.reshape(n_e, d, n)


def _cast_kernel(w_ref, o_ref):
    o_ref[...] = w_ref[...].astype(o_ref.dtype)


def _prep_down(w4, layer):
    _, n_e, ff, d = w4.shape
    return pl.pallas_call(
        _cast_kernel, out_shape=jax.ShapeDtypeStruct((n_e, ff, d), BF16),
        grid=(n_e,),
        in_specs=[pl.BlockSpec((None, 1, ff, d), lambda e: (layer, e, 0, 0))],
        out_specs=pl.BlockSpec((1, ff, d), lambda e: (e, 0, 0)),
        compiler_params=_cparams(2 * ff * d * 6, 1), name="prep_down",
    )(w4)


def _expert_kernel(be_ref, nv_ref, x_ref, wgu_ref, bgu_ref, wdn_ref, bdn_ref, y_ref):
    live = pl.program_id(0) < nv_ref[0]

    @pl.when(live)
    def _():
        g, h = V7X_MXU_DIM, V7X_LANES
        half = x_ref.shape[0] // 2
        for r in (0, half):
            gu = _dot(x_ref[r:r + half, :], wgu_ref[0]) + bgu_ref[0]
            groups = range(gu.shape[1] // g)
            gate = jnp.minimum(jnp.concatenate([gu[:, j * g:j * g + h] for j in groups], axis=1), SWIGLU_LIMIT)
            up = jnp.clip(jnp.concatenate([gu[:, j * g + h:(j + 1) * g] for j in groups], axis=1),
                          -SWIGLU_LIMIT, SWIGLU_LIMIT)
            act = gate * jax.nn.sigmoid(SWIGLU_ALPHA * gate) * (up + 1.0)
            y_ref[r:r + half, :] = (_dot(act.astype(BF16), wdn_ref[0]) + bdn_ref[0]).astype(y_ref.dtype)

    @pl.when(jnp.logical_not(live))
    def _():
        y_ref[...] = jnp.zeros(y_ref.shape, y_ref.dtype)


def _experts(xg, block_e, n_valid, wgu, bgu, wdn, bdn):
    n_rows, d = xg.shape
    n_blocks = n_rows // MOE_TILE
    ff = wdn.shape[1]
    vm = 2 * (d * 2 * ff * 2 + ff * d * 2 + MOE_TILE * d * 8) + MOE_TILE * (2 * ff * 8 + d * 6)
    return pl.pallas_call(
        _expert_kernel, out_shape=jax.ShapeDtypeStruct((n_rows, d), BF16),
        grid_spec=pltpu.PrefetchScalarGridSpec(
            num_scalar_prefetch=2, grid=(n_blocks,),
            in_specs=[pl.BlockSpec((MOE_TILE, d), lambda b, be, nv: (jnp.minimum(b, nv[0] - 1), 0)),
                      pl.BlockSpec((1, d, 2 * ff), lambda b, be, nv: (be[b], 0, 0)),
                      pl.BlockSpec((1, 1, 2 * ff), lambda b, be, nv: (be[b], 0, 0)),
                      pl.BlockSpec((1, ff, d), lambda b, be, nv: (be[b], 0, 0)),
                      pl.BlockSpec((1, 1, d), lambda b, be, nv: (be[b], 0, 0))],
            out_specs=pl.BlockSpec((MOE_TILE, d), lambda b, be, nv: (b, 0))),
        compiler_params=_cparams(vm, 1), name="expert_ffn",
    )(block_e, n_valid, xg, wgu, bgu, wdn, bdn)


def _route(top_idx, n_tok):
    flat_e = top_idx[:n_tok, :TOP_K].reshape(-1)
    n_pairs = flat_e.shape[0]
    onehot = (flat_e[:, None] == jnp.arange(N_EXPERTS, dtype=I32)[None, :]).astype(I32)
    csum = jnp.cumsum(onehot, axis=0)
    rank = jnp.sum(onehot * csum, axis=1) - 1
    counts = csum[-1]
    padded = (counts + MOE_TILE - 1) // MOE_TILE * MOE_TILE
    pend = jnp.cumsum(padded)
    dest = jnp.sum(onehot * (pend - padded)[None, :], axis=1) + rank
    n_blocks = -(-(n_pairs + N_EXPERTS * (MOE_TILE - 1)) // MOE_TILE)
    row_tok = jnp.zeros((n_blocks * MOE_TILE,), I32).at[dest].set(jnp.arange(n_pairs, dtype=I32) // TOP_K)
    n_valid = (pend[-1] // MOE_TILE).astype(I32).reshape(1)
    blk = jnp.minimum(jnp.arange(n_blocks, dtype=I32), n_valid[0] - 1) * MOE_TILE
    block_e = jnp.minimum(jnp.sum((pend[None, :] <= blk[:, None]).astype(I32), axis=1), N_EXPERTS - 1)
    return dest.reshape(n_tok, TOP_K), row_tok, block_e, n_valid


def _moe(h2, top_idx, n_tok, wgu, bgu, wdn, bdn):
    dest, row_tok, block_e, n_valid = _route(top_idx, n_tok)
    xg = h2.at[row_tok].get(mode='promise_in_bounds')
    y = _experts(xg, block_e, n_valid, wgu, bgu, wdn, bdn)
    yg = y.at[dest.T.reshape(-1)].get(mode='promise_in_bounds')
    return yg.reshape(TOP_K, n_tok, -1)


def _rope_tables(n_tok):
    t = jnp.arange(n_tok, dtype=I32)
    row = (t // GRID_W).astype(F32)
    col = (t % GRID_W).astype(F32)
    pairs = HEAD_DIM // 4
    inv_freq = ROPE_THETA ** (-jnp.arange(pairs, dtype=F32) / pairs)
    ang = jnp.concatenate([row[:, None] * inv_freq, col[:, None] * inv_freq], axis=-1)
    cos_rep = jnp.repeat(jnp.cos(ang), 2, axis=-1)
    sin_sgn = jnp.stack([-jnp.sin(ang), jnp.sin(ang)], axis=-1).reshape(n_tok, HEAD_DIM)
    return cos_rep, sin_sgn


def kernel(x, c, ctx, c_ctx, w_ada, b_ada, norm_pre_mix, norm_post_mix, norm_pre_ffn, norm_post_ffn,
           w_in, w_fnet, b_fnet, na_rpb, hgrn_lb_logits, hgrn_out_norm, w_out,
           w_router, b_router, w_gate_up, b_gate_up, w_down, b_down):
    batch, n_lat, d = x.shape
    n_ctx = ctx.shape[1]
    assert batch == 1 and c.shape[0] == 1
    depth = w_ada.shape[0]
    fnet_w = w_fnet.shape[1]
    n_heads = na_rpb.shape[1]
    ff = w_down.shape[2]
    fnet_blocks = fnet_w // HEAD_DIM

    cvec = jnp.zeros((8, d), F32).at[0].set(c[0]).at[1].set(c_ctx)
    ada = _ada(cvec, w_ada, b_ada)
    mods = jnp.pad(ada[:, :2].reshape(depth, 2, 6, d), ((0, 0), (0, 0), (0, 2), (0, 0)))
    cos_rep, sin_sgn = _rope_tables(n_lat)
    p_lb = jax.nn.softmax(hgrn_lb_logits.astype(F32), axis=0)
    lower_bounds = jnp.cumsum(p_lb, axis=0) - p_lb[0]
    groups = 2 * ff // V7X_MXU_DIM

    xs = jnp.concatenate([x[0], ctx[0]], axis=0)
    h = _pre(xs, mods[0], norm_pre_mix[0], n_lat)
    for layer in range(depth):
        last = layer == depth - 1
        p = _matmul(h, w_in, layer, BF16)
        yf = _fourier_mix(p, fnet_w, w_fnet[layer], b_fnet[layer], n_lat)
        na = _na(p, fnet_blocks, n_heads, na_rpb[layer], cos_rep, sin_sgn, n_lat)
        hg = _hgrn(p, fnet_blocks + 3 * n_heads, n_heads, lower_bounds[layer], hgrn_out_norm[layer], n_lat)
        mix = jnp.concatenate([yf, na, hg], axis=1)
        y = _matmul(mix, w_out, layer, BF16)
        x1, h2, top_idx, top_w = _mid(xs, y, mods[layer], norm_post_mix[layer], norm_pre_ffn[layer],
                                      w_router[layer], b_router[layer], n_lat)
        bgu = b_gate_up[layer].reshape(N_EXPERTS, groups, V7X_LANES, 2).transpose(0, 1, 3, 2)
        yg = _moe(h2, top_idx, n_lat if last else n_lat + n_ctx,
                  _prep_gate_up(w_gate_up, layer), bgu.reshape(N_EXPERTS, 1, 2 * ff),
                  _prep_down(w_down, layer), b_down[layer].reshape(N_EXPERTS, 1, d))
        if last:
            xs = _final(x1, yg, top_w, mods[layer], norm_post_ffn[layer], n_lat)
        else:
            xs, h = _end(x1, yg, top_w, mods[layer], norm_post_ffn[layer], mods[layer + 1],
                         norm_pre_mix[layer + 1], n_lat)
    return xs[:n_lat].reshape(batch, n_lat, d)
```

```python
import functools

import numpy as np
import jax
import jax.numpy as jnp
from jax import lax
from jax.experimental import pallas as pl
from jax.experimental.pallas import tpu as pltpu

F32 = jnp.float32
BF16 = jnp.bfloat16
I32 = jnp.int32

GRID_W = 64
HEAD_DIM = 128
NA_KR = 8
NA_KC = 16
NA_QROWS = 2
ROPE_THETA = 10000.0
N_EXPERTS = 32
TOP_K = 4
SWIGLU_LIMIT = 7.0
SWIGLU_ALPHA = 1.702
RMS_EPS = 1e-6
MASK_VALUE = -1e30
GATE_FLOOR = 1e-30

V7X_VMEM_BYTES = 64 * 1024 * 1024
V7X_LANES = 128
V7X_MXU_DIM = 256
BF16_SUBLANES = 16

ROW_TILE = 256
COMBINE_TILE = 128
MOE_TILE = 256
HGRN_CHUNK = 128
HGRN_SUB = 16
HGRN_SAFE_DECAY = 80.0
HGRN_UNROLL = 4
NA_UNROLL = 2
FFT_N1 = 64


def _cparams(vmem_bytes, n_grid):
    limit = int(min(max(vmem_bytes * 5 // 4 + (4 << 20), 32 << 20), V7X_VMEM_BYTES - (4 << 20)))
    return pltpu.CompilerParams(dimension_semantics=("arbitrary",) * n_grid, vmem_limit_bytes=limit)


def _dot(a, b):
    return jnp.dot(a, b, preferred_element_type=F32)


def _dot_nt(a, b):
    return lax.dot_general(a, b, (((1,), (1,)), ((), ())), preferred_element_type=F32)


def _dot_tn(a, b):
    return lax.dot_general(a, b, (((0,), (0,)), ((), ())), preferred_element_type=F32)


def _silu(x):
    return x * jax.nn.sigmoid(x)


def _rms(x, g):
    return x * lax.rsqrt(jnp.mean(x * x, axis=-1, keepdims=True) + RMS_EPS) * g


def _ada_kernel(c_ref, w_ref, b_ref, o_ref):
    a = _silu(c_ref[...])
    a_hi = a.astype(BF16)
    a_lo = (a - a_hi.astype(F32)).astype(BF16)
    w = w_ref[0].astype(BF16)
    o_ref[0] = _dot(a_hi, w) + _dot(a_lo, w) + b_ref[0]


def _ada(cvec, w_ada, b_ada):
    depth, d, n = w_ada.shape
    tn = 512
    return pl.pallas_call(
        _ada_kernel,
        out_shape=jax.ShapeDtypeStruct((depth, 8, n), F32),
        grid=(depth, n // tn),
        in_specs=[pl.BlockSpec((8, d), lambda l, j: (0, 0)),
                  pl.BlockSpec((1, d, tn), lambda l, j: (l, 0, j)),
                  pl.BlockSpec((1, 1, tn), lambda l, j: (l, 0, j))],
        out_specs=pl.BlockSpec((1, 8, tn), lambda l, j: (l, 0, j)),
        compiler_params=_cparams(2 * d * tn * 4 + d * tn * 2, 2),
        name="adaln",
    )(cvec, w_ada, b_ada.reshape(depth, 1, n))


def _pre_kernel(x_ref, mod_ref, g_ref, h_ref):
    y = _rms(x_ref[...], g_ref[...])
    h_ref[...] = (y * (1.0 + mod_ref[0, 1:2, :]) + mod_ref[0, 0:1, :]).astype(h_ref.dtype)


def _mid_kernel(x_ref, y_ref, mod_ref, gpost_ref, gpre_ref, wr_ref, br_ref,
                x1_ref, h2_ref, idx_ref, tw_ref):
    x1 = x_ref[...] + mod_ref[0, 2:3, :] * _rms(y_ref[...].astype(F32), gpost_ref[...])
    x1_ref[...] = x1
    h2 = _rms(x1, gpre_ref[...]) * (1.0 + mod_ref[0, 4:5, :]) + mod_ref[0, 3:4, :]
    h2_ref[...] = h2.astype(h2_ref.dtype)
    logits = jnp.dot(h2, wr_ref[...], precision=lax.Precision.HIGHEST,
                     preferred_element_type=F32) + br_ref[...]
    lane = lax.broadcasted_iota(I32, logits.shape, 1)
    idx_acc = jnp.zeros(logits.shape, I32)
    top_acc = jnp.full(logits.shape, -jnp.inf, F32)
    work = logits
    for r in range(TOP_K):
        m = jnp.max(work, axis=-1, keepdims=True)
        sel = jnp.min(jnp.where(work == m, lane, V7X_LANES), axis=-1, keepdims=True)
        idx_acc = jnp.where(lane == r, sel, idx_acc)
        top_acc = jnp.where(lane == r, m, top_acc)
        work = jnp.where(lane == sel, -jnp.inf, work)
    e = jnp.exp(top_acc - jnp.max(top_acc, axis=-1, keepdims=True))
    idx_ref[...] = idx_acc
    tw_ref[...] = e / jnp.sum(e, axis=-1, keepdims=True)


def _combine(yg_ref, tw_ref):
    tw = tw_ref[...]
    fx = tw[:, 0:1] * yg_ref[0].astype(F32)
    for k in range(1, TOP_K):
        fx = fx + tw[:, k:k + 1] * yg_ref[k].astype(F32)
    return fx


def _end_kernel(x1_ref, yg_ref, tw_ref, mod_ref, gpost_ref, modn_ref, gpren_ref, x2_ref, h_ref):
    x2 = x1_ref[...] + mod_ref[0, 5:6, :] * _rms(_combine(yg_ref, tw_ref), gpost_ref[...])
    x2_ref[...] = x2
    y = _rms(x2, gpren_ref[...])
    h_ref[...] = (y * (1.0 + modn_ref[0, 1:2, :]) + modn_ref[0, 0:1, :]).astype(h_ref.dtype)


def _final_kernel(x1_ref, yg_ref, tw_ref, mod_ref, gpost_ref, x2_ref):
    x2_ref[...] = x1_ref[...] + mod_ref[0, 5:6, :] * _rms(_combine(yg_ref, tw_ref), gpost_ref[...])


def _row_specs(tile, d, n_lat):
    row = lambda w: pl.BlockSpec((tile, w), lambda i: (i, 0))
    mod = pl.BlockSpec((1, 8, d), lambda i: (jnp.minimum(i // (n_lat // tile), 1), 0, 0))
    vec = lambda w: pl.BlockSpec((1, w), lambda i: (0, 0))
    return row, mod, vec


def _pre(x, mods, g, n_lat):
    rows, d = x.shape
    row, mod, vec = _row_specs(ROW_TILE, d, n_lat)
    return pl.pallas_call(
        _pre_kernel, out_shape=jax.ShapeDtypeStruct((rows, d), BF16),
        grid=(rows // ROW_TILE,), in_specs=[row(d), mod, vec(d)], out_specs=row(d),
        compiler_params=_cparams(ROW_TILE * d * 24, 1), name="pre_norm",
    )(x, mods, g.reshape(1, d))


def _mid(x, y, mods, gpost, gpre, w_router, b_router, n_lat):
    rows, d = x.shape
    row, mod, vec = _row_specs(ROW_TILE, d, n_lat)
    wr = jnp.zeros((d, V7X_LANES), F32).at[:, :N_EXPERTS].set(w_router)
    br = jnp.full((1, V7X_LANES), -jnp.inf, F32).at[0, :N_EXPERTS].set(b_router)
    return pl.pallas_call(
        _mid_kernel,
        out_shape=(jax.ShapeDtypeStruct((rows, d), F32), jax.ShapeDtypeStruct((rows, d), F32),
                   jax.ShapeDtypeStruct((rows, V7X_LANES), I32), jax.ShapeDtypeStruct((rows, V7X_LANES), F32)),
        grid=(rows // ROW_TILE,),
        in_specs=[row(d), row(d), mod, vec(d), vec(d),
                  pl.BlockSpec((d, V7X_LANES), lambda i: (0, 0)), vec(V7X_LANES)],
        out_specs=(row(d), row(d), row(V7X_LANES), row(V7X_LANES)),
        compiler_params=_cparams(ROW_TILE * d * 48 + d * V7X_LANES * 8, 1), name="post_mix_router",
    )(x, y, mods, gpost.reshape(1, d), gpre.reshape(1, d), wr, br)


def _end(x1, yg, top_w, mods, gpost, mods_next, gpre_next, n_lat):
    rows, d = x1.shape
    t = COMBINE_TILE
    row, mod, vec = _row_specs(t, d, n_lat)
    return pl.pallas_call(
        _end_kernel,
        out_shape=(jax.ShapeDtypeStruct((rows, d), F32), jax.ShapeDtypeStruct((rows, d), BF16)),
        grid=(rows // t,),
        in_specs=[row(d), pl.BlockSpec((TOP_K, t, d), lambda i: (0, i, 0)), row(V7X_LANES),
                  mod, vec(d), mod, vec(d)],
        out_specs=(row(d), row(d)),
        compiler_params=_cparams(t * d * (2 * 4 * (TOP_K + 2) + 2 * 2 + 16), 1), name="post_ffn_pre_norm",
    )(x1, yg, top_w, mods, gpost.reshape(1, d), mods_next, gpre_next.reshape(1, d))


def _final(x1, yg, top_w, mods, gpost, n_lat):
    d = x1.shape[1]
    t = COMBINE_TILE
    row, mod, vec = _row_specs(t, d, n_lat)
    return pl.pallas_call(
        _final_kernel, out_shape=jax.ShapeDtypeStruct((n_lat, d), F32),
        grid=(n_lat // t,),
        in_specs=[row(d), pl.BlockSpec((TOP_K, t, d), lambda i: (0, i, 0)), row(V7X_LANES), mod, vec(d)],
        out_specs=row(d),
        compiler_params=_cparams(t * d * (2 * 4 * (TOP_K + 2) + 16), 1), name="post_ffn",
    )(x1, yg, top_w, mods, gpost.reshape(1, d))


def _mm_kernel(a_ref, b_ref, o_ref, bscr):
    @pl.when(pl.program_id(1) == 0)
    def _():
        bscr[...] = b_ref[0].astype(BF16)

    o_ref[...] = _dot(a_ref[...], bscr[...]).astype(o_ref.dtype)


def _pick_tile(n, prefs):
    for t in prefs:
        if n % t == 0:
            return t
    return n


def _matmul(a, w3, layer, out_dtype):
    m, k = a.shape
    n = w3.shape[2]
    tm = _pick_tile(m, (384, 256, 128))
    tn = _pick_tile(n, (1024, 512, 256, 128))
    vm = 2 * (tm * k * 2 + k * tn * 4 + tm * tn * 4) + k * tn * 2 + tm * tn * 4
    return pl.pallas_call(
        _mm_kernel, out_shape=jax.ShapeDtypeStruct((m, n), out_dtype),
        grid=(n // tn, m // tm),
        in_specs=[pl.BlockSpec((tm, k), lambda j, i: (i, 0)),
                  pl.BlockSpec((1, k, tn), lambda j, i: (layer, 0, j))],
        out_specs=pl.BlockSpec((tm, tn), lambda j, i: (i, j)),
        scratch_shapes=[pltpu.VMEM((k, tn), BF16)],
        compiler_params=_cparams(vm, 2), name="matmul",
    )(a, w3)


def _dft_cos_sin(n):
    i = jnp.arange(n, dtype=I32)
    ang = ((i[:, None] * i[None, :]) % n).astype(F32) * (2.0 * np.pi / n)
    s = 1.0 / np.sqrt(n)
    return jnp.cos(ang) * s, jnp.sin(ang) * s


def _fft1_kernel(x_ref, f_ref, twr_ref, twi_ref, o_ref):
    n1 = f_ref.shape[1]
    tb, _, w = o_ref.shape[1:]
    xt = pltpu.einshape("abw->baw", x_ref[...])
    for j in range(tb):
        res = _dot(f_ref[...], xt[j])
        ar = res[:n1]
        ai = res[n1:]
        wr = jnp.tile(twr_ref[j], (1, w // V7X_LANES))
        wi = jnp.tile(twi_ref[j], (1, w // V7X_LANES))
        o_ref[0, j] = (ar * wr - ai * wi).astype(o_ref.dtype)
        o_ref[1, j] = (ar * wi + ai * wr).astype(o_ref.dtype)


def _fft2_kernel(m_ref, a_ref, o_ref, sr, si):
    n2, tc = a_ref.shape[1:3]
    ar = pltpu.einshape("bcw->cbw", a_ref[0])
    ai = pltpu.einshape("bcw->cbw", a_ref[1])
    for c in range(tc):
        res = _dot(m_ref[...], jnp.concatenate([ar[c], ai[c]], axis=0))
        sr[c] = res[:n2].astype(sr.dtype)
        si[c] = res[n2:].astype(si.dtype)
    o_ref[0] = pltpu.einshape("cdw->dcw", sr[...])
    o_ref[1] = pltpu.einshape("cdw->dcw", si[...])


def _dft_dense_kernel(m_ref, a_ref, o_ref):
    o_ref[...] = _dot(m_ref[...], a_ref[...]).astype(o_ref.dtype)


def _fft3_kernel(x_ref, cb_ref, sb_ref, wf_ref, bf_ref, o_ref):
    z = _dot(x_ref[0], cb_ref[...]) + _dot(x_ref[1], sb_ref[...])
    o_ref[...] = (_dot(z.astype(BF16), wf_ref[...]) + bf_ref[...]).astype(o_ref.dtype)


def _fourier_mix(p, w, w_fnet, b_fnet, n_lat):
    rows, cols = p.shape
    n_ctx = rows - n_lat
    n1 = FFT_N1
    n2 = n_lat // n1
    tb = BF16_SUBLANES
    c1, s1 = _dft_cos_sin(n1)
    f1 = jnp.concatenate([c1, -s1], axis=0).astype(BF16)
    bi = jnp.arange(n2, dtype=I32)[:, None]
    ci = jnp.arange(n1, dtype=I32)[None, :]
    tang = ((bi * ci) % n_lat).astype(F32) * (2.0 * np.pi / n_lat)
    twr = jnp.broadcast_to(jnp.cos(tang)[:, :, None], (n2, n1, V7X_LANES))
    twi = jnp.broadcast_to(-jnp.sin(tang)[:, :, None], (n2, n1, V7X_LANES))
    a = pl.pallas_call(
        _fft1_kernel, out_shape=jax.ShapeDtypeStruct((2, n2, n1, w), BF16),
        grid=(n2 // tb,),
        in_specs=[pl.BlockSpec((n1, tb, w), lambda j: (0, j, 0)),
                  pl.BlockSpec((2 * n1, n1), lambda j: (0, 0)),
                  pl.BlockSpec((tb, n1, V7X_LANES), lambda j: (j, 0, 0)),
                  pl.BlockSpec((tb, n1, V7X_LANES), lambda j: (j, 0, 0))],
        out_specs=pl.BlockSpec((2, tb, n1, w), lambda j: (0, j, 0, 0)),
        compiler_params=_cparams(2 * 3 * n1 * tb * w * 2 + 8 * n1 * w * 4, 1), name="fft_stage1",
    )(p.reshape(rows // n2, n2, cols), f1, twr, twi)
    c2, s2 = _dft_cos_sin(n2)
    m2 = jnp.concatenate([jnp.concatenate([c2, s2], axis=1),
                          jnp.concatenate([-s2, c2], axis=1)], axis=0).astype(BF16)
    tc = BF16_SUBLANES
    xl = pl.pallas_call(
        _fft2_kernel, out_shape=jax.ShapeDtypeStruct((2, n2, n1, w), BF16),
        grid=(n1 // tc,),
        in_specs=[pl.BlockSpec((2 * n2, 2 * n2), lambda j: (0, 0)),
                  pl.BlockSpec((2, n2, tc, w), lambda j: (0, 0, j, 0))],
        out_specs=pl.BlockSpec((2, n2, tc, w), lambda j: (0, 0, j, 0)),
        scratch_shapes=[pltpu.VMEM((tc, n2, w), BF16), pltpu.VMEM((tc, n2, w), BF16)],
        compiler_params=_cparams(5 * 2 * n2 * tc * w * 2 + 8 * n2 * w * 4, 1), name="fft_stage2",
    )(m2, a).reshape(2, n_lat, w)
    cc, sc = _dft_cos_sin(n_ctx)
    mc = jnp.concatenate([cc, -sc], axis=0).astype(BF16)
    xc = pl.pallas_call(
        _dft_dense_kernel, out_shape=jax.ShapeDtypeStruct((2 * n_ctx, w), BF16),
        grid=(1,),
        in_specs=[pl.BlockSpec((2 * n_ctx, n_ctx), lambda j: (0, 0)),
                  pl.BlockSpec((n_ctx, w), lambda j: (n_lat // n_ctx, 0))],
        out_specs=pl.BlockSpec((2 * n_ctx, w), lambda j: (0, 0)),
        compiler_params=_cparams(16 * n_ctx * w, 1), name="dft_context",
    )(mc, p).reshape(2, n_ctx, w)
    xall = jnp.concatenate([xl, xc], axis=1)
    cg, sg = _dft_cos_sin(HEAD_DIM)
    eye = jnp.eye(w // HEAD_DIM, dtype=F32)
    cb = jnp.kron(eye, cg).astype(BF16)
    sb = jnp.kron(eye, sg).astype(BF16)
    full = lambda r, c: pl.BlockSpec((r, c), lambda i: (0, 0))
    return pl.pallas_call(
        _fft3_kernel, out_shape=jax.ShapeDtypeStruct((rows, w), BF16),
        grid=(rows // ROW_TILE,),
        in_specs=[pl.BlockSpec((2, ROW_TILE, w), lambda i: (0, i, 0)),
                  full(w, w), full(w, w), full(w, w), full(1, w)],
        out_specs=pl.BlockSpec((ROW_TILE, w), lambda i: (i, 0)),
        compiler_params=_cparams(2 * (3 * w * w * 2 + 3 * ROW_TILE * w * 2) + ROW_TILE * w * 12, 1),
        name="fft_channel_linear",
    )(xall, cb, sb, w_fnet.astype(BF16), b_fnet.reshape(1, w))


def _na_tables(rows):
    kr = min(NA_KR, rows)
    nb = min(kr + NA_QROWS - 1, rows)
    nqb = rows // NA_QROWS
    r0 = np.arange(nqb) * NA_QROWS
    band0 = np.minimum(np.clip(r0 - kr // 2, 0, rows - kr), rows - nb)
    band_rows = band0[:, None] + np.arange(nb)[None, :]
    q_row = np.repeat(r0[:, None] + np.arange(NA_QROWS)[None, :], GRID_W, axis=1)
    q_col = np.tile(np.arange(GRID_W), NA_QROWS)
    k_row = np.repeat(band_rows, GRID_W, axis=1)
    k_col = np.tile(np.arange(GRID_W), nb)
    win_r = np.clip(q_row - kr // 2, 0, rows - kr)[:, :, None]
    win_c = np.clip(q_col - NA_KC // 2, 0, GRID_W - NA_KC)[:, None]
    kro = k_row[:, None, :]
    col_ok = (k_col[None, :] >= win_c) & (k_col[None, :] < win_c + NA_KC)
    mask = (kro >= win_r) & (kro < win_r + kr) & col_ok[None]
    d_row = np.clip(kro - q_row[:, :, None] + NA_KR - 1, 0, 2 * NA_KR - 2)
    d_col = np.clip(k_col[None, :] - q_col[:, None] + NA_KC - 1, 0, 2 * NA_KC - 2)
    pats, pid = [], np.zeros(nqb, np.int32)
    for n in range(nqb):
        for p, (m0, d0) in enumerate(pats):
            if np.array_equal(m0, mask[n]) and np.array_equal(d0, d_row[n]):
                pid[n] = p
                break
        else:
            pid[n] = len(pats)
            pats.append((mask[n], d_row[n]))
    pmask = np.stack([p[0] for p in pats])
    pdrow = np.stack([p[1] for p in pats])
    return (band0 * GRID_W).astype(np.int32), pid, pmask, pdrow, d_col, nb


def _na_bias(rpb, pmask, pdrow, d_col, nb):
    n_heads = rpb.shape[0]
    n_pat, qb, nk = pmask.shape
    tiles = rpb.astype(F32)[:, :, d_col[:GRID_W, :GRID_W]]
    drow_small = pdrow[:, ::GRID_W, ::GRID_W]
    b6 = tiles[:, drow_small]
    bias = b6.transpose(0, 1, 2, 4, 3, 5).reshape(n_heads, n_pat, qb, nk)
    return jnp.where(pmask[None], bias, MASK_VALUE)


def _na_kernel(pid_ref, st_ref, q_ref, k_ref, v_ref, cos_ref, sin_ref, bias_ref, o_ref, qs, ks,
               *, n_lat, n_ctx, nqb, qb, nk):
    scale = HEAD_DIM ** -0.5
    rt = 512 if n_lat % 512 == 0 else qb

    def rope_body(i, carry):
        r = pl.multiple_of(i * rt, rt)
        c = cos_ref[pl.ds(r, rt), :]
        s = sin_ref[pl.ds(r, rt), :]
        even = (lax.broadcasted_iota(I32, (rt, HEAD_DIM), 1) % 2) == 0

        def rot(x):
            xs = jnp.where(even, pltpu.roll(x, HEAD_DIM - 1, 1), pltpu.roll(x, 1, 1))
            return x * c + xs * s

        qs[pl.ds(r, rt), :] = (rot(q_ref[pl.ds(r, rt), :].astype(F32)) * scale).astype(BF16)
        ks[pl.ds(r, rt), :] = rot(k_ref[pl.ds(r, rt), :].astype(F32)).astype(BF16)
        return carry

    lax.fori_loop(0, n_lat // rt, rope_body, 0)
    kc = k_ref[n_lat:n_lat + n_ctx, :]
    vc = v_ref[n_lat:n_lat + n_ctx, :]

    def block_body(n, carry):
        r = pl.multiple_of(n * qb, qb)
        st = pl.multiple_of(st_ref[n], GRID_W)
        qblk = qs[pl.ds(r, qb), :]
        s_loc = _dot_nt(qblk, ks[pl.ds(st, nk), :]) + bias_ref[0, pid_ref[n]]
        s_ctx = _dot_nt(qblk, kc)
        m = jnp.maximum(jnp.max(s_loc, axis=-1, keepdims=True), jnp.max(s_ctx, axis=-1, keepdims=True))
        p_loc = jnp.exp(s_loc - m)
        p_ctx = jnp.exp(s_ctx - m)
        denom = jnp.sum(p_loc, axis=-1, keepdims=True) + jnp.sum(p_ctx, axis=-1, keepdims=True)
        o = _dot(p_loc.astype(BF16), v_ref[pl.ds(st, nk), :]) + _dot(p_ctx.astype(BF16), vc)
        o_ref[pl.ds(r, qb), :] = (o / denom).astype(o_ref.dtype)
        return carry

    lax.fori_loop(0, nqb, block_body, 0, unroll=NA_UNROLL)
    s = _dot_nt(q_ref[n_lat:n_lat + n_ctx, :], kc) * scale
    p = jnp.exp(s - jnp.max(s, axis=-1, keepdims=True))
    o = _dot(p.astype(BF16), vc) / jnp.sum(p, axis=-1, keepdims=True)
    o_ref[n_lat:n_lat + n_ctx, :] = o.astype(o_ref.dtype)


def _na(p, col0, n_heads, rpb, cos_rep, sin_sgn, n_lat):
    rows = p.shape[0]
    n_ctx = rows - n_lat
    grid_rows = n_lat // GRID_W
    st, pid, pmask, pdrow, d_col, nb = _na_tables(grid_rows)
    n_pat, qb, nk = pmask.shape
    nqb = grid_rows // NA_QROWS
    bias = _na_bias(rpb, pmask, pdrow, d_col, nb)
    head = lambda off: pl.BlockSpec((rows, HEAD_DIM), lambda h, *_: (0, col0 + off * n_heads + h))
    kern = functools.partial(_na_kernel, n_lat=n_lat, n_ctx=n_ctx, nqb=nqb, qb=qb, nk=nk)
    vm = 2 * (4 * rows * HEAD_DIM * 2 + n_pat * qb * nk * 4) + 2 * n_lat * HEAD_DIM * 4 \
        + 2 * n_lat * HEAD_DIM * 2 + 8 * qb * (nk + n_ctx) * 4
    return pl.pallas_call(
        kern, out_shape=jax.ShapeDtypeStruct((rows, n_heads * HEAD_DIM), BF16),
        grid_spec=pltpu.PrefetchScalarGridSpec(
            num_scalar_prefetch=2, grid=(n_heads,),
            in_specs=[head(0), head(1), head(2),
                      pl.BlockSpec(memory_space=pltpu.VMEM), pl.BlockSpec(memory_space=pltpu.VMEM),
                      pl.BlockSpec((1, n_pat, qb, nk), lambda h, *_: (h, 0, 0, 0))],
            out_specs=pl.BlockSpec((rows, HEAD_DIM), lambda h, *_: (0, h)),
            scratch_shapes=[pltpu.VMEM((n_lat, HEAD_DIM), BF16), pltpu.VMEM((n_lat, HEAD_DIM), BF16)]),
        compiler_params=_cparams(vm, 1), name="neighborhood_attention",
    )(jnp.asarray(pid), jnp.asarray(st), p, p, p, cos_rep, sin_sgn, bias)


def _hgrn_gates(z, lbv):
    f = lbv + (1.0 - lbv) * jax.nn.sigmoid(z)
    lf = jnp.log(jnp.maximum(f, GATE_FLOOR))
    kk = (1.0 - lbv) * jax.nn.sigmoid(-z)
    return lf, kk


def _hgrn_kernel(hq_ref, hi_ref, hf_ref, hb_ref, hg_ref, lb_ref, gn_ref, o_ref, acc, stf, stb,
                 *, n_lat, n_ctx):
    c_rows, sub = HGRN_CHUNK, HGRN_SUB
    n_sub = c_rows // sub
    dk = HEAD_DIM
    rows = n_lat + n_ctx
    lbf = lb_ref[0, 0:1, :]
    lbb = lb_ref[0, 1:2, :]

    def decay_body(i, mn):
        r = pl.multiple_of(i * c_rows, c_rows)
        lf_f, _ = _hgrn_gates(hf_ref[pl.ds(r, c_rows), :].astype(F32), lbf)
        lf_b, _ = _hgrn_gates(hb_ref[pl.ds(r, c_rows), :].astype(F32), lbb)
        for j in range(n_sub):
            mn = jnp.minimum(mn, jnp.sum(lf_f[j * sub:(j + 1) * sub], axis=0, keepdims=True))
            mn = jnp.minimum(mn, jnp.sum(lf_b[j * sub:(j + 1) * sub], axis=0, keepdims=True))
        return mn

    mn = lax.fori_loop(0, rows // c_rows, decay_body, jnp.zeros((1, dk), F32))
    safe = jnp.min(mn) >= -HGRN_SAFE_DECAY

    acc[...] = jnp.zeros(acc.shape, F32)
    stf[...] = jnp.zeros(stf.shape, F32)
    stb[...] = jnp.zeros(stb.shape, F32)

    row_i = lax.broadcasted_iota(I32, (c_rows, dk), 0)
    t_i = lax.broadcasted_iota(I32, (c_rows, c_rows), 0)
    s_i = lax.broadcasted_iota(I32, (c_rows, c_rows), 1)

    def chunk(r0, z_ref, lbv, st_ref, rev):
        lf, kk = _hgrn_gates(z_ref[pl.ds(r0, c_rows), :].astype(F32), lbv)
        q = _silu(hq_ref[pl.ds(r0, c_rows), :].astype(F32))
        v = hi_ref[pl.ds(r0, c_rows), :]
        b = lf
        step = 1
        while step < c_rows:
            if rev:
                b = b + jnp.where(row_i < c_rows - step, pltpu.roll(b, c_rows - step, 0), 0.0)
            else:
                b = b + jnp.where(row_i >= step, pltpu.roll(b, step, 0), 0.0)
            step *= 2
        zero = jnp.zeros((1, dk), F32)
        refs = []
        for i in range(n_sub):
            if rev:
                refs.append(b[(i + 1) * sub:(i + 1) * sub + 1] if i < n_sub - 1 else zero)
            else:
                refs.append(b[i * sub - 1:i * sub] if i > 0 else zero)
        ref_rows = refs[n_sub - 1]
        for i in range(n_sub - 2, -1, -1):
            ref_rows = jnp.where(row_i < (i + 1) * sub, refs[i], ref_rows)
        qd = (q * jnp.exp(b - ref_rows)).astype(BF16)
        ks = []
        for i in range(n_sub):
            lo, hi = (i * sub, c_rows) if rev else (0, (i + 1) * sub)
            part = (kk[lo:hi] * jnp.exp(refs[i] - b[lo:hi])).astype(BF16)
            if hi - lo < c_rows:
                pad = jnp.zeros((c_rows - (hi - lo), dk), BF16)
                part = jnp.concatenate([pad, part] if rev else [part, pad], axis=0)
            ks.append(part)
        scores = _dot_nt(qd, jnp.concatenate(ks, axis=0))
        own = jnp.concatenate([scores[i * sub:(i + 1) * sub, i * c_rows:(i + 1) * c_rows]
                               for i in range(n_sub)], axis=0)
        pm = jnp.where((s_i >= t_i) if rev else (s_i <= t_i), own, 0.0).astype(BF16)
        st = st_ref[...]
        o = _dot(pm, v) + _dot_nt((q * jnp.exp(b)).astype(BF16), st.astype(BF16))
        blast = b[0:1] if rev else b[c_rows - 1:c_rows]
        ke = (kk * jnp.exp(blast - b)).astype(BF16)
        st_ref[...] = st * jnp.exp(blast) + _dot_tn(v, ke)
        return o

    def fast_path():
        def run(base, n):
            def body(c, carry):
                rf = pl.multiple_of(base + c * c_rows, c_rows)
                rb = pl.multiple_of(base + (n - 1 - c) * c_rows, c_rows)
                o_f = chunk(rf, hf_ref, lbf, stf, False)
                acc[pl.ds(rf, c_rows), :] = acc[pl.ds(rf, c_rows), :] + o_f
                o_b = chunk(rb, hb_ref, lbb, stb, True)
                acc[pl.ds(rb, c_rows), :] = acc[pl.ds(rb, c_rows), :] + o_b
                return carry
            lax.fori_loop(0, n, body, 0, unroll=HGRN_UNROLL)
        run(n_lat, n_ctx // c_rows)
        run(0, n_lat // c_rows)

    def slow_path():
        pack = BF16_SUBLANES
        sub_i = lax.broadcasted_iota(I32, (pack, dk), 0)
        row8 = lax.broadcasted_iota(I32, (8, dk), 0)

        def load_row(ref, t):
            r = pl.multiple_of((t // pack) * pack, pack)
            blk = ref[pl.ds(r, pack), :].astype(F32)
            return jnp.sum(jnp.where(sub_i == t - r, blk, 0.0), axis=0, keepdims=True)

        def run(base, n, z_ref, lbv, st_ref, rev):
            def body(i, carry):
                t = base + ((n - 1 - i) if rev else i)
                lf, kk = _hgrn_gates(load_row(z_ref, t), lbv)
                q = _silu(load_row(hq_ref, t))
                v = load_row(hi_ref, t)
                v8 = jnp.where(row8 == 0, v, 0.0).astype(BF16)
                k8 = jnp.where(row8 == 0, kk, 0.0).astype(BF16)
                st = st_ref[...] * jnp.exp(lf) + _dot_tn(v8, k8)
                st_ref[...] = st
                q8 = jnp.broadcast_to(q, (8, dk)).astype(BF16)
                o = _dot_nt(q8, st.astype(BF16))
                acc[pl.ds(t, 1), :] = acc[pl.ds(t, 1), :] + o[0:1]
                return carry
            lax.fori_loop(0, n, body, 0)
        run(n_lat, n_ctx, hf_ref, lbf, stf, False)
        run(0, n_lat, hf_ref, lbf, stf, False)
        run(n_lat, n_ctx, hb_ref, lbb, stb, True)
        run(0, n_lat, hb_ref, lbb, stb, True)

    lax.cond(safe, fast_path, slow_path)

    rt = ROW_TILE

    def readout(i, carry):
        r = pl.multiple_of(i * rt, rt)
        y = _rms(acc[pl.ds(r, rt), :], gn_ref[...])
        o_ref[pl.ds(r, rt), :] = (y * _silu(hg_ref[pl.ds(r, rt), :].astype(F32))).astype(o_ref.dtype)
        return carry

    lax.fori_loop(0, rows // rt, readout, 0)


def _hgrn(p, col0, n_heads, lb, g_norm, n_lat):
    rows = p.shape[0]
    n_ctx = rows - n_lat
    head = lambda off: pl.BlockSpec((rows, HEAD_DIM), lambda h: (0, col0 + off * n_heads + h))
    lbh = lb.reshape(2, n_heads, HEAD_DIM).transpose(1, 0, 2)
    kern = functools.partial(_hgrn_kernel, n_lat=n_lat, n_ctx=n_ctx)
    n_keys = HGRN_CHUNK * HGRN_CHUNK // HGRN_SUB
    vm = 2 * 6 * rows * HEAD_DIM * 2 + rows * HEAD_DIM * 4 + 6 * HGRN_UNROLL * HGRN_CHUNK * n_keys * 4
    return pl.pallas_call(
        kern, out_shape=jax.ShapeDtypeStruct((rows, n_heads * HEAD_DIM), BF16),
        grid=(n_heads,),
        in_specs=[head(0), head(1), head(2), head(3), head(4),
                  pl.BlockSpec((1, 2, HEAD_DIM), lambda h: (h, 0, 0)),
                  pl.BlockSpec((1, HEAD_DIM), lambda h: (0, 0))],
        out_specs=pl.BlockSpec((rows, HEAD_DIM), lambda h: (0, h)),
        scratch_shapes=[pltpu.VMEM((rows, HEAD_DIM), F32), pltpu.VMEM((HEAD_DIM, HEAD_DIM), F32),
                        pltpu.VMEM((HEAD_DIM, HEAD_DIM), F32)],
        compiler_params=_cparams(vm, 1), name="hgrn2_bidir",
    )(p, p, p, p, p, lbh, g_norm.reshape(1, HEAD_DIM))


def _prep_gate_up_kernel(w_ref, perm_ref, o_ref):
    wb = w_ref[...].astype(BF16)
    g = V7X_MXU_DIM
    for j in range(w_ref.shape[1] // g):
        o_ref[:, j * g:(j + 1) * g] = _dot(wb[:, j * g:(j + 1) * g], perm_ref[...]).astype(o_ref.dtype)


def _deinterleave_perm():
    g = V7X_MXU_DIM
    src = np.arange(g)
    dst = np.where(src % 2 == 0, src // 2, g // 2 + src // 2)
    perm = np.zeros((g, g), np.float32)
    perm[src, dst] = 1.0
    return jnp.asarray(perm, BF16)


def _prep_gate_up(w4, layer):
    _, n_e, d, n = w4.shape
    tr = 512
    steps = d // tr
    out = pl.pallas_call(
        _prep_gate_up_kernel, out_shape=jax.ShapeDtypeStruct((n_e * d, n), BF16),
        grid=(n_e * steps,),
        in_specs=[pl.BlockSpec((None, None, tr, n), lambda i: (layer, i // steps, i % steps, 0)),
                  pl.BlockSpec((V7X_MXU_DIM, V7X_MXU_DIM), lambda i: (0, 0))],
        out_specs=pl.BlockSpec((tr, n), lambda i: (i, 0)),
        compiler_params=_cparams(2 * tr * n * 6 + tr * n * 6, 1), name="prep_gate_up",
    )(w4, _deinterleave_perm())
    return out.reshape(n_e, d, n)


def _cast_kernel(w_ref, o_ref):
    o_ref[...] = w_ref[...].astype(o_ref.dtype)


def _prep_down(w4, layer):
    _, n_e, ff, d = w4.shape
    return pl.pallas_call(
        _cast_kernel, out_shape=jax.ShapeDtypeStruct((n_e, ff, d), BF16),
        grid=(n_e,),
        in_specs=[pl.BlockSpec((None, 1, ff, d), lambda e: (layer, e, 0, 0))],
        out_specs=pl.BlockSpec((1, ff, d), lambda e: (e, 0, 0)),
        compiler_params=_cparams(2 * ff * d * 6, 1), name="prep_down",
    )(w4)


def _expert_kernel(be_ref, nv_ref, tok_cur, tok_nxt, h_hbm, wgu_ref, bgu_ref, wdn_ref, bdn_ref, y_ref,
                   xbuf, sems):
    b = pl.program_id(0)
    n_valid = nv_ref[0]
    live = b < n_valid
    tm = xbuf.shape[1]
    slot = b % 2

    def start_gather(tok_ref, dst_slot):
        def body(r, carry):
            t = tok_ref[0, 0, r]
            pltpu.make_async_copy(h_hbm.at[pl.ds(t, 1), :], xbuf.at[dst_slot, pl.ds(r, 1), :],
                                  sems.at[dst_slot]).start()
            return carry
        lax.fori_loop(0, tm, body, 0, unroll=8)

    @pl.when(b == 0)
    def _():
        start_gather(tok_cur, 0)

    @pl.when(b + 1 < n_valid)
    def _():
        start_gather(tok_nxt, 1 - slot)

    @pl.when(live)
    def _():
        pltpu.make_async_copy(h_hbm.at[pl.ds(0, tm), :], xbuf.at[slot], sems.at[slot]).wait()
        g, h = V7X_MXU_DIM, V7X_LANES
        half = tm // 2
        for r in (0, half):
            gu = _dot(xbuf[slot, r:r + half, :].astype(BF16), wgu_ref[0]) + bgu_ref[0]
            groups = range(gu.shape[1] // g)
            gate = jnp.minimum(jnp.concatenate([gu[:, j * g:j * g + h] for j in groups], axis=1), SWIGLU_LIMIT)
            up = jnp.clip(jnp.concatenate([gu[:, j * g + h:(j + 1) * g] for j in groups], axis=1),
                          -SWIGLU_LIMIT, SWIGLU_LIMIT)
            act = gate * jax.nn.sigmoid(SWIGLU_ALPHA * gate) * (up + 1.0)
            y_ref[r:r + half, :] = (_dot(act.astype(BF16), wdn_ref[0]) + bdn_ref[0]).astype(y_ref.dtype)

    @pl.when(jnp.logical_not(live))
    def _():
        y_ref[...] = jnp.zeros(y_ref.shape, y_ref.dtype)


def _experts(h2, row_tok, block_e, n_valid, wgu, bgu, wdn, bdn):
    d = h2.shape[1]
    n_blocks = row_tok.shape[0] // MOE_TILE
    ff = wdn.shape[1]
    tok3 = row_tok.reshape(n_blocks, 1, MOE_TILE)
    tok_spec = lambda off: pl.BlockSpec((1, 1, MOE_TILE),
                                        lambda b, be, nv: (jnp.minimum(b + off, n_blocks - 1), 0, 0),
                                        memory_space=pltpu.SMEM)
    vm = 2 * (d * 2 * ff * 2 + ff * d * 2 + MOE_TILE * d * 2) + 2 * MOE_TILE * d * 4 \
        + MOE_TILE * (2 * ff * 8 + d * 6)
    return pl.pallas_call(
        _expert_kernel, out_shape=jax.ShapeDtypeStruct((n_blocks * MOE_TILE, d), BF16),
        grid_spec=pltpu.PrefetchScalarGridSpec(
            num_scalar_prefetch=2, grid=(n_blocks,),
            in_specs=[tok_spec(0), tok_spec(1),
                      pl.BlockSpec(memory_space=pl.ANY),
                      pl.BlockSpec((1, d, 2 * ff), lambda b, be, nv: (be[b], 0, 0)),
                      pl.BlockSpec((1, 1, 2 * ff), lambda b, be, nv: (be[b], 0, 0)),
                      pl.BlockSpec((1, ff, d), lambda b, be, nv: (be[b], 0, 0)),
                      pl.BlockSpec((1, 1, d), lambda b, be, nv: (be[b], 0, 0))],
            out_specs=pl.BlockSpec((MOE_TILE, d), lambda b, be, nv: (b, 0)),
            scratch_shapes=[pltpu.VMEM((2, MOE_TILE, d), F32), pltpu.SemaphoreType.DMA((2,))]),
        compiler_params=_cparams(vm, 1), name="expert_ffn",
    )(block_e, n_valid, tok3, tok3, h2, wgu, bgu, wdn, bdn)


def _route(top_idx, n_tok):
    flat_e = top_idx[:n_tok, :TOP_K].reshape(-1)
    n_pairs = flat_e.shape[0]
    onehot = (flat_e[:, None] == jnp.arange(N_EXPERTS, dtype=I32)[None, :]).astype(I32)
    csum = jnp.cumsum(onehot, axis=0)
    rank = jnp.sum(onehot * csum, axis=1) - 1
    counts = csum[-1]
    padded = (counts + MOE_TILE - 1) // MOE_TILE * MOE_TILE
    pend = jnp.cumsum(padded)
    dest = jnp.sum(onehot * (pend - padded)[None, :], axis=1) + rank
    n_blocks = -(-(n_pairs + N_EXPERTS * (MOE_TILE - 1)) // MOE_TILE)
    row_tok = jnp.zeros((n_blocks * MOE_TILE,), I32).at[dest].set(jnp.arange(n_pairs, dtype=I32) // TOP_K)
    n_valid = (pend[-1] // MOE_TILE).astype(I32).reshape(1)
    blk = jnp.minimum(jnp.arange(n_blocks, dtype=I32), n_valid[0] - 1) * MOE_TILE
    block_e = jnp.minimum(jnp.sum((pend[None, :] <= blk[:, None]).astype(I32), axis=1), N_EXPERTS - 1)
    return dest.reshape(n_tok, TOP_K), row_tok, block_e, n_valid


def _moe(h2, top_idx, n_tok, wgu, bgu, wdn, bdn):
    dest, row_tok, block_e, n_valid = _route(top_idx, n_tok)
    y = _experts(h2, row_tok, block_e, n_valid, wgu, bgu, wdn, bdn)
    yg = y.at[dest.T.reshape(-1)].get(mode='promise_in_bounds')
    return yg.reshape(TOP_K, n_tok, -1)


def _rope_tables(n_tok):
    t = jnp.arange(n_tok, dtype=I32)
    row = (t // GRID_W).astype(F32)
    col = (t % GRID_W).astype(F32)
    pairs = HEAD_DIM // 4
    inv_freq = ROPE_THETA ** (-jnp.arange(pairs, dtype=F32) / pairs)
    ang = jnp.concatenate([row[:, None] * inv_freq, col[:, None] * inv_freq], axis=-1)
    cos_rep = jnp.repeat(jnp.cos(ang), 2, axis=-1)
    sin_sgn = jnp.stack([-jnp.sin(ang), jnp.sin(ang)], axis=-1).reshape(n_tok, HEAD_DIM)
    return cos_rep, sin_sgn


def kernel(x, c, ctx, c_ctx, w_ada, b_ada, norm_pre_mix, norm_post_mix, norm_pre_ffn, norm_post_ffn,
           w_in, w_fnet, b_fnet, na_rpb, hgrn_lb_logits, hgrn_out_norm, w_out,
           w_router, b_router, w_gate_up, b_gate_up, w_down, b_down):
    batch, n_lat, d = x.shape
    n_ctx = ctx.shape[1]
    assert batch == 1 and c.shape[0] == 1
    depth = w_ada.shape[0]
    fnet_w = w_fnet.shape[1]
    n_heads = na_rpb.shape[1]
    ff = w_down.shape[2]
    fnet_blocks = fnet_w // HEAD_DIM

    cvec = jnp.zeros((8, d), F32).at[0].set(c[0]).at[1].set(c_ctx)
    ada = _ada(cvec, w_ada, b_ada)
    mods = jnp.pad(ada[:, :2].reshape(depth, 2, 6, d), ((0, 0), (0, 0), (0, 2), (0, 0)))
    cos_rep, sin_sgn = _rope_tables(n_lat)
    p_lb = jax.nn.softmax(hgrn_lb_logits.astype(F32), axis=0)
    lower_bounds = jnp.cumsum(p_lb, axis=0) - p_lb[0]
    groups = 2 * ff // V7X_MXU_DIM

    xs = jnp.concatenate([x[0], ctx[0]], axis=0)
    h = _pre(xs, mods[0], norm_pre_mix[0], n_lat)
    for layer in range(depth):
        last = layer == depth - 1
        p = _matmul(h, w_in, layer, BF16)
        yf = _fourier_mix(p, fnet_w, w_fnet[layer], b_fnet[layer], n_lat)
        na = _na(p, fnet_blocks, n_heads, na_rpb[layer], cos_rep, sin_sgn, n_lat)
        hg = _hgrn(p, fnet_blocks + 3 * n_heads, n_heads, lower_bounds[layer], hgrn_out_norm[layer], n_lat)
        mix = jnp.concatenate([yf, na, hg], axis=1)
        y = _matmul(mix, w_out, layer, BF16)
        x1, h2, top_idx, top_w = _mid(xs, y, mods[layer], norm_post_mix[layer], norm_pre_ffn[layer],
                                      w_router[layer], b_router[layer], n_lat)
        bgu = b_gate_up[layer].reshape(N_EXPERTS, groups, V7X_LANES, 2).transpose(0, 1, 3, 2)
        yg = _moe(h2, top_idx, n_lat if last else n_lat + n_ctx,
                  _prep_gate_up(w_gate_up, layer), bgu.reshape(N_EXPERTS, 1, 2 * ff),
                  _prep_down(w_down, layer), b_down[layer].reshape(N_EXPERTS, 1, d))
        if last:
            xs = _final(x1, yg, top_w, mods[layer], norm_post_ffn[layer], n_lat)
        else:
            xs, h = _end(x1, yg, top_w, mods[layer], norm_post_ffn[layer], mods[layer + 1],
                         norm_pre_mix[layer + 1], n_lat)
    return xs[:n_lat].reshape(batch, n_lat, d)
```

```python
import functools

import numpy as np
import jax
import jax.numpy as jnp
from jax import lax
from jax.experimental import pallas as pl
from jax.experimental.pallas import tpu as pltpu

F32 = jnp.float32
BF16 = jnp.bfloat16
I32 = jnp.int32

GRID_W = 64
HEAD_DIM = 128
NA_KR = 8
NA_KC = 16
NA_QROWS = 2
ROPE_THETA = 10000.0
N_EXPERTS = 32
TOP_K = 4
SWIGLU_LIMIT = 7.0
SWIGLU_ALPHA = 1.702
RMS_EPS = 1e-6
MASK_VALUE = -1e30
GATE_FLOOR = 1e-30

V7X_VMEM_BYTES = 64 * 1024 * 1024
V7X_LANES = 128
V7X_MXU_DIM = 256
BF16_SUBLANES = 16

ROW_TILE = 256
COMBINE_TILE = 128
MOE_TILE = 256
MOE_WCHUNKS = 8
HGRN_CHUNK = 128
HGRN_SUB = 16
HGRN_SAFE_DECAY = 80.0
HGRN_UNROLL = 4
NA_UNROLL = 2
FFT_N1 = 64


def _cparams(vmem_bytes, n_grid):
    limit = int(min(max(vmem_bytes * 5 // 4 + (4 << 20), 32 << 20), V7X_VMEM_BYTES - (4 << 20)))
    return pltpu.CompilerParams(dimension_semantics=("arbitrary",) * n_grid, vmem_limit_bytes=limit)


def _dot(a, b):
    return jnp.dot(a, b, preferred_element_type=F32)


def _dot_nt(a, b):
    return lax.dot_general(a, b, (((1,), (1,)), ((), ())), preferred_element_type=F32)


def _dot_tn(a, b):
    return lax.dot_general(a, b, (((0,), (0,)), ((), ())), preferred_element_type=F32)


def _silu(x):
    return x * jax.nn.sigmoid(x)


def _rms(x, g):
    return x * lax.rsqrt(jnp.mean(x * x, axis=-1, keepdims=True) + RMS_EPS) * g


def _ada_kernel(c_ref, w_ref, b_ref, o_ref):
    a = _silu(c_ref[...])
    a_hi = a.astype(BF16)
    a_lo = (a - a_hi.astype(F32)).astype(BF16)
    w = w_ref[0].astype(BF16)
    o_ref[0] = _dot(a_hi, w) + _dot(a_lo, w) + b_ref[0]


def _ada(cvec, w_ada, b_ada):
    depth, d, n = w_ada.shape
    tn = 512
    return pl.pallas_call(
        _ada_kernel,
        out_shape=jax.ShapeDtypeStruct((depth, 8, n), F32),
        grid=(depth, n // tn),
        in_specs=[pl.BlockSpec((8, d), lambda l, j: (0, 0)),
                  pl.BlockSpec((1, d, tn), lambda l, j: (l, 0, j)),
                  pl.BlockSpec((1, 1, tn), lambda l, j: (l, 0, j))],
        out_specs=pl.BlockSpec((1, 8, tn), lambda l, j: (l, 0, j)),
        compiler_params=_cparams(2 * d * tn * 4 + d * tn * 2, 2),
        name="adaln",
    )(cvec, w_ada, b_ada.reshape(depth, 1, n))


def _pre_kernel(x_ref, mod_ref, g_ref, h_ref):
    y = _rms(x_ref[...], g_ref[...])
    h_ref[...] = (y * (1.0 + mod_ref[0, 1:2, :]) + mod_ref[0, 0:1, :]).astype(h_ref.dtype)


def _mid_kernel(x_ref, y_ref, mod_ref, gpost_ref, gpre_ref, wr_ref, br_ref,
                x1_ref, h2_ref, idx_ref, tw_ref):
    x1 = x_ref[...] + mod_ref[0, 2:3, :] * _rms(y_ref[...].astype(F32), gpost_ref[...])
    x1_ref[...] = x1
    h2 = _rms(x1, gpre_ref[...]) * (1.0 + mod_ref[0, 4:5, :]) + mod_ref[0, 3:4, :]
    h2_ref[...] = h2.astype(h2_ref.dtype)
    logits = jnp.dot(h2, wr_ref[...], precision=lax.Precision.HIGHEST,
                     preferred_element_type=F32) + br_ref[...]
    lane = lax.broadcasted_iota(I32, logits.shape, 1)
    idx_acc = jnp.zeros(logits.shape, I32)
    top_acc = jnp.full(logits.shape, -jnp.inf, F32)
    work = logits
    for r in range(TOP_K):
        m = jnp.max(work, axis=-1, keepdims=True)
        sel = jnp.min(jnp.where(work == m, lane, V7X_LANES), axis=-1, keepdims=True)
        idx_acc = jnp.where(lane == r, sel, idx_acc)
        top_acc = jnp.where(lane == r, m, top_acc)
        work = jnp.where(lane == sel, -jnp.inf, work)
    e = jnp.exp(top_acc - jnp.max(top_acc, axis=-1, keepdims=True))
    idx_ref[...] = idx_acc
    tw_ref[...] = e / jnp.sum(e, axis=-1, keepdims=True)


def _combine(yg_ref, tw_ref):
    tw = tw_ref[...]
    fx = tw[:, 0:1] * yg_ref[0].astype(F32)
    for k in range(1, TOP_K):
        fx = fx + tw[:, k:k + 1] * yg_ref[k].astype(F32)
    return fx


def _end_kernel(x1_ref, yg_ref, tw_ref, mod_ref, gpost_ref, modn_ref, gpren_ref, x2_ref, h_ref):
    x2 = x1_ref[...] + mod_ref[0, 5:6, :] * _rms(_combine(yg_ref, tw_ref), gpost_ref[...])
    x2_ref[...] = x2
    y = _rms(x2, gpren_ref[...])
    h_ref[...] = (y * (1.0 + modn_ref[0, 1:2, :]) + modn_ref[0, 0:1, :]).astype(h_ref.dtype)


def _final_kernel(x1_ref, yg_ref, tw_ref, mod_ref, gpost_ref, x2_ref):
    x2_ref[...] = x1_ref[...] + mod_ref[0, 5:6, :] * _rms(_combine(yg_ref, tw_ref), gpost_ref[...])


def _row_specs(tile, d, n_lat):
    row = lambda w: pl.BlockSpec((tile, w), lambda i: (i, 0))
    mod = pl.BlockSpec((1, 8, d), lambda i: (jnp.minimum(i // (n_lat // tile), 1), 0, 0))
    vec = lambda w: pl.BlockSpec((1, w), lambda i: (0, 0))
    return row, mod, vec


def _pre(x, mods, g, n_lat):
    rows, d = x.shape
    row, mod, vec = _row_specs(ROW_TILE, d, n_lat)
    return pl.pallas_call(
        _pre_kernel, out_shape=jax.ShapeDtypeStruct((rows, d), BF16),
        grid=(rows // ROW_TILE,), in_specs=[row(d), mod, vec(d)], out_specs=row(d),
        compiler_params=_cparams(ROW_TILE * d * 24, 1), name="pre_norm",
    )(x, mods, g.reshape(1, d))


def _mid(x, y, mods, gpost, gpre, w_router, b_router, n_lat):
    rows, d = x.shape
    row, mod, vec = _row_specs(ROW_TILE, d, n_lat)
    wr = jnp.zeros((d, V7X_LANES), F32).at[:, :N_EXPERTS].set(w_router)
    br = jnp.full((1, V7X_LANES), -jnp.inf, F32).at[0, :N_EXPERTS].set(b_router)
    return pl.pallas_call(
        _mid_kernel,
        out_shape=(jax.ShapeDtypeStruct((rows, d), F32), jax.ShapeDtypeStruct((rows, d), F32),
                   jax.ShapeDtypeStruct((rows, V7X_LANES), I32), jax.ShapeDtypeStruct((rows, V7X_LANES), F32)),
        grid=(rows // ROW_TILE,),
        in_specs=[row(d), row(d), mod, vec(d), vec(d),
                  pl.BlockSpec((d, V7X_LANES), lambda i: (0, 0)), vec(V7X_LANES)],
        out_specs=(row(d), row(d), row(V7X_LANES), row(V7X_LANES)),
        compiler_params=_cparams(ROW_TILE * d * 48 + d * V7X_LANES * 8, 1), name="post_mix_router",
    )(x, y, mods, gpost.reshape(1, d), gpre.reshape(1, d), wr, br)


def _end(x1, yg, top_w, mods, gpost, mods_next, gpre_next, n_lat):
    rows, d = x1.shape
    t = COMBINE_TILE
    row, mod, vec = _row_specs(t, d, n_lat)
    return pl.pallas_call(
        _end_kernel,
        out_shape=(jax.ShapeDtypeStruct((rows, d), F32), jax.ShapeDtypeStruct((rows, d), BF16)),
        grid=(rows // t,),
        in_specs=[row(d), pl.BlockSpec((TOP_K, t, d), lambda i: (0, i, 0)), row(V7X_LANES),
                  mod, vec(d), mod, vec(d)],
        out_specs=(row(d), row(d)),
        compiler_params=_cparams(t * d * (2 * 4 * (TOP_K + 2) + 2 * 2 + 16), 1), name="post_ffn_pre_norm",
    )(x1, yg, top_w, mods, gpost.reshape(1, d), mods_next, gpre_next.reshape(1, d))


def _final(x1, yg, top_w, mods, gpost, n_lat):
    d = x1.shape[1]
    t = COMBINE_TILE
    row, mod, vec = _row_specs(t, d, n_lat)
    return pl.pallas_call(
        _final_kernel, out_shape=jax.ShapeDtypeStruct((n_lat, d), F32),
        grid=(n_lat // t,),
        in_specs=[row(d), pl.BlockSpec((TOP_K, t, d), lambda i: (0, i, 0)), row(V7X_LANES), mod, vec(d)],
        out_specs=row(d),
        compiler_params=_cparams(t * d * (2 * 4 * (TOP_K + 2) + 16), 1), name="post_ffn",
    )(x1, yg, top_w, mods, gpost.reshape(1, d))


def _mm_kernel(a_ref, b_ref, o_ref, bscr):
    @pl.when(pl.program_id(1) == 0)
    def _():
        bscr[...] = b_ref[0].astype(BF16)

    o_ref[...] = _dot(a_ref[...], bscr[...]).astype(o_ref.dtype)


def _pick_tile(n, prefs):
    for t in prefs:
        if n % t == 0:
            return t
    return n


def _matmul(a, w3, layer, out_dtype):
    m, k = a.shape
    n = w3.shape[2]
    tm = _pick_tile(m, (384, 256, 128))
    tn = _pick_tile(n, (1024, 512, 256, 128))
    vm = 2 * (tm * k * 2 + k * tn * 4 + tm * tn * 4) + k * tn * 2 + tm * tn * 4
    return pl.pallas_call(
        _mm_kernel, out_shape=jax.ShapeDtypeStruct((m, n), out_dtype),
        grid=(n // tn, m // tm),
        in_specs=[pl.BlockSpec((tm, k), lambda j, i: (i, 0)),
                  pl.BlockSpec((1, k, tn), lambda j, i: (layer, 0, j))],
        out_specs=pl.BlockSpec((tm, tn), lambda j, i: (i, j)),
        scratch_shapes=[pltpu.VMEM((k, tn), BF16)],
        compiler_params=_cparams(vm, 2), name="matmul",
    )(a, w3)


def _dft_cos_sin(n):
    i = jnp.arange(n, dtype=I32)
    ang = ((i[:, None] * i[None, :]) % n).astype(F32) * (2.0 * np.pi / n)
    s = 1.0 / np.sqrt(n)
    return jnp.cos(ang) * s, jnp.sin(ang) * s


def _fft1_kernel(x_ref, f_ref, twr_ref, twi_ref, o_ref):
    n1 = f_ref.shape[1]
    tb, _, w = o_ref.shape[1:]
    xt = pltpu.einshape("abw->baw", x_ref[...])
    for j in range(tb):
        res = _dot(f_ref[...], xt[j])
        ar = res[:n1]
        ai = res[n1:]
        wr = jnp.tile(twr_ref[j], (1, w // V7X_LANES))
        wi = jnp.tile(twi_ref[j], (1, w // V7X_LANES))
        o_ref[0, j] = (ar * wr - ai * wi).astype(o_ref.dtype)
        o_ref[1, j] = (ar * wi + ai * wr).astype(o_ref.dtype)


def _fft2_kernel(m_ref, a_ref, o_ref, sr, si):
    n2, tc = a_ref.shape[1:3]
    ar = pltpu.einshape("bcw->cbw", a_ref[0])
    ai = pltpu.einshape("bcw->cbw", a_ref[1])
    for c in range(tc):
        res = _dot(m_ref[...], jnp.concatenate([ar[c], ai[c]], axis=0))
        sr[c] = res[:n2].astype(sr.dtype)
        si[c] = res[n2:].astype(si.dtype)
    o_ref[0] = pltpu.einshape("cdw->dcw", sr[...])
    o_ref[1] = pltpu.einshape("cdw->dcw", si[...])


def _dft_dense_kernel(m_ref, a_ref, o_ref):
    o_ref[...] = _dot(m_ref[...], a_ref[...]).astype(o_ref.dtype)


def _fft3_kernel(x_ref, cb_ref, sb_ref, wf_ref, bf_ref, o_ref):
    z = _dot(x_ref[0], cb_ref[...]) + _dot(x_ref[1], sb_ref[...])
    o_ref[...] = (_dot(z.astype(BF16), wf_ref[...]) + bf_ref[...]).astype(o_ref.dtype)


def _fourier_mix(p, w, w_fnet, b_fnet, n_lat):
    rows, cols = p.shape
    n_ctx = rows - n_lat
    n1 = FFT_N1
    n2 = n_lat // n1
    tb = BF16_SUBLANES
    c1, s1 = _dft_cos_sin(n1)
    f1 = jnp.concatenate([c1, -s1], axis=0).astype(BF16)
    bi = jnp.arange(n2, dtype=I32)[:, None]
    ci = jnp.arange(n1, dtype=I32)[None, :]
    tang = ((bi * ci) % n_lat).astype(F32) * (2.0 * np.pi / n_lat)
    twr = jnp.broadcast_to(jnp.cos(tang)[:, :, None], (n2, n1, V7X_LANES))
    twi = jnp.broadcast_to(-jnp.sin(tang)[:, :, None], (n2, n1, V7X_LANES))
    a = pl.pallas_call(
        _fft1_kernel, out_shape=jax.ShapeDtypeStruct((2, n2, n1, w), BF16),
        grid=(n2 // tb,),
        in_specs=[pl.BlockSpec((n1, tb, w), lambda j: (0, j, 0)),
                  pl.BlockSpec((2 * n1, n1), lambda j: (0, 0)),
                  pl.BlockSpec((tb, n1, V7X_LANES), lambda j: (j, 0, 0)),
                  pl.BlockSpec((tb, n1, V7X_LANES), lambda j: (j, 0, 0))],
        out_specs=pl.BlockSpec((2, tb, n1, w), lambda j: (0, j, 0, 0)),
        compiler_params=_cparams(2 * 3 * n1 * tb * w * 2 + 8 * n1 * w * 4, 1), name="fft_stage1",
    )(p.reshape(rows // n2, n2, cols), f1, twr, twi)
    c2, s2 = _dft_cos_sin(n2)
    m2 = jnp.concatenate([jnp.concatenate([c2, s2], axis=1),
                          jnp.concatenate([-s2, c2], axis=1)], axis=0).astype(BF16)
    tc = BF16_SUBLANES
    xl = pl.pallas_call(
        _fft2_kernel, out_shape=jax.ShapeDtypeStruct((2, n2, n1, w), BF16),
        grid=(n1 // tc,),
        in_specs=[pl.BlockSpec((2 * n2, 2 * n2), lambda j: (0, 0)),
                  pl.BlockSpec((2, n2, tc, w), lambda j: (0, 0, j, 0))],
        out_specs=pl.BlockSpec((2, n2, tc, w), lambda j: (0, 0, j, 0)),
        scratch_shapes=[pltpu.VMEM((tc, n2, w), BF16), pltpu.VMEM((tc, n2, w), BF16)],
        compiler_params=_cparams(5 * 2 * n2 * tc * w * 2 + 8 * n2 * w * 4, 1), name="fft_stage2",
    )(m2, a).reshape(2, n_lat, w)
    cc, sc = _dft_cos_sin(n_ctx)
    mc = jnp.concatenate([cc, -sc], axis=0).astype(BF16)
    xc = pl.pallas_call(
        _dft_dense_kernel, out_shape=jax.ShapeDtypeStruct((2 * n_ctx, w), BF16),
        grid=(1,),
        in_specs=[pl.BlockSpec((2 * n_ctx, n_ctx), lambda j: (0, 0)),
                  pl.BlockSpec((n_ctx, w), lambda j: (n_lat // n_ctx, 0))],
        out_specs=pl.BlockSpec((2 * n_ctx, w), lambda j: (0, 0)),
        compiler_params=_cparams(16 * n_ctx * w, 1), name="dft_context",
    )(mc, p).reshape(2, n_ctx, w)
    xall = jnp.concatenate([xl, xc], axis=1)
    cg, sg = _dft_cos_sin(HEAD_DIM)
    eye = jnp.eye(w // HEAD_DIM, dtype=F32)
    cb = jnp.kron(eye, cg).astype(BF16)
    sb = jnp.kron(eye, sg).astype(BF16)
    full = lambda r, c: pl.BlockSpec((r, c), lambda i: (0, 0))
    return pl.pallas_call(
        _fft3_kernel, out_shape=jax.ShapeDtypeStruct((rows, w), BF16),
        grid=(rows // ROW_TILE,),
        in_specs=[pl.BlockSpec((2, ROW_TILE, w), lambda i: (0, i, 0)),
                  full(w, w), full(w, w), full(w, w), full(1, w)],
        out_specs=pl.BlockSpec((ROW_TILE, w), lambda i: (i, 0)),
        compiler_params=_cparams(2 * (3 * w * w * 2 + 3 * ROW_TILE * w * 2) + ROW_TILE * w * 12, 1),
        name="fft_channel_linear",
    )(xall, cb, sb, w_fnet.astype(BF16), b_fnet.reshape(1, w))


def _na_tables(rows):
    kr = min(NA_KR, rows)
    nb = min(kr + NA_QROWS - 1, rows)
    nqb = rows // NA_QROWS
    r0 = np.arange(nqb) * NA_QROWS
    band0 = np.minimum(np.clip(r0 - kr // 2, 0, rows - kr), rows - nb)
    band_rows = band0[:, None] + np.arange(nb)[None, :]
    q_row = np.repeat(r0[:, None] + np.arange(NA_QROWS)[None, :], GRID_W, axis=1)
    q_col = np.tile(np.arange(GRID_W), NA_QROWS)
    k_row = np.repeat(band_rows, GRID_W, axis=1)
    k_col = np.tile(np.arange(GRID_W), nb)
    win_r = np.clip(q_row - kr // 2, 0, rows - kr)[:, :, None]
    win_c = np.clip(q_col - NA_KC // 2, 0, GRID_W - NA_KC)[:, None]
    kro = k_row[:, None, :]
    col_ok = (k_col[None, :] >= win_c) & (k_col[None, :] < win_c + NA_KC)
    mask = (kro >= win_r) & (kro < win_r + kr) & col_ok[None]
    d_row = np.clip(kro - q_row[:, :, None] + NA_KR - 1, 0, 2 * NA_KR - 2)
    d_col = np.clip(k_col[None, :] - q_col[:, None] + NA_KC - 1, 0, 2 * NA_KC - 2)
    pats, pid = [], np.zeros(nqb, np.int32)
    for n in range(nqb):
        for p, (m0, d0) in enumerate(pats):
            if np.array_equal(m0, mask[n]) and np.array_equal(d0, d_row[n]):
                pid[n] = p
                break
        else:
            pid[n] = len(pats)
            pats.append((mask[n], d_row[n]))
    pmask = np.stack([p[0] for p in pats])
    pdrow = np.stack([p[1] for p in pats])
    return (band0 * GRID_W).astype(np.int32), pid, pmask, pdrow, d_col, nb


def _na_bias(rpb, pmask, pdrow, d_col, nb):
    n_heads = rpb.shape[0]
    n_pat, qb, nk = pmask.shape
    tiles = rpb.astype(F32)[:, :, d_col[:GRID_W, :GRID_W]]
    drow_small = pdrow[:, ::GRID_W, ::GRID_W]
    b6 = tiles[:, drow_small]
    bias = b6.transpose(0, 1, 2, 4, 3, 5).reshape(n_heads, n_pat, qb, nk)
    return jnp.where(pmask[None], bias, MASK_VALUE)


def _na_kernel(pid_ref, st_ref, q_ref, k_ref, v_ref, cos_ref, sin_ref, bias_ref, o_ref, qs, ks,
               *, n_lat, n_ctx, nqb, qb, nk):
    scale = HEAD_DIM ** -0.5
    rt = 512 if n_lat % 512 == 0 else qb

    def rope_body(i, carry):
        r = pl.multiple_of(i * rt, rt)
        c = cos_ref[pl.ds(r, rt), :]
        s = sin_ref[pl.ds(r, rt), :]
        even = (lax.broadcasted_iota(I32, (rt, HEAD_DIM), 1) % 2) == 0

        def rot(x):
            xs = jnp.where(even, pltpu.roll(x, HEAD_DIM - 1, 1), pltpu.roll(x, 1, 1))
            return x * c + xs * s

        qs[pl.ds(r, rt), :] = (rot(q_ref[pl.ds(r, rt), :].astype(F32)) * scale).astype(BF16)
        ks[pl.ds(r, rt), :] = rot(k_ref[pl.ds(r, rt), :].astype(F32)).astype(BF16)
        return carry

    lax.fori_loop(0, n_lat // rt, rope_body, 0)
    kc = k_ref[n_lat:n_lat + n_ctx, :]
    vc = v_ref[n_lat:n_lat + n_ctx, :]

    def block_body(n, carry):
        r = pl.multiple_of(n * qb, qb)
        st = pl.multiple_of(st_ref[n], GRID_W)
        qblk = qs[pl.ds(r, qb), :]
        s_loc = _dot_nt(qblk, ks[pl.ds(st, nk), :]) + bias_ref[0, pid_ref[n]]
        s_ctx = _dot_nt(qblk, kc)
        m = jnp.maximum(jnp.max(s_loc, axis=-1, keepdims=True), jnp.max(s_ctx, axis=-1, keepdims=True))
        p_loc = jnp.exp(s_loc - m)
        p_ctx = jnp.exp(s_ctx - m)
        denom = jnp.sum(p_loc, axis=-1, keepdims=True) + jnp.sum(p_ctx, axis=-1, keepdims=True)
        o = _dot(p_loc.astype(BF16), v_ref[pl.ds(st, nk), :]) + _dot(p_ctx.astype(BF16), vc)
        o_ref[pl.ds(r, qb), :] = (o / denom).astype(o_ref.dtype)
        return carry

    lax.fori_loop(0, nqb, block_body, 0, unroll=NA_UNROLL)
    s = _dot_nt(q_ref[n_lat:n_lat + n_ctx, :], kc) * scale
    p = jnp.exp(s - jnp.max(s, axis=-1, keepdims=True))
    o = _dot(p.astype(BF16), vc) / jnp.sum(p, axis=-1, keepdims=True)
    o_ref[n_lat:n_lat + n_ctx, :] = o.astype(o_ref.dtype)


def _na(p, col0, n_heads, rpb, cos_rep, sin_sgn, n_lat):
    rows = p.shape[0]
    n_ctx = rows - n_lat
    grid_rows = n_lat // GRID_W
    st, pid, pmask, pdrow, d_col, nb = _na_tables(grid_rows)
    n_pat, qb, nk = pmask.shape
    nqb = grid_rows // NA_QROWS
    bias = _na_bias(rpb, pmask, pdrow, d_col, nb)
    head = lambda off: pl.BlockSpec((rows, HEAD_DIM), lambda h, *_: (0, col0 + off * n_heads + h))
    kern = functools.partial(_na_kernel, n_lat=n_lat, n_ctx=n_ctx, nqb=nqb, qb=qb, nk=nk)
    vm = 2 * (4 * rows * HEAD_DIM * 2 + n_pat * qb * nk * 4) + 2 * n_lat * HEAD_DIM * 4 \
        + 2 * n_lat * HEAD_DIM * 2 + 8 * qb * (nk + n_ctx) * 4
    return pl.pallas_call(
        kern, out_shape=jax.ShapeDtypeStruct((rows, n_heads * HEAD_DIM), BF16),
        grid_spec=pltpu.PrefetchScalarGridSpec(
            num_scalar_prefetch=2, grid=(n_heads,),
            in_specs=[head(0), head(1), head(2),
                      pl.BlockSpec(memory_space=pltpu.VMEM), pl.BlockSpec(memory_space=pltpu.VMEM),
                      pl.BlockSpec((1, n_pat, qb, nk), lambda h, *_: (h, 0, 0, 0))],
            out_specs=pl.BlockSpec((rows, HEAD_DIM), lambda h, *_: (0, h)),
            scratch_shapes=[pltpu.VMEM((n_lat, HEAD_DIM), BF16), pltpu.VMEM((n_lat, HEAD_DIM), BF16)]),
        compiler_params=_cparams(vm, 1), name="neighborhood_attention",
    )(jnp.asarray(pid), jnp.asarray(st), p, p, p, cos_rep, sin_sgn, bias)


def _hgrn_gates(z, lbv):
    f = lbv + (1.0 - lbv) * jax.nn.sigmoid(z)
    lf = jnp.log(jnp.maximum(f, GATE_FLOOR))
    kk = (1.0 - lbv) * jax.nn.sigmoid(-z)
    return lf, kk


def _hgrn_kernel(hq_ref, hi_ref, hf_ref, hb_ref, hg_ref, lb_ref, gn_ref, o_ref, acc, stf, stb,
                 *, n_lat, n_ctx):
    c_rows, sub = HGRN_CHUNK, HGRN_SUB
    n_sub = c_rows // sub
    dk = HEAD_DIM
    rows = n_lat + n_ctx
    lbf = lb_ref[0, 0:1, :]
    lbb = lb_ref[0, 1:2, :]

    def decay_body(i, mn):
        r = pl.multiple_of(i * c_rows, c_rows)
        lf_f, _ = _hgrn_gates(hf_ref[pl.ds(r, c_rows), :].astype(F32), lbf)
        lf_b, _ = _hgrn_gates(hb_ref[pl.ds(r, c_rows), :].astype(F32), lbb)
        for j in range(n_sub):
            mn = jnp.minimum(mn, jnp.sum(lf_f[j * sub:(j + 1) * sub], axis=0, keepdims=True))
            mn = jnp.minimum(mn, jnp.sum(lf_b[j * sub:(j + 1) * sub], axis=0, keepdims=True))
        return mn

    mn = lax.fori_loop(0, rows // c_rows, decay_body, jnp.zeros((1, dk), F32))
    safe = jnp.min(mn) >= -HGRN_SAFE_DECAY

    acc[...] = jnp.zeros(acc.shape, F32)
    stf[...] = jnp.zeros(stf.shape, F32)
    stb[...] = jnp.zeros(stb.shape, F32)

    row_i = lax.broadcasted_iota(I32, (c_rows, dk), 0)
    t_i = lax.broadcasted_iota(I32, (c_rows, c_rows), 0)
    s_i = lax.broadcasted_iota(I32, (c_rows, c_rows), 1)

    def chunk(r0, z_ref, lbv, st_ref, rev):
        lf, kk = _hgrn_gates(z_ref[pl.ds(r0, c_rows), :].astype(F32), lbv)
        q = _silu(hq_ref[pl.ds(r0, c_rows), :].astype(F32))
        v = hi_ref[pl.ds(r0, c_rows), :]
        b = lf
        step = 1
        while step < c_rows:
            if rev:
                b = b + jnp.where(row_i < c_rows - step, pltpu.roll(b, c_rows - step, 0), 0.0)
            else:
                b = b + jnp.where(row_i >= step, pltpu.roll(b, step, 0), 0.0)
            step *= 2
        zero = jnp.zeros((1, dk), F32)
        refs = []
        for i in range(n_sub):
            if rev:
                refs.append(b[(i + 1) * sub:(i + 1) * sub + 1] if i < n_sub - 1 else zero)
            else:
                refs.append(b[i * sub - 1:i * sub] if i > 0 else zero)
        ref_rows = refs[n_sub - 1]
        for i in range(n_sub - 2, -1, -1):
            ref_rows = jnp.where(row_i < (i + 1) * sub, refs[i], ref_rows)
        qd = (q * jnp.exp(b - ref_rows)).astype(BF16)
        ks = []
        for i in range(n_sub):
            lo, hi = (i * sub, c_rows) if rev else (0, (i + 1) * sub)
            part = (kk[lo:hi] * jnp.exp(refs[i] - b[lo:hi])).astype(BF16)
            if hi - lo < c_rows:
                pad = jnp.zeros((c_rows - (hi - lo), dk), BF16)
                part = jnp.concatenate([pad, part] if rev else [part, pad], axis=0)
            ks.append(part)
        scores = _dot_nt(qd, jnp.concatenate(ks, axis=0))
        own = jnp.concatenate([scores[i * sub:(i + 1) * sub, i * c_rows:(i + 1) * c_rows]
                               for i in range(n_sub)], axis=0)
        pm = jnp.where((s_i >= t_i) if rev else (s_i <= t_i), own, 0.0).astype(BF16)
        st = st_ref[...]
        o = _dot(pm, v) + _dot_nt((q * jnp.exp(b)).astype(BF16), st.astype(BF16))
        blast = b[0:1] if rev else b[c_rows - 1:c_rows]
        ke = (kk * jnp.exp(blast - b)).astype(BF16)
        st_ref[...] = st * jnp.exp(blast) + _dot_tn(v, ke)
        return o

    def fast_path():
        def run(base, n):
            def body(c, carry):
                rf = pl.multiple_of(base + c * c_rows, c_rows)
                rb = pl.multiple_of(base + (n - 1 - c) * c_rows, c_rows)
                o_f = chunk(rf, hf_ref, lbf, stf, False)
                acc[pl.ds(rf, c_rows), :] = acc[pl.ds(rf, c_rows), :] + o_f
                o_b = chunk(rb, hb_ref, lbb, stb, True)
                acc[pl.ds(rb, c_rows), :] = acc[pl.ds(rb, c_rows), :] + o_b
                return carry
            lax.fori_loop(0, n, body, 0, unroll=HGRN_UNROLL)
        run(n_lat, n_ctx // c_rows)
        run(0, n_lat // c_rows)

    def slow_path():
        pack = BF16_SUBLANES
        sub_i = lax.broadcasted_iota(I32, (pack, dk), 0)
        row8 = lax.broadcasted_iota(I32, (8, dk), 0)

        def load_row(ref, t):
            r = pl.multiple_of((t // pack) * pack, pack)
            blk = ref[pl.ds(r, pack), :].astype(F32)
            return jnp.sum(jnp.where(sub_i == t - r, blk, 0.0), axis=0, keepdims=True)

        def run(base, n, z_ref, lbv, st_ref, rev):
            def body(i, carry):
                t = base + ((n - 1 - i) if rev else i)
                lf, kk = _hgrn_gates(load_row(z_ref, t), lbv)
                q = _silu(load_row(hq_ref, t))
                v = load_row(hi_ref, t)
                v8 = jnp.where(row8 == 0, v, 0.0).astype(BF16)
                k8 = jnp.where(row8 == 0, kk, 0.0).astype(BF16)
                st = st_ref[...] * jnp.exp(lf) + _dot_tn(v8, k8)
                st_ref[...] = st
                q8 = jnp.broadcast_to(q, (8, dk)).astype(BF16)
                o = _dot_nt(q8, st.astype(BF16))
                acc[pl.ds(t, 1), :] = acc[pl.ds(t, 1), :] + o[0:1]
                return carry
            lax.fori_loop(0, n, body, 0)
        run(n_lat, n_ctx, hf_ref, lbf, stf, False)
        run(0, n_lat, hf_ref, lbf, stf, False)
        run(n_lat, n_ctx, hb_ref, lbb, stb, True)
        run(0, n_lat, hb_ref, lbb, stb, True)

    lax.cond(safe, fast_path, slow_path)

    rt = ROW_TILE

    def readout(i, carry):
        r = pl.multiple_of(i * rt, rt)
        y = _rms(acc[pl.ds(r, rt), :], gn_ref[...])
        o_ref[pl.ds(r, rt), :] = (y * _silu(hg_ref[pl.ds(r, rt), :].astype(F32))).astype(o_ref.dtype)
        return carry

    lax.fori_loop(0, rows // rt, readout, 0)


def _hgrn(p, col0, n_heads, lb, g_norm, n_lat):
    rows = p.shape[0]
    n_ctx = rows - n_lat
    head = lambda off: pl.BlockSpec((rows, HEAD_DIM), lambda h: (0, col0 + off * n_heads + h))
    lbh = lb.reshape(2, n_heads, HEAD_DIM).transpose(1, 0, 2)
    kern = functools.partial(_hgrn_kernel, n_lat=n_lat, n_ctx=n_ctx)
    n_keys = HGRN_CHUNK * HGRN_CHUNK // HGRN_SUB
    vm = 2 * 6 * rows * HEAD_DIM * 2 + rows * HEAD_DIM * 4 + 6 * HGRN_UNROLL * HGRN_CHUNK * n_keys * 4
    return pl.pallas_call(
        kern, out_shape=jax.ShapeDtypeStruct((rows, n_heads * HEAD_DIM), BF16),
        grid=(n_heads,),
        in_specs=[head(0), head(1), head(2), head(3), head(4),
                  pl.BlockSpec((1, 2, HEAD_DIM), lambda h: (h, 0, 0)),
                  pl.BlockSpec((1, HEAD_DIM), lambda h: (0, 0))],
        out_specs=pl.BlockSpec((rows, HEAD_DIM), lambda h: (0, h)),
        scratch_shapes=[pltpu.VMEM((rows, HEAD_DIM), F32), pltpu.VMEM((HEAD_DIM, HEAD_DIM), F32),
                        pltpu.VMEM((HEAD_DIM, HEAD_DIM), F32)],
        compiler_params=_cparams(vm, 1), name="hgrn2_bidir",
    )(p, p, p, p, p, lbh, g_norm.reshape(1, HEAD_DIM))


STEP_LOAD, STEP_COMPUTE, STEP_IDLE, STEP_ZERO = 0, 1, 2, 3


def _deinterleave_perm():
    g = V7X_MXU_DIM
    src = np.arange(g)
    dst = np.where(src % 2 == 0, src // 2, g // 2 + src // 2)
    perm = np.zeros((g, g), np.float32)
    perm[src, dst] = 1.0
    return jnp.asarray(perm, BF16)


def _expert_kernel(kind_ref, wexp_ref, wchunk_ref, wslot_ref, blk_ref, cexp_ref, cslot_ref, nv_ref,
                   tok_cur, tok_nxt, h_hbm, wgu_ref, wdn_ref, bgu_ref, bdn_ref, perm_ref, y_ref,
                   wgu_s, wdn_s, xbuf, sems):
    s = pl.program_id(0)
    kind = kind_ref[s]
    tm = xbuf.shape[1]

    @pl.when(kind == STEP_LOAD)
    def _():
        c = wchunk_ref[s]
        sl = wslot_ref[s]
        rg = wgu_ref.shape[0]
        rd = wdn_ref.shape[0]
        wgu_s[sl, pl.ds(pl.multiple_of(c * rg, rg), rg), :] = wgu_ref[...].astype(BF16)
        wdn_s[sl, pl.ds(pl.multiple_of(c * rd, rd), rd), :] = wdn_ref[...].astype(BF16)

    @pl.when(kind == STEP_ZERO)
    def _():
        y_ref[...] = jnp.zeros(y_ref.shape, y_ref.dtype)

    @pl.when(kind == STEP_COMPUTE)
    def _():
        b = blk_ref[s]
        n_valid = nv_ref[0]
        slot = b % 2
        cs = cslot_ref[s]

        def start_gather(tok_ref, dst_slot):
            def body(r, carry):
                t = tok_ref[0, 0, r]
                pltpu.make_async_copy(h_hbm.at[pl.ds(t, 1), :], xbuf.at[dst_slot, pl.ds(r, 1), :],
                                      sems.at[dst_slot]).start()
                return carry
            lax.fori_loop(0, tm, body, 0, unroll=8)

        @pl.when(b == 0)
        def _():
            start_gather(tok_cur, 0)

        @pl.when(b + 1 < n_valid)
        def _():
            start_gather(tok_nxt, 1 - slot)

        pltpu.make_async_copy(h_hbm.at[pl.ds(0, tm), :], xbuf.at[slot], sems.at[slot]).wait()
        g, h = V7X_MXU_DIM, V7X_LANES
        half = tm // 2
        perm = perm_ref[...]
        for r in (0, half):
            gu = _dot(xbuf[slot, r:r + half, :].astype(BF16), wgu_s[cs]) + bgu_ref[0]
            hi = gu.astype(BF16)
            lo = (gu - hi.astype(F32)).astype(BF16)
            parts = [_dot(hi[:, j * g:(j + 1) * g], perm) + _dot(lo[:, j * g:(j + 1) * g], perm)
                     for j in range(gu.shape[1] // g)]
            gate = jnp.minimum(jnp.concatenate([p[:, :h] for p in parts], axis=1), SWIGLU_LIMIT)
            up = jnp.clip(jnp.concatenate([p[:, h:] for p in parts], axis=1), -SWIGLU_LIMIT, SWIGLU_LIMIT)
            act = gate * jax.nn.sigmoid(SWIGLU_ALPHA * gate) * (up + 1.0)
            y_ref[r:r + half, :] = (_dot(act.astype(BF16), wdn_s[cs]) + bdn_ref[0]).astype(y_ref.dtype)


def _expert_program(counts, block_e, n_valid, n_blocks):
    n_e, n_ch = N_EXPERTS, MOE_WCHUNKS
    n_steps = n_blocks + n_ch * n_e
    nb = (counts + MOE_TILE - 1) // MOE_TILE
    active = nb > 0
    bstart = jnp.cumsum(nb) - nb
    e_idx = jnp.arange(n_e, dtype=I32)
    at_or_after = lax.cummin(jnp.where(active, e_idx, n_e)[::-1], axis=0)[::-1]
    next_active = jnp.concatenate([at_or_after[1:], jnp.full((1,), n_e, I32)])
    first_active = at_or_after[0]
    nl = jnp.where(active & (next_active < n_e), n_ch, 0)
    seg_len = jnp.where(active, nb + nl, 0)
    seg_start = n_ch + jnp.cumsum(seg_len) - seg_len
    slot = (jnp.cumsum(active.astype(I32)) - active.astype(I32)) % 2
    j = jnp.arange(n_ch, dtype=I32)
    lpos = jnp.where((nl > 0)[:, None], seg_start[:, None] + j[None, :] + jnp.minimum(j[None, :], nb[:, None]),
                     n_steps).reshape(-1)
    lpos = jnp.concatenate([j, lpos])
    lexp = jnp.concatenate([jnp.broadcast_to(first_active, (n_ch,)),
                            jnp.broadcast_to(jnp.minimum(next_active, n_e - 1)[:, None], (n_e, n_ch)).reshape(-1)])
    lslot = jnp.concatenate([jnp.zeros((n_ch,), I32),
                             jnp.broadcast_to((1 - slot)[:, None], (n_e, n_ch)).reshape(-1)])
    lchunk = jnp.concatenate([j, jnp.broadcast_to(j[None, :], (n_e, n_ch)).reshape(-1)])
    b = jnp.arange(n_blocks, dtype=I32)
    jb = b - bstart[block_e]
    cpos = jnp.where(b < n_valid[0], seg_start[block_e] + jb + jnp.minimum(jb + 1, nl[block_e]), n_steps)
    fill = lambda pos, val: jnp.zeros((n_steps + 1,), I32).at[pos].set(val)[:n_steps]
    kind = jnp.full((n_steps + 1,), STEP_IDLE, I32).at[lpos].set(STEP_LOAD).at[cpos].set(STEP_COMPUTE)[:n_steps]
    step = jnp.arange(n_steps, dtype=I32)
    last_load = lax.cummax(jnp.where(kind == STEP_LOAD, step, 0), axis=0)
    last_comp = lax.cummax(jnp.where(kind == STEP_COMPUTE, step, -1), axis=0)
    wexp = fill(lpos, lexp)[last_load]
    wchunk = fill(lpos, lchunk)[last_load]
    wslot = fill(lpos, lslot)[last_load]
    blk = jnp.where(last_comp >= 0, fill(cpos, b)[jnp.maximum(last_comp, 0)], 0)
    n_used = n_ch + jnp.sum(seg_len)
    spare = n_valid[0] + step - n_used
    kind = jnp.where((step >= n_used) & (spare < n_blocks), STEP_ZERO, kind)
    blk = jnp.where(step >= n_used, jnp.minimum(spare, n_blocks - 1), blk)
    cexp = block_e[blk]
    return kind, wexp, wchunk, wslot, blk, cexp, slot[cexp]


def _experts(h2, row_tok, counts, block_e, n_valid, w_gate_up, b_gate_up, w_down, b_down, layer):
    d = h2.shape[1]
    n_blocks = row_tok.shape[0] // MOE_TILE
    ff = w_down.shape[2]
    n_ch = MOE_WCHUNKS
    prog = _expert_program(counts, block_e, n_valid, n_blocks)
    n_steps = prog[0].shape[0]
    tok3 = row_tok.reshape(n_blocks, 1, MOE_TILE)
    tok_spec = lambda off: pl.BlockSpec(
        (1, 1, MOE_TILE), lambda s, kind, we, wc, ws, blk, *_: (jnp.minimum(blk[s] + off, n_blocks - 1), 0, 0),
        memory_space=pltpu.SMEM)
    wspec = lambda r, c: pl.BlockSpec((None, None, r, c), lambda s, kind, we, wc, *_: (layer, we[s], wc[s], 0))
    bspec = lambda c: pl.BlockSpec((1, 1, c), lambda s, kind, we, wc, ws, blk, ce, *_: (ce[s], 0, 0))
    vm = 2 * (d * 2 * ff * 2 + ff * d * 2) + 2 * (d * 2 * ff + ff * d) * 4 // n_ch \
        + 2 * MOE_TILE * d * 4 + 2 * MOE_TILE * d * 2 + MOE_TILE * (2 * ff * 10 + d * 6)
    return pl.pallas_call(
        _expert_kernel, out_shape=jax.ShapeDtypeStruct((n_blocks * MOE_TILE, d), BF16),
        grid_spec=pltpu.PrefetchScalarGridSpec(
            num_scalar_prefetch=8, grid=(n_steps,),
            in_specs=[tok_spec(0), tok_spec(1),
                      pl.BlockSpec(memory_space=pl.ANY),
                      wspec(d // n_ch, 2 * ff), wspec(ff // n_ch, d), bspec(2 * ff), bspec(d),
                      pl.BlockSpec((V7X_MXU_DIM, V7X_MXU_DIM), lambda s, *_: (0, 0))],
            out_specs=pl.BlockSpec((MOE_TILE, d), lambda s, kind, we, wc, ws, blk, *_: (blk[s], 0)),
            scratch_shapes=[pltpu.VMEM((2, d, 2 * ff), BF16), pltpu.VMEM((2, ff, d), BF16),
                            pltpu.VMEM((2, MOE_TILE, d), F32), pltpu.SemaphoreType.DMA((2,))]),
        compiler_params=_cparams(vm, 1), name="expert_ffn",
    )(*prog, n_valid, tok3, tok3, h2, w_gate_up, w_down,
      b_gate_up[layer].reshape(N_EXPERTS, 1, 2 * ff), b_down[layer].reshape(N_EXPERTS, 1, d), _deinterleave_perm())


def _route(top_idx, n_tok):
    flat_e = top_idx[:n_tok, :TOP_K].reshape(-1)
    n_pairs = flat_e.shape[0]
    onehot = (flat_e[:, None] == jnp.arange(N_EXPERTS, dtype=I32)[None, :]).astype(I32)
    csum = jnp.cumsum(onehot, axis=0)
    rank = jnp.sum(onehot * csum, axis=1) - 1
    counts = csum[-1]
    padded = (counts + MOE_TILE - 1) // MOE_TILE * MOE_TILE
    pend = jnp.cumsum(padded)
    dest = jnp.sum(onehot * (pend - padded)[None, :], axis=1) + rank
    n_blocks = -(-(n_pairs + N_EXPERTS * (MOE_TILE - 1)) // MOE_TILE)
    row_tok = jnp.zeros((n_blocks * MOE_TILE,), I32).at[dest].set(jnp.arange(n_pairs, dtype=I32) // TOP_K)
    n_valid = (pend[-1] // MOE_TILE).astype(I32).reshape(1)
    blk = jnp.minimum(jnp.arange(n_blocks, dtype=I32), n_valid[0] - 1) * MOE_TILE
    block_e = jnp.minimum(jnp.sum((pend[None, :] <= blk[:, None]).astype(I32), axis=1), N_EXPERTS - 1)
    return dest.reshape(n_tok, TOP_K), row_tok, counts, block_e, n_valid


def _moe(h2, top_idx, n_tok, w_gate_up, b_gate_up, w_down, b_down, layer):
    dest, row_tok, counts, block_e, n_valid = _route(top_idx, n_tok)
    y = _experts(h2, row_tok, counts, block_e, n_valid, w_gate_up, b_gate_up, w_down, b_down, layer)
    yg = y.at[dest.T.reshape(-1)].get(mode='promise_in_bounds')
    return yg.reshape(TOP_K, n_tok, -1)


def _rope_tables(n_tok):
    t = jnp.arange(n_tok, dtype=I32)
    row = (t // GRID_W).astype(F32)
    col = (t % GRID_W).astype(F32)
    pairs = HEAD_DIM // 4
    inv_freq = ROPE_THETA ** (-jnp.arange(pairs, dtype=F32) / pairs)
    ang = jnp.concatenate([row[:, None] * inv_freq, col[:, None] * inv_freq], axis=-1)
    cos_rep = jnp.repeat(jnp.cos(ang), 2, axis=-1)
    sin_sgn = jnp.stack([-jnp.sin(ang), jnp.sin(ang)], axis=-1).reshape(n_tok, HEAD_DIM)
    return cos_rep, sin_sgn


def kernel(x, c, ctx, c_ctx, w_ada, b_ada, norm_pre_mix, norm_post_mix, norm_pre_ffn, norm_post_ffn,
           w_in, w_fnet, b_fnet, na_rpb, hgrn_lb_logits, hgrn_out_norm, w_out,
           w_router, b_router, w_gate_up, b_gate_up, w_down, b_down):
    batch, n_lat, d = x.shape
    n_ctx = ctx.shape[1]
    assert batch == 1 and c.shape[0] == 1
    depth = w_ada.shape[0]
    fnet_w = w_fnet.shape[1]
    n_heads = na_rpb.shape[1]
    ff = w_down.shape[2]
    fnet_blocks = fnet_w // HEAD_DIM

    cvec = jnp.zeros((8, d), F32).at[0].set(c[0]).at[1].set(c_ctx)
    ada = _ada(cvec, w_ada, b_ada)
    mods = jnp.pad(ada[:, :2].reshape(depth, 2, 6, d), ((0, 0), (0, 0), (0, 2), (0, 0)))
    cos_rep, sin_sgn = _rope_tables(n_lat)
    p_lb = jax.nn.softmax(hgrn_lb_logits.astype(F32), axis=0)
    lower_bounds = jnp.cumsum(p_lb, axis=0) - p_lb[0]
    groups = 2 * ff // V7X_MXU_DIM

    xs = jnp.concatenate([x[0], ctx[0]], axis=0)
    h = _pre(xs, mods[0], norm_pre_mix[0], n_lat)
    for layer in range(depth):
        last = layer == depth - 1
        p = _matmul(h, w_in, layer, BF16)
        yf = _fourier_mix(p, fnet_w, w_fnet[layer], b_fnet[layer], n_lat)
        na = _na(p, fnet_blocks, n_heads, na_rpb[layer], cos_rep, sin_sgn, n_lat)
        hg = _hgrn(p, fnet_blocks + 3 * n_heads, n_heads, lower_bounds[layer], hgrn_out_norm[layer], n_lat)
        mix = jnp.concatenate([yf, na, hg], axis=1)
        y = _matmul(mix, w_out, layer, BF16)
        x1, h2, top_idx, top_w = _mid(xs, y, mods[layer], norm_post_mix[layer], norm_pre_ffn[layer],
                                      w_router[layer], b_router[layer], n_lat)
        yg = _moe(h2, top_idx, n_lat if last else n_lat + n_ctx, w_gate_up, b_gate_up, w_down, b_down, layer)
        if last:
            xs = _final(x1, yg, top_w, mods[layer], norm_post_ffn[layer], n_lat)
        else:
            xs, h = _end(x1, yg, top_w, mods[layer], norm_post_ffn[layer], mods[layer + 1],
                         norm_pre_mix[layer + 1], n_lat)
    return xs[:n_lat].reshape(batch, n_lat, d)
```

```python
import functools

import numpy as np
import jax
import jax.numpy as jnp
from jax import lax
from jax.experimental import pallas as pl
from jax.experimental.pallas import tpu as pltpu

F32 = jnp.float32
BF16 = jnp.bfloat16
I32 = jnp.int32

GRID_W = 64
HEAD_DIM = 128
NA_KR = 8
NA_KC = 16
NA_QROWS = 2
ROPE_THETA = 10000.0
N_EXPERTS = 32
TOP_K = 4
SWIGLU_LIMIT = 7.0
SWIGLU_ALPHA = 1.702
RMS_EPS = 1e-6
MASK_VALUE = -1e30
GATE_FLOOR = 1e-30

V7X_VMEM_BYTES = 64 * 1024 * 1024
V7X_LANES = 128
V7X_MXU_DIM = 256
BF16_SUBLANES = 16

ROW_TILE = 256
COMBINE_TILE = 128
MOE_TILE = 256
MOE_WCHUNKS = 8
HGRN_CHUNK = 128
HGRN_SUB = 16
HGRN_SAFE_DECAY = 80.0
HGRN_UNROLL = 4
NA_UNROLL = 2
FFT_N1 = 64


def _cparams(vmem_bytes, n_grid):
    limit = int(min(max(vmem_bytes * 5 // 4 + (4 << 20), 32 << 20), V7X_VMEM_BYTES - (4 << 20)))
    return pltpu.CompilerParams(dimension_semantics=("arbitrary",) * n_grid, vmem_limit_bytes=limit)


def _dot(a, b):
    return jnp.dot(a, b, preferred_element_type=F32)


def _dot_nt(a, b):
    return lax.dot_general(a, b, (((1,), (1,)), ((), ())), preferred_element_type=F32)


def _dot_tn(a, b):
    return lax.dot_general(a, b, (((0,), (0,)), ((), ())), preferred_element_type=F32)


def _silu(x):
    return x * jax.nn.sigmoid(x)


def _rms(x, g):
    return x * lax.rsqrt(jnp.mean(x * x, axis=-1, keepdims=True) + RMS_EPS) * g


def _ada_kernel(c_ref, w_ref, b_ref, o_ref):
    a = _silu(c_ref[...])
    a_hi = a.astype(BF16)
    a_lo = (a - a_hi.astype(F32)).astype(BF16)
    w = w_ref[0].astype(BF16)
    o_ref[0] = _dot(a_hi, w) + _dot(a_lo, w) + b_ref[0]


def _ada(cvec, w_ada, b_ada):
    depth, d, n = w_ada.shape
    tn = 512
    return pl.pallas_call(
        _ada_kernel,
        out_shape=jax.ShapeDtypeStruct((depth, 8, n), F32),
        grid=(depth, n // tn),
        in_specs=[pl.BlockSpec((8, d), lambda l, j: (0, 0)),
                  pl.BlockSpec((1, d, tn), lambda l, j: (l, 0, j)),
                  pl.BlockSpec((1, 1, tn), lambda l, j: (l, 0, j))],
        out_specs=pl.BlockSpec((1, 8, tn), lambda l, j: (l, 0, j)),
        compiler_params=_cparams(2 * d * tn * 4 + d * tn * 2, 2),
        name="adaln",
    )(cvec, w_ada, b_ada.reshape(depth, 1, n))


def _pre_kernel(x_ref, mod_ref, g_ref, h_ref):
    y = _rms(x_ref[...], g_ref[...])
    h_ref[...] = (y * (1.0 + mod_ref[0, 1:2, :]) + mod_ref[0, 0:1, :]).astype(h_ref.dtype)


def _mid_kernel(x_ref, y_ref, mod_ref, gpost_ref, gpre_ref, wr_ref, br_ref,
                x1_ref, h2_ref, idx_ref, tw_ref):
    x1 = x_ref[...] + mod_ref[0, 2:3, :] * _rms(y_ref[...].astype(F32), gpost_ref[...])
    x1_ref[...] = x1
    h2 = _rms(x1, gpre_ref[...]) * (1.0 + mod_ref[0, 4:5, :]) + mod_ref[0, 3:4, :]
    h2_ref[...] = h2.astype(h2_ref.dtype)
    logits = jnp.dot(h2, wr_ref[...], precision=lax.Precision.HIGHEST,
                     preferred_element_type=F32) + br_ref[...]
    lane = lax.broadcasted_iota(I32, logits.shape, 1)
    idx_acc = jnp.zeros(logits.shape, I32)
    top_acc = jnp.full(logits.shape, -jnp.inf, F32)
    work = logits
    for r in range(TOP_K):
        m = jnp.max(work, axis=-1, keepdims=True)
        sel = jnp.min(jnp.where(work == m, lane, V7X_LANES), axis=-1, keepdims=True)
        idx_acc = jnp.where(lane == r, sel, idx_acc)
        top_acc = jnp.where(lane == r, m, top_acc)
        work = jnp.where(lane == sel, -jnp.inf, work)
    e = jnp.exp(top_acc - jnp.max(top_acc, axis=-1, keepdims=True))
    idx_ref[...] = idx_acc
    tw_ref[...] = e / jnp.sum(e, axis=-1, keepdims=True)


def _combine(yg_ref, tw_ref):
    tw = tw_ref[...]
    fx = tw[:, 0:1] * yg_ref[0].astype(F32)
    for k in range(1, TOP_K):
        fx = fx + tw[:, k:k + 1] * yg_ref[k].astype(F32)
    return fx


def _end_kernel(x1_ref, yg_ref, tw_ref, mod_ref, gpost_ref, modn_ref, gpren_ref, x2_ref, h_ref):
    x2 = x1_ref[...] + mod_ref[0, 5:6, :] * _rms(_combine(yg_ref, tw_ref), gpost_ref[...])
    x2_ref[...] = x2
    y = _rms(x2, gpren_ref[...])
    h_ref[...] = (y * (1.0 + modn_ref[0, 1:2, :]) + modn_ref[0, 0:1, :]).astype(h_ref.dtype)


def _final_kernel(x1_ref, yg_ref, tw_ref, mod_ref, gpost_ref, x2_ref):
    x2_ref[...] = x1_ref[...] + mod_ref[0, 5:6, :] * _rms(_combine(yg_ref, tw_ref), gpost_ref[...])


def _row_specs(tile, d, n_lat):
    row = lambda w: pl.BlockSpec((tile, w), lambda i: (i, 0))
    mod = pl.BlockSpec((1, 8, d), lambda i: (jnp.minimum(i // (n_lat // tile), 1), 0, 0))
    vec = lambda w: pl.BlockSpec((1, w), lambda i: (0, 0))
    return row, mod, vec


def _pre(x, mods, g, n_lat):
    rows, d = x.shape
    row, mod, vec = _row_specs(ROW_TILE, d, n_lat)
    return pl.pallas_call(
        _pre_kernel, out_shape=jax.ShapeDtypeStruct((rows, d), BF16),
        grid=(rows // ROW_TILE,), in_specs=[row(d), mod, vec(d)], out_specs=row(d),
        compiler_params=_cparams(ROW_TILE * d * 24, 1), name="pre_norm",
    )(x, mods, g.reshape(1, d))


def _mid(x, y, mods, gpost, gpre, w_router, b_router, n_lat):
    rows, d = x.shape
    row, mod, vec = _row_specs(ROW_TILE, d, n_lat)
    wr = jnp.zeros((d, V7X_LANES), F32).at[:, :N_EXPERTS].set(w_router)
    br = jnp.full((1, V7X_LANES), -jnp.inf, F32).at[0, :N_EXPERTS].set(b_router)
    return pl.pallas_call(
        _mid_kernel,
        out_shape=(jax.ShapeDtypeStruct((rows, d), F32), jax.ShapeDtypeStruct((rows, d), F32),
                   jax.ShapeDtypeStruct((rows, V7X_LANES), I32), jax.ShapeDtypeStruct((rows, V7X_LANES), F32)),
        grid=(rows // ROW_TILE,),
        in_specs=[row(d), row(d), mod, vec(d), vec(d),
                  pl.BlockSpec((d, V7X_LANES), lambda i: (0, 0)), vec(V7X_LANES)],
        out_specs=(row(d), row(d), row(V7X_LANES), row(V7X_LANES)),
        compiler_params=_cparams(ROW_TILE * d * 48 + d * V7X_LANES * 8, 1), name="post_mix_router",
    )(x, y, mods, gpost.reshape(1, d), gpre.reshape(1, d), wr, br)


def _end(x1, yg, top_w, mods, gpost, mods_next, gpre_next, n_lat):
    rows, d = x1.shape
    t = COMBINE_TILE
    row, mod, vec = _row_specs(t, d, n_lat)
    return pl.pallas_call(
        _end_kernel,
        out_shape=(jax.ShapeDtypeStruct((rows, d), F32), jax.ShapeDtypeStruct((rows, d), BF16)),
        grid=(rows // t,),
        in_specs=[row(d), pl.BlockSpec((TOP_K, t, d), lambda i: (0, i, 0)), row(V7X_LANES),
                  mod, vec(d), mod, vec(d)],
        out_specs=(row(d), row(d)),
        compiler_params=_cparams(t * d * (2 * 4 * (TOP_K + 2) + 2 * 2 + 16), 1), name="post_ffn_pre_norm",
    )(x1, yg, top_w, mods, gpost.reshape(1, d), mods_next, gpre_next.reshape(1, d))


def _final(x1, yg, top_w, mods, gpost, n_lat):
    d = x1.shape[1]
    t = COMBINE_TILE
    row, mod, vec = _row_specs(t, d, n_lat)
    return pl.pallas_call(
        _final_kernel, out_shape=jax.ShapeDtypeStruct((n_lat, d), F32),
        grid=(n_lat // t,),
        in_specs=[row(d), pl.BlockSpec((TOP_K, t, d), lambda i: (0, i, 0)), row(V7X_LANES), mod, vec(d)],
        out_specs=row(d),
        compiler_params=_cparams(t * d * (2 * 4 * (TOP_K + 2) + 16), 1), name="post_ffn",
    )(x1, yg, top_w, mods, gpost.reshape(1, d))


def _mm_kernel(a_ref, b_ref, o_ref, bscr):
    @pl.when(pl.program_id(1) == 0)
    def _():
        bscr[...] = b_ref[0].astype(BF16)

    o_ref[...] = _dot(a_ref[...], bscr[...]).astype(o_ref.dtype)


def _pick_tile(n, prefs):
    for t in prefs:
        if n % t == 0:
            return t
    return n


def _matmul(a, w3, layer, out_dtype):
    m, k = a.shape
    n = w3.shape[2]
    tm = _pick_tile(m, (384, 256, 128))
    tn = _pick_tile(n, (1024, 512, 256, 128))
    vm = 2 * (tm * k * 2 + k * tn * 4 + tm * tn * 4) + k * tn * 2 + tm * tn * 4
    return pl.pallas_call(
        _mm_kernel, out_shape=jax.ShapeDtypeStruct((m, n), out_dtype),
        grid=(n // tn, m // tm),
        in_specs=[pl.BlockSpec((tm, k), lambda j, i: (i, 0)),
                  pl.BlockSpec((1, k, tn), lambda j, i: (layer, 0, j))],
        out_specs=pl.BlockSpec((tm, tn), lambda j, i: (i, j)),
        scratch_shapes=[pltpu.VMEM((k, tn), BF16)],
        compiler_params=_cparams(vm, 2), name="matmul",
    )(a, w3)


def _dft_cos_sin(n):
    i = jnp.arange(n, dtype=I32)
    ang = ((i[:, None] * i[None, :]) % n).astype(F32) * (2.0 * np.pi / n)
    s = 1.0 / np.sqrt(n)
    return jnp.cos(ang) * s, jnp.sin(ang) * s


def _fft1_kernel(x_ref, f_ref, twr_ref, twi_ref, o_ref):
    n1 = f_ref.shape[1]
    tb, _, w = o_ref.shape[1:]
    xt = pltpu.einshape("abw->baw", x_ref[...])
    for j in range(tb):
        res = _dot(f_ref[...], xt[j])
        ar = res[:n1]
        ai = res[n1:]
        wr = jnp.tile(twr_ref[j], (1, w // V7X_LANES))
        wi = jnp.tile(twi_ref[j], (1, w // V7X_LANES))
        o_ref[0, j] = (ar * wr - ai * wi).astype(o_ref.dtype)
        o_ref[1, j] = (ar * wi + ai * wr).astype(o_ref.dtype)


def _fft2_kernel(m_ref, a_ref, o_ref, sr, si):
    n2, tc = a_ref.shape[1:3]
    ar = pltpu.einshape("bcw->cbw", a_ref[0])
    ai = pltpu.einshape("bcw->cbw", a_ref[1])
    for c in range(tc):
        res = _dot(m_ref[...], jnp.concatenate([ar[c], ai[c]], axis=0))
        sr[c] = res[:n2].astype(sr.dtype)
        si[c] = res[n2:].astype(si.dtype)
    o_ref[0] = pltpu.einshape("cdw->dcw", sr[...])
    o_ref[1] = pltpu.einshape("cdw->dcw", si[...])


def _dft_dense_kernel(m_ref, a_ref, o_ref):
    o_ref[...] = _dot(m_ref[...], a_ref[...]).astype(o_ref.dtype)


def _fft3_kernel(x_ref, cb_ref, sb_ref, wf_ref, bf_ref, o_ref):
    z = _dot(x_ref[0], cb_ref[...]) + _dot(x_ref[1], sb_ref[...])
    o_ref[...] = (_dot(z.astype(BF16), wf_ref[...]) + bf_ref[...]).astype(o_ref.dtype)


def _fourier_mix(p, w, w_fnet, b_fnet, n_lat):
    rows, cols = p.shape
    n_ctx = rows - n_lat
    n1 = FFT_N1
    n2 = n_lat // n1
    tb = BF16_SUBLANES
    c1, s1 = _dft_cos_sin(n1)
    f1 = jnp.concatenate([c1, -s1], axis=0).astype(BF16)
    bi = jnp.arange(n2, dtype=I32)[:, None]
    ci = jnp.arange(n1, dtype=I32)[None, :]
    tang = ((bi * ci) % n_lat).astype(F32) * (2.0 * np.pi / n_lat)
    twr = jnp.broadcast_to(jnp.cos(tang)[:, :, None], (n2, n1, V7X_LANES))
    twi = jnp.broadcast_to(-jnp.sin(tang)[:, :, None], (n2, n1, V7X_LANES))
    a = pl.pallas_call(
        _fft1_kernel, out_shape=jax.ShapeDtypeStruct((2, n2, n1, w), BF16),
        grid=(n2 // tb,),
        in_specs=[pl.BlockSpec((n1, tb, w), lambda j: (0, j, 0)),
                  pl.BlockSpec((2 * n1, n1), lambda j: (0, 0)),
                  pl.BlockSpec((tb, n1, V7X_LANES), lambda j: (j, 0, 0)),
                  pl.BlockSpec((tb, n1, V7X_LANES), lambda j: (j, 0, 0))],
        out_specs=pl.BlockSpec((2, tb, n1, w), lambda j: (0, j, 0, 0)),
        compiler_params=_cparams(2 * 3 * n1 * tb * w * 2 + 8 * n1 * w * 4, 1), name="fft_stage1",
    )(p.reshape(rows // n2, n2, cols), f1, twr, twi)
    c2, s2 = _dft_cos_sin(n2)
    m2 = jnp.concatenate([jnp.concatenate([c2, s2], axis=1),
                          jnp.concatenate([-s2, c2], axis=1)], axis=0).astype(BF16)
    tc = BF16_SUBLANES
    xl = pl.pallas_call(
        _fft2_kernel, out_shape=jax.ShapeDtypeStruct((2, n2, n1, w), BF16),
        grid=(n1 // tc,),
        in_specs=[pl.BlockSpec((2 * n2, 2 * n2), lambda j: (0, 0)),
                  pl.BlockSpec((2, n2, tc, w), lambda j: (0, 0, j, 0))],
        out_specs=pl.BlockSpec((2, n2, tc, w), lambda j: (0, 0, j, 0)),
        scratch_shapes=[pltpu.VMEM((tc, n2, w), BF16), pltpu.VMEM((tc, n2, w), BF16)],
        compiler_params=_cparams(5 * 2 * n2 * tc * w * 2 + 8 * n2 * w * 4, 1), name="fft_stage2",
    )(m2, a).reshape(2, n_lat, w)
    cc, sc = _dft_cos_sin(n_ctx)
    mc = jnp.concatenate([cc, -sc], axis=0).astype(BF16)
    xc = pl.pallas_call(
        _dft_dense_kernel, out_shape=jax.ShapeDtypeStruct((2 * n_ctx, w), BF16),
        grid=(1,),
        in_specs=[pl.BlockSpec((2 * n_ctx, n_ctx), lambda j: (0, 0)),
                  pl.BlockSpec((n_ctx, w), lambda j: (n_lat // n_ctx, 0))],
        out_specs=pl.BlockSpec((2 * n_ctx, w), lambda j: (0, 0)),
        compiler_params=_cparams(16 * n_ctx * w, 1), name="dft_context",
    )(mc, p).reshape(2, n_ctx, w)
    xall = jnp.concatenate([xl, xc], axis=1)
    cg, sg = _dft_cos_sin(HEAD_DIM)
    eye = jnp.eye(w // HEAD_DIM, dtype=F32)
    cb = jnp.kron(eye, cg).astype(BF16)
    sb = jnp.kron(eye, sg).astype(BF16)
    full = lambda r, c: pl.BlockSpec((r, c), lambda i: (0, 0))
    return pl.pallas_call(
        _fft3_kernel, out_shape=jax.ShapeDtypeStruct((rows, w), BF16),
        grid=(rows // ROW_TILE,),
        in_specs=[pl.BlockSpec((2, ROW_TILE, w), lambda i: (0, i, 0)),
                  full(w, w), full(w, w), full(w, w), full(1, w)],
        out_specs=pl.BlockSpec((ROW_TILE, w), lambda i: (i, 0)),
        compiler_params=_cparams(2 * (3 * w * w * 2 + 3 * ROW_TILE * w * 2) + ROW_TILE * w * 12, 1),
        name="fft_channel_linear",
    )(xall, cb, sb, w_fnet.astype(BF16), b_fnet.reshape(1, w))


def _na_tables(rows):
    kr = min(NA_KR, rows)
    nb = min(kr + NA_QROWS - 1, rows)
    nqb = rows // NA_QROWS
    r0 = np.arange(nqb) * NA_QROWS
    band0 = np.minimum(np.clip(r0 - kr // 2, 0, rows - kr), rows - nb)
    band_rows = band0[:, None] + np.arange(nb)[None, :]
    q_row = np.repeat(r0[:, None] + np.arange(NA_QROWS)[None, :], GRID_W, axis=1)
    q_col = np.tile(np.arange(GRID_W), NA_QROWS)
    k_row = np.repeat(band_rows, GRID_W, axis=1)
    k_col = np.tile(np.arange(GRID_W), nb)
    win_r = np.clip(q_row - kr // 2, 0, rows - kr)[:, :, None]
    win_c = np.clip(q_col - NA_KC // 2, 0, GRID_W - NA_KC)[:, None]
    kro = k_row[:, None, :]
    col_ok = (k_col[None, :] >= win_c) & (k_col[None, :] < win_c + NA_KC)
    mask = (kro >= win_r) & (kro < win_r + kr) & col_ok[None]
    d_row = np.clip(kro - q_row[:, :, None] + NA_KR - 1, 0, 2 * NA_KR - 2)
    d_col = np.clip(k_col[None, :] - q_col[:, None] + NA_KC - 1, 0, 2 * NA_KC - 2)
    pats, pid = [], np.zeros(nqb, np.int32)
    for n in range(nqb):
        for p, (m0, d0) in enumerate(pats):
            if np.array_equal(m0, mask[n]) and np.array_equal(d0, d_row[n]):
                pid[n] = p
                break
        else:
            pid[n] = len(pats)
            pats.append((mask[n], d_row[n]))
    pmask = np.stack([p[0] for p in pats])
    pdrow = np.stack([p[1] for p in pats])
    return (band0 * GRID_W).astype(np.int32), pid, pmask, pdrow, d_col, nb


def _na_bias(rpb, pmask, pdrow, d_col, nb, n_ctx):
    n_heads = rpb.shape[0]
    n_pat, qb, nk = pmask.shape
    tiles = rpb.astype(F32)[:, :, d_col[:GRID_W, :GRID_W]]
    drow_small = pdrow[:, ::GRID_W, ::GRID_W]
    b6 = tiles[:, drow_small]
    bias = b6.transpose(0, 1, 2, 4, 3, 5).reshape(n_heads, n_pat, qb, nk)
    return jnp.pad(jnp.where(pmask[None], bias, MASK_VALUE), ((0, 0), (0, 0), (0, 0), (0, n_ctx)))


def _na_kernel(pid_ref, st_ref, q_ref, k_ref, v_ref, cos_ref, sin_ref, bias_ref, o_ref, qs, ks,
               *, n_lat, n_ctx, nqb, qb, nk):
    scale = HEAD_DIM ** -0.5
    rt = 512 if n_lat % 512 == 0 else qb

    def rope_body(i, carry):
        r = pl.multiple_of(i * rt, rt)
        c = cos_ref[pl.ds(r, rt), :]
        s = sin_ref[pl.ds(r, rt), :]
        even = (lax.broadcasted_iota(I32, (rt, HEAD_DIM), 1) % 2) == 0

        def rot(x):
            xs = jnp.where(even, pltpu.roll(x, HEAD_DIM - 1, 1), pltpu.roll(x, 1, 1))
            return x * c + xs * s

        qs[pl.ds(r, rt), :] = (rot(q_ref[pl.ds(r, rt), :].astype(F32)) * scale).astype(BF16)
        ks[pl.ds(r, rt), :] = rot(k_ref[pl.ds(r, rt), :].astype(F32)).astype(BF16)
        return carry

    lax.fori_loop(0, n_lat // rt, rope_body, 0)
    kc = k_ref[n_lat:n_lat + n_ctx, :]
    vc = v_ref[n_lat:n_lat + n_ctx, :]

    def block_body(n, carry):
        r = pl.multiple_of(n * qb, qb)
        st = pl.multiple_of(st_ref[n], GRID_W)
        k_all = jnp.concatenate([ks[pl.ds(st, nk), :], kc], axis=0)
        v_all = jnp.concatenate([v_ref[pl.ds(st, nk), :], vc], axis=0)
        s = _dot_nt(qs[pl.ds(r, qb), :], k_all) + bias_ref[0, pid_ref[n]]
        p = jnp.exp(s - jnp.max(s, axis=-1, keepdims=True))
        o = _dot(p.astype(BF16), v_all) / jnp.sum(p, axis=-1, keepdims=True)
        o_ref[pl.ds(r, qb), :] = o.astype(o_ref.dtype)
        return carry

    lax.fori_loop(0, nqb, block_body, 0, unroll=NA_UNROLL)
    s = _dot_nt(q_ref[n_lat:n_lat + n_ctx, :], kc) * scale
    p = jnp.exp(s - jnp.max(s, axis=-1, keepdims=True))
    o = _dot(p.astype(BF16), vc) / jnp.sum(p, axis=-1, keepdims=True)
    o_ref[n_lat:n_lat + n_ctx, :] = o.astype(o_ref.dtype)


def _na(p, col0, n_heads, rpb, cos_rep, sin_sgn, n_lat):
    rows = p.shape[0]
    n_ctx = rows - n_lat
    grid_rows = n_lat // GRID_W
    st, pid, pmask, pdrow, d_col, nb = _na_tables(grid_rows)
    n_pat, qb, nk = pmask.shape
    nqb = grid_rows // NA_QROWS
    bias = _na_bias(rpb, pmask, pdrow, d_col, nb, n_ctx)
    head = lambda off: pl.BlockSpec((rows, HEAD_DIM), lambda h, *_: (0, col0 + off * n_heads + h))
    kern = functools.partial(_na_kernel, n_lat=n_lat, n_ctx=n_ctx, nqb=nqb, qb=qb, nk=nk)
    vm = 2 * (4 * rows * HEAD_DIM * 2 + n_pat * qb * nk * 4) + 2 * n_lat * HEAD_DIM * 4 \
        + 2 * n_lat * HEAD_DIM * 2 + 8 * qb * (nk + n_ctx) * 4
    return pl.pallas_call(
        kern, out_shape=jax.ShapeDtypeStruct((rows, n_heads * HEAD_DIM), BF16),
        grid_spec=pltpu.PrefetchScalarGridSpec(
            num_scalar_prefetch=2, grid=(n_heads,),
            in_specs=[head(0), head(1), head(2),
                      pl.BlockSpec(memory_space=pltpu.VMEM), pl.BlockSpec(memory_space=pltpu.VMEM),
                      pl.BlockSpec((1, n_pat, qb, nk + n_ctx), lambda h, *_: (h, 0, 0, 0))],
            out_specs=pl.BlockSpec((rows, HEAD_DIM), lambda h, *_: (0, h)),
            scratch_shapes=[pltpu.VMEM((n_lat, HEAD_DIM), BF16), pltpu.VMEM((n_lat, HEAD_DIM), BF16)]),
        compiler_params=_cparams(vm, 1), name="neighborhood_attention",
    )(jnp.asarray(pid), jnp.asarray(st), p, p, p, cos_rep, sin_sgn, bias)


def _hgrn_gates(z, lbv):
    sg = jax.nn.sigmoid(z)
    f = lbv + (1.0 - lbv) * sg
    lf = jnp.log(jnp.maximum(f, GATE_FLOOR))
    kk = (1.0 - lbv) * (1.0 - sg)
    return lf, kk


def _hgrn_kernel(hq_ref, hi_ref, hf_ref, hb_ref, hg_ref, lb_ref, gn_ref, o_ref, acc, stf, stb,
                 *, n_lat, n_ctx):
    c_rows, sub = HGRN_CHUNK, HGRN_SUB
    n_sub = c_rows // sub
    dk = HEAD_DIM
    rows = n_lat + n_ctx
    lbf = lb_ref[0, 0:1, :]
    lbb = lb_ref[0, 1:2, :]

    def decay_body(i, mn):
        r = pl.multiple_of(i * c_rows, c_rows)
        lf_f, _ = _hgrn_gates(hf_ref[pl.ds(r, c_rows), :].astype(F32), lbf)
        lf_b, _ = _hgrn_gates(hb_ref[pl.ds(r, c_rows), :].astype(F32), lbb)
        for j in range(n_sub):
            mn = jnp.minimum(mn, jnp.sum(lf_f[j * sub:(j + 1) * sub], axis=0, keepdims=True))
            mn = jnp.minimum(mn, jnp.sum(lf_b[j * sub:(j + 1) * sub], axis=0, keepdims=True))
        return mn

    mn = lax.fori_loop(0, rows // c_rows, decay_body, jnp.zeros((1, dk), F32))
    safe = jnp.min(mn) >= -HGRN_SAFE_DECAY

    acc[...] = jnp.zeros(acc.shape, F32)
    stf[...] = jnp.zeros(stf.shape, F32)
    stb[...] = jnp.zeros(stb.shape, F32)

    row_i = lax.broadcasted_iota(I32, (c_rows, dk), 0)
    t_i = lax.broadcasted_iota(I32, (c_rows, c_rows), 0)
    s_i = lax.broadcasted_iota(I32, (c_rows, c_rows), 1)

    def chunk(r0, z_ref, lbv, st_ref, rev):
        lf, kk = _hgrn_gates(z_ref[pl.ds(r0, c_rows), :].astype(F32), lbv)
        q = _silu(hq_ref[pl.ds(r0, c_rows), :].astype(F32))
        v = hi_ref[pl.ds(r0, c_rows), :]
        b = lf
        step = 1
        while step < c_rows:
            if rev:
                b = b + jnp.where(row_i < c_rows - step, pltpu.roll(b, c_rows - step, 0), 0.0)
            else:
                b = b + jnp.where(row_i >= step, pltpu.roll(b, step, 0), 0.0)
            step *= 2
        zero = jnp.zeros((1, dk), F32)
        refs = []
        for i in range(n_sub):
            if rev:
                refs.append(b[(i + 1) * sub:(i + 1) * sub + 1] if i < n_sub - 1 else zero)
            else:
                refs.append(b[i * sub - 1:i * sub] if i > 0 else zero)
        qd = jnp.concatenate([(q[i * sub:(i + 1) * sub] * jnp.exp(b[i * sub:(i + 1) * sub] - refs[i])).astype(BF16)
                              for i in range(n_sub)], axis=0)
        ks = []
        for i in range(n_sub):
            lo, hi = (i * sub, c_rows) if rev else (0, (i + 1) * sub)
            part = (kk[lo:hi] * jnp.exp(refs[i] - b[lo:hi])).astype(BF16)
            if hi - lo < c_rows:
                pad = jnp.zeros((c_rows - (hi - lo), dk), BF16)
                part = jnp.concatenate([pad, part] if rev else [part, pad], axis=0)
            ks.append(part)
        scores = _dot_nt(qd, jnp.concatenate(ks, axis=0))
        own = jnp.concatenate([scores[i * sub:(i + 1) * sub, i * c_rows:(i + 1) * c_rows]
                               for i in range(n_sub)], axis=0)
        pm = jnp.where((s_i >= t_i) if rev else (s_i <= t_i), own, 0.0).astype(BF16)
        st = st_ref[...]
        o = _dot(pm, v) + _dot_nt((q * jnp.exp(b)).astype(BF16), st.astype(BF16))
        blast = b[0:1] if rev else b[c_rows - 1:c_rows]
        ke = (kk * jnp.exp(blast - b)).astype(BF16)
        st_ref[...] = st * jnp.exp(blast) + _dot_tn(v, ke)
        return o

    def fast_path():
        def run(base, n):
            def body(c, carry):
                rf = pl.multiple_of(base + c * c_rows, c_rows)
                rb = pl.multiple_of(base + (n - 1 - c) * c_rows, c_rows)
                o_f = chunk(rf, hf_ref, lbf, stf, False)
                acc[pl.ds(rf, c_rows), :] = acc[pl.ds(rf, c_rows), :] + o_f
                o_b = chunk(rb, hb_ref, lbb, stb, True)
                acc[pl.ds(rb, c_rows), :] = acc[pl.ds(rb, c_rows), :] + o_b
                return carry
            lax.fori_loop(0, n, body, 0, unroll=HGRN_UNROLL)
        run(n_lat, n_ctx // c_rows)
        run(0, n_lat // c_rows)

    def slow_path():
        pack = BF16_SUBLANES
        sub_i = lax.broadcasted_iota(I32, (pack, dk), 0)
        row8 = lax.broadcasted_iota(I32, (8, dk), 0)

        def load_row(ref, t):
            r = pl.multiple_of((t // pack) * pack, pack)
            blk = ref[pl.ds(r, pack), :].astype(F32)
            return jnp.sum(jnp.where(sub_i == t - r, blk, 0.0), axis=0, keepdims=True)

        def run(base, n, z_ref, lbv, st_ref, rev):
            def body(i, carry):
                t = base + ((n - 1 - i) if rev else i)
                lf, kk = _hgrn_gates(load_row(z_ref, t), lbv)
                q = _silu(load_row(hq_ref, t))
                v = load_row(hi_ref, t)
                v8 = jnp.where(row8 == 0, v, 0.0).astype(BF16)
                k8 = jnp.where(row8 == 0, kk, 0.0).astype(BF16)
                st = st_ref[...] * jnp.exp(lf) + _dot_tn(v8, k8)
                st_ref[...] = st
                q8 = jnp.broadcast_to(q, (8, dk)).astype(BF16)
                o = _dot_nt(q8, st.astype(BF16))
                acc[pl.ds(t, 1), :] = acc[pl.ds(t, 1), :] + o[0:1]
                return carry
            lax.fori_loop(0, n, body, 0)
        run(n_lat, n_ctx, hf_ref, lbf, stf, False)
        run(0, n_lat, hf_ref, lbf, stf, False)
        run(n_lat, n_ctx, hb_ref, lbb, stb, True)
        run(0, n_lat, hb_ref, lbb, stb, True)

    lax.cond(safe, fast_path, slow_path)

    rt = ROW_TILE

    def readout(i, carry):
        r = pl.multiple_of(i * rt, rt)
        y = _rms(acc[pl.ds(r, rt), :], gn_ref[...])
        o_ref[pl.ds(r, rt), :] = (y * _silu(hg_ref[pl.ds(r, rt), :].astype(F32))).astype(o_ref.dtype)
        return carry

    lax.fori_loop(0, rows // rt, readout, 0)


def _hgrn(p, col0, n_heads, lb, g_norm, n_lat):
    rows = p.shape[0]
    n_ctx = rows - n_lat
    head = lambda off: pl.BlockSpec((rows, HEAD_DIM), lambda h: (0, col0 + off * n_heads + h))
    lbh = lb.reshape(2, n_heads, HEAD_DIM).transpose(1, 0, 2)
    kern = functools.partial(_hgrn_kernel, n_lat=n_lat, n_ctx=n_ctx)
    n_keys = HGRN_CHUNK * HGRN_CHUNK // HGRN_SUB
    vm = 2 * 6 * rows * HEAD_DIM * 2 + rows * HEAD_DIM * 4 + 6 * HGRN_UNROLL * HGRN_CHUNK * n_keys * 4
    return pl.pallas_call(
        kern, out_shape=jax.ShapeDtypeStruct((rows, n_heads * HEAD_DIM), BF16),
        grid=(n_heads,),
        in_specs=[head(0), head(1), head(2), head(3), head(4),
                  pl.BlockSpec((1, 2, HEAD_DIM), lambda h: (h, 0, 0)),
                  pl.BlockSpec((1, HEAD_DIM), lambda h: (0, 0))],
        out_specs=pl.BlockSpec((rows, HEAD_DIM), lambda h: (0, h)),
        scratch_shapes=[pltpu.VMEM((rows, HEAD_DIM), F32), pltpu.VMEM((HEAD_DIM, HEAD_DIM), F32),
                        pltpu.VMEM((HEAD_DIM, HEAD_DIM), F32)],
        compiler_params=_cparams(vm, 1), name="hgrn2_bidir",
    )(p, p, p, p, p, lbh, g_norm.reshape(1, HEAD_DIM))


STEP_LOAD, STEP_COMPUTE, STEP_IDLE, STEP_ZERO = 0, 1, 2, 3


def _deinterleave_perm():
    g = V7X_MXU_DIM
    src = np.arange(g)
    dst = np.where(src % 2 == 0, src // 2, g // 2 + src // 2)
    perm = np.zeros((g, g), np.float32)
    perm[src, dst] = 1.0
    return jnp.asarray(perm, BF16)


def _expert_kernel(kind_ref, wexp_ref, wchunk_ref, wslot_ref, blk_ref, cexp_ref, cslot_ref, nv_ref,
                   tok_cur, tok_nxt, h_hbm, wgu_ref, wdn_ref, bgu_ref, bdn_ref, perm_ref, y_ref,
                   wgu_s, wdn_s, xbuf, sems):
    s = pl.program_id(0)
    kind = kind_ref[s]
    tm = xbuf.shape[1]

    @pl.when(kind == STEP_LOAD)
    def _():
        c = wchunk_ref[s]
        sl = wslot_ref[s]
        rg = wgu_ref.shape[0]
        rd = wdn_ref.shape[0]
        wgu_s[sl, pl.ds(pl.multiple_of(c * rg, rg), rg), :] = wgu_ref[...].astype(BF16)
        wdn_s[sl, pl.ds(pl.multiple_of(c * rd, rd), rd), :] = wdn_ref[...].astype(BF16)

    @pl.when(kind == STEP_ZERO)
    def _():
        y_ref[...] = jnp.zeros(y_ref.shape, y_ref.dtype)

    @pl.when(kind == STEP_COMPUTE)
    def _():
        b = blk_ref[s]
        n_valid = nv_ref[0]
        slot = b % 2
        cs = cslot_ref[s]

        def row_copy(tok_ref, r, dst_slot):
            return pltpu.make_async_copy(h_hbm.at[pl.ds(tok_ref[0, 0, r], 1), :],
                                         xbuf.at[dst_slot, pl.ds(r, 1), :], sems.at[dst_slot])

        def block_wait(dst_slot):
            pltpu.make_async_copy(h_hbm.at[pl.ds(0, tm), :], xbuf.at[dst_slot], sems.at[dst_slot]).wait()

        @pl.when(b == 0)
        def _():
            def body(r, carry):
                row_copy(tok_cur, r, 0).start()
                return carry
            lax.fori_loop(0, tm, body, 0, unroll=8)

        block_wait(slot)
        g, h = V7X_MXU_DIM, V7X_LANES
        half = tm // 2
        perm = perm_ref[...]
        d_in = xbuf.shape[2]
        k_chunks = 8
        kc = d_in // k_chunks
        per_chunk = tm // k_chunks
        for r in (0, half):
            if r == 0:
                gu = bgu_ref[0]
                for c in range(k_chunks):
                    for i in range(per_chunk):
                        row_copy(tok_nxt, c * per_chunk + i, 1 - slot).start()
                    gu = gu + _dot(xbuf[slot, 0:half, c * kc:(c + 1) * kc].astype(BF16),
                                   wgu_s[cs, c * kc:(c + 1) * kc, :])
            else:
                gu = _dot(xbuf[slot, r:r + half, :].astype(BF16), wgu_s[cs]) + bgu_ref[0]
            hi = gu.astype(BF16)
            lo = (gu - hi.astype(F32)).astype(BF16)
            parts = [_dot(hi[:, j * g:(j + 1) * g], perm) + _dot(lo[:, j * g:(j + 1) * g], perm)
                     for j in range(gu.shape[1] // g)]
            gate = jnp.minimum(jnp.concatenate([p[:, :h] for p in parts], axis=1), SWIGLU_LIMIT)
            up = jnp.clip(jnp.concatenate([p[:, h:] for p in parts], axis=1), -SWIGLU_LIMIT, SWIGLU_LIMIT)
            act = gate * jax.nn.sigmoid(SWIGLU_ALPHA * gate) * (up + 1.0)
            y_ref[r:r + half, :] = (_dot(act.astype(BF16), wdn_s[cs]) + bdn_ref[0]).astype(y_ref.dtype)

        @pl.when(b + 1 >= n_valid)
        def _():
            block_wait(1 - slot)


def _expert_program(counts, block_e, n_valid, n_blocks):
    n_e, n_ch = N_EXPERTS, MOE_WCHUNKS
    n_steps = n_blocks + n_ch * n_e
    nb = (counts + MOE_TILE - 1) // MOE_TILE
    active = nb > 0
    bstart = jnp.cumsum(nb) - nb
    e_idx = jnp.arange(n_e, dtype=I32)
    at_or_after = lax.cummin(jnp.where(active, e_idx, n_e)[::-1], axis=0)[::-1]
    next_active = jnp.concatenate([at_or_after[1:], jnp.full((1,), n_e, I32)])
    first_active = at_or_after[0]
    nl = jnp.where(active & (next_active < n_e), n_ch, 0)
    seg_len = jnp.where(active, nb + nl, 0)
    seg_start = n_ch + jnp.cumsum(seg_len) - seg_len
    slot = (jnp.cumsum(active.astype(I32)) - active.astype(I32)) % 2
    j = jnp.arange(n_ch, dtype=I32)
    lpos = jnp.where((nl > 0)[:, None], seg_start[:, None] + j[None, :] + jnp.minimum(j[None, :], nb[:, None]),
                     n_steps).reshape(-1)
    lpos = jnp.concatenate([j, lpos])
    lexp = jnp.concatenate([jnp.broadcast_to(first_active, (n_ch,)),
                            jnp.broadcast_to(jnp.minimum(next_active, n_e - 1)[:, None], (n_e, n_ch)).reshape(-1)])
    lslot = jnp.concatenate([jnp.zeros((n_ch,), I32),
                             jnp.broadcast_to((1 - slot)[:, None], (n_e, n_ch)).reshape(-1)])
    lchunk = jnp.concatenate([j, jnp.broadcast_to(j[None, :], (n_e, n_ch)).reshape(-1)])
    b = jnp.arange(n_blocks, dtype=I32)
    jb = b - bstart[block_e]
    cpos = jnp.where(b < n_valid[0], seg_start[block_e] + jb + jnp.minimum(jb + 1, nl[block_e]), n_steps)
    fill = lambda pos, val: jnp.zeros((n_steps + 1,), I32).at[pos].set(val)[:n_steps]
    kind = jnp.full((n_steps + 1,), STEP_IDLE, I32).at[lpos].set(STEP_LOAD).at[cpos].set(STEP_COMPUTE)[:n_steps]
    step = jnp.arange(n_steps, dtype=I32)
    last_load = lax.cummax(jnp.where(kind == STEP_LOAD, step, 0), axis=0)
    last_comp = lax.cummax(jnp.where(kind == STEP_COMPUTE, step, -1), axis=0)
    wexp = fill(lpos, lexp)[last_load]
    wchunk = fill(lpos, lchunk)[last_load]
    wslot = fill(lpos, lslot)[last_load]
    blk = jnp.where(last_comp >= 0, fill(cpos, b)[jnp.maximum(last_comp, 0)], 0)
    n_used = n_ch + jnp.sum(seg_len)
    spare = n_valid[0] + step - n_used
    kind = jnp.where((step >= n_used) & (spare < n_blocks), STEP_ZERO, kind)
    blk = jnp.where(step >= n_used, jnp.minimum(spare, n_blocks - 1), blk)
    cexp = block_e[blk]
    return kind, wexp, wchunk, wslot, blk, cexp, slot[cexp]


def _experts(h2, row_tok, counts, block_e, n_valid, w_gate_up, b_gate_up, w_down, b_down, layer):
    d = h2.shape[1]
    n_blocks = row_tok.shape[0] // MOE_TILE
    ff = w_down.shape[2]
    n_ch = MOE_WCHUNKS
    prog = _expert_program(counts, block_e, n_valid, n_blocks)
    n_steps = prog[0].shape[0]
    tok3 = row_tok.reshape(n_blocks, 1, MOE_TILE)
    tok_spec = lambda off: pl.BlockSpec(
        (1, 1, MOE_TILE), lambda s, kind, we, wc, ws, blk, *_: (jnp.minimum(blk[s] + off, n_blocks - 1), 0, 0),
        memory_space=pltpu.SMEM)
    wspec = lambda r, c: pl.BlockSpec((None, None, r, c), lambda s, kind, we, wc, *_: (layer, we[s], wc[s], 0))
    bspec = lambda c: pl.BlockSpec((1, 1, c), lambda s, kind, we, wc, ws, blk, ce, *_: (ce[s], 0, 0))
    vm = 2 * (d * 2 * ff * 2 + ff * d * 2) + 2 * (d * 2 * ff + ff * d) * 4 // n_ch \
        + 2 * MOE_TILE * d * 4 + 2 * MOE_TILE * d * 2 + MOE_TILE * (2 * ff * 10 + d * 6)
    return pl.pallas_call(
        _expert_kernel, out_shape=jax.ShapeDtypeStruct((n_blocks * MOE_TILE, d), BF16),
        grid_spec=pltpu.PrefetchScalarGridSpec(
            num_scalar_prefetch=8, grid=(n_steps,),
            in_specs=[tok_spec(0), tok_spec(1),
                      pl.BlockSpec(memory_space=pl.ANY),
                      wspec(d // n_ch, 2 * ff), wspec(ff // n_ch, d), bspec(2 * ff), bspec(d),
                      pl.BlockSpec((V7X_MXU_DIM, V7X_MXU_DIM), lambda s, *_: (0, 0))],
            out_specs=pl.BlockSpec((MOE_TILE, d), lambda s, kind, we, wc, ws, blk, *_: (blk[s], 0)),
            scratch_shapes=[pltpu.VMEM((2, d, 2 * ff), BF16), pltpu.VMEM((2, ff, d), BF16),
                            pltpu.VMEM((2, MOE_TILE, d), F32), pltpu.SemaphoreType.DMA((2,))]),
        compiler_params=_cparams(vm, 1), name="expert_ffn",
    )(*prog, n_valid, tok3, tok3, h2, w_gate_up, w_down,
      b_gate_up[layer].reshape(N_EXPERTS, 1, 2 * ff), b_down[layer].reshape(N_EXPERTS, 1, d), _deinterleave_perm())


def _route(top_idx, n_tok):
    flat_e = top_idx[:n_tok, :TOP_K].reshape(-1)
    n_pairs = flat_e.shape[0]
    onehot = (flat_e[:, None] == jnp.arange(N_EXPERTS, dtype=I32)[None, :]).astype(I32)
    csum = jnp.cumsum(onehot, axis=0)
    rank = jnp.sum(onehot * csum, axis=1) - 1
    counts = csum[-1]
    padded = (counts + MOE_TILE - 1) // MOE_TILE * MOE_TILE
    pend = jnp.cumsum(padded)
    dest = jnp.sum(onehot * (pend - padded)[None, :], axis=1) + rank
    n_blocks = -(-(n_pairs + N_EXPERTS * (MOE_TILE - 1)) // MOE_TILE)
    row_tok = jnp.zeros((n_blocks * MOE_TILE,), I32).at[dest].set(jnp.arange(n_pairs, dtype=I32) // TOP_K)
    n_valid = (pend[-1] // MOE_TILE).astype(I32).reshape(1)
    blk = jnp.minimum(jnp.arange(n_blocks, dtype=I32), n_valid[0] - 1) * MOE_TILE
    block_e = jnp.minimum(jnp.sum((pend[None, :] <= blk[:, None]).astype(I32), axis=1), N_EXPERTS - 1)
    return dest.reshape(n_tok, TOP_K), row_tok, counts, block_e, n_valid


def _moe(h2, top_idx, n_tok, w_gate_up, b_gate_up, w_down, b_down, layer):
    dest, row_tok, counts, block_e, n_valid = _route(top_idx, n_tok)
    y = _experts(h2, row_tok, counts, block_e, n_valid, w_gate_up, b_gate_up, w_down, b_down, layer)
    yg = y.at[dest.T.reshape(-1)].get(mode='promise_in_bounds')
    return yg.reshape(TOP_K, n_tok, -1)


def _rope_tables(n_tok):
    t = jnp.arange(n_tok, dtype=I32)
    row = (t // GRID_W).astype(F32)
    col = (t % GRID_W).astype(F32)
    pairs = HEAD_DIM // 4
    inv_freq = ROPE_THETA ** (-jnp.arange(pairs, dtype=F32) / pairs)
    ang = jnp.concatenate([row[:, None] * inv_freq, col[:, None] * inv_freq], axis=-1)
    cos_rep = jnp.repeat(jnp.cos(ang), 2, axis=-1)
    sin_sgn = jnp.stack([-jnp.sin(ang), jnp.sin(ang)], axis=-1).reshape(n_tok, HEAD_DIM)
    return cos_rep, sin_sgn


def kernel(x, c, ctx, c_ctx, w_ada, b_ada, norm_pre_mix, norm_post_mix, norm_pre_ffn, norm_post_ffn,
           w_in, w_fnet, b_fnet, na_rpb, hgrn_lb_logits, hgrn_out_norm, w_out,
           w_router, b_router, w_gate_up, b_gate_up, w_down, b_down):
    batch, n_lat, d = x.shape
    n_ctx = ctx.shape[1]
    assert batch == 1 and c.shape[0] == 1
    depth = w_ada.shape[0]
    fnet_w = w_fnet.shape[1]
    n_heads = na_rpb.shape[1]
    ff = w_down.shape[2]
    fnet_blocks = fnet_w // HEAD_DIM

    cvec = jnp.zeros((8, d), F32).at[0].set(c[0]).at[1].set(c_ctx)
    ada = _ada(cvec, w_ada, b_ada)
    mods = jnp.pad(ada[:, :2].reshape(depth, 2, 6, d), ((0, 0), (0, 0), (0, 2), (0, 0)))
    cos_rep, sin_sgn = _rope_tables(n_lat)
    p_lb = jax.nn.softmax(hgrn_lb_logits.astype(F32), axis=0)
    lower_bounds = jnp.cumsum(p_lb, axis=0) - p_lb[0]
    groups = 2 * ff // V7X_MXU_DIM

    xs = jnp.concatenate([x[0], ctx[0]], axis=0)
    h = _pre(xs, mods[0], norm_pre_mix[0], n_lat)
    for layer in range(depth):
        last = layer == depth - 1
        p = _matmul(h, w_in, layer, BF16)
        yf = _fourier_mix(p, fnet_w, w_fnet[layer], b_fnet[layer], n_lat)
        na = _na(p, fnet_blocks, n_heads, na_rpb[layer], cos_rep, sin_sgn, n_lat)
        hg = _hgrn(p, fnet_blocks + 3 * n_heads, n_heads, lower_bounds[layer], hgrn_out_norm[layer], n_lat)
        mix = jnp.concatenate([yf, na, hg], axis=1)
        y = _matmul(mix, w_out, layer, BF16)
        x1, h2, top_idx, top_w = _mid(xs, y, mods[layer], norm_post_mix[layer], norm_pre_ffn[layer],
                                      w_router[layer], b_router[layer], n_lat)
        yg = _moe(h2, top_idx, n_lat if last else n_lat + n_ctx, w_gate_up, b_gate_up, w_down, b_down, layer)
        if last:
            xs = _final(x1, yg, top_w, mods[layer], norm_post_ffn[layer], n_lat)
        else:
            xs, h = _end(x1, yg, top_w, mods[layer], norm_post_ffn[layer], mods[layer + 1],
                         norm_pre_mix[layer + 1], n_lat)
    return xs[:n_lat].reshape(batch, n_lat, d)
```

```python
import functools

import numpy as np
import jax
import jax.numpy as jnp
from jax import lax
from jax.experimental import pallas as pl
from jax.experimental.pallas import tpu as pltpu

F32 = jnp.float32
BF16 = jnp.bfloat16
I32 = jnp.int32

GRID_W = 64
HEAD_DIM = 128
NA_KR = 8
NA_KC = 16
NA_QROWS = 2
ROPE_THETA = 10000.0
N_EXPERTS = 32
TOP_K = 4
SWIGLU_LIMIT = 7.0
SWIGLU_ALPHA = 1.702
RMS_EPS = 1e-6
MASK_VALUE = -1e30
GATE_FLOOR = 1e-30

V7X_VMEM_BYTES = 64 * 1024 * 1024
V7X_LANES = 128
V7X_MXU_DIM = 256
BF16_SUBLANES = 16

ROW_TILE = 256
COMBINE_TILE = 128
MOE_TILE = 256
MOE_WCHUNKS = 8
HGRN_CHUNK = 128
HGRN_SUB = 16
HGRN_SAFE_DECAY = 80.0
HGRN_UNROLL = 4
NA_UNROLL = 2
FFT_N1 = 64


def _cparams(vmem_bytes, n_grid):
    limit = int(min(max(vmem_bytes * 5 // 4 + (4 << 20), 32 << 20), V7X_VMEM_BYTES - (4 << 20)))
    return pltpu.CompilerParams(dimension_semantics=("arbitrary",) * n_grid, vmem_limit_bytes=limit)


def _dot(a, b):
    return jnp.dot(a, b, preferred_element_type=F32)


def _dot_nt(a, b):
    return lax.dot_general(a, b, (((1,), (1,)), ((), ())), preferred_element_type=F32)


def _dot_tn(a, b):
    return lax.dot_general(a, b, (((0,), (0,)), ((), ())), preferred_element_type=F32)


def _silu(x):
    return x * jax.nn.sigmoid(x)


def _rms(x, g):
    return x * lax.rsqrt(jnp.mean(x * x, axis=-1, keepdims=True) + RMS_EPS) * g


def _ada_kernel(c_ref, w_ref, b_ref, o_ref):
    a = _silu(c_ref[...])
    a_hi = a.astype(BF16)
    a_lo = (a - a_hi.astype(F32)).astype(BF16)
    w = w_ref[0].astype(BF16)
    o_ref[0] = _dot(a_hi, w) + _dot(a_lo, w) + b_ref[0]


def _ada(cvec, w_ada, b_ada):
    depth, d, n = w_ada.shape
    tn = 512
    return pl.pallas_call(
        _ada_kernel,
        out_shape=jax.ShapeDtypeStruct((depth, 8, n), F32),
        grid=(depth, n // tn),
        in_specs=[pl.BlockSpec((8, d), lambda l, j: (0, 0)),
                  pl.BlockSpec((1, d, tn), lambda l, j: (l, 0, j)),
                  pl.BlockSpec((1, 1, tn), lambda l, j: (l, 0, j))],
        out_specs=pl.BlockSpec((1, 8, tn), lambda l, j: (l, 0, j)),
        compiler_params=_cparams(2 * d * tn * 4 + d * tn * 2, 2),
        name="adaln",
    )(cvec, w_ada, b_ada.reshape(depth, 1, n))


def _pre_kernel(x_ref, mod_ref, g_ref, h_ref):
    y = _rms(x_ref[...], g_ref[...])
    h_ref[...] = (y * (1.0 + mod_ref[0, 1:2, :]) + mod_ref[0, 0:1, :]).astype(h_ref.dtype)


def _mid_kernel(x_ref, y_ref, mod_ref, gpost_ref, gpre_ref, wr_ref, br_ref,
                x1_ref, h2_ref, idx_ref, tw_ref):
    x1 = x_ref[...] + mod_ref[0, 2:3, :] * _rms(y_ref[...].astype(F32), gpost_ref[...])
    x1_ref[...] = x1
    h2 = _rms(x1, gpre_ref[...]) * (1.0 + mod_ref[0, 4:5, :]) + mod_ref[0, 3:4, :]
    bits = lax.bitcast_convert_type(h2.astype(BF16).astype(F32), jnp.uint32)
    half = bits.shape[1] // 2
    h2_ref[...] = (bits[:, :half] >> 16) | (bits[:, half:] & jnp.uint32(0xFFFF0000))
    logits = jnp.dot(h2, wr_ref[...], precision=lax.Precision.HIGHEST,
                     preferred_element_type=F32) + br_ref[...]
    lane = lax.broadcasted_iota(I32, logits.shape, 1)
    idx_acc = jnp.zeros(logits.shape, I32)
    top_acc = jnp.full(logits.shape, -jnp.inf, F32)
    work = logits
    for r in range(TOP_K):
        m = jnp.max(work, axis=-1, keepdims=True)
        sel = jnp.min(jnp.where(work == m, lane, V7X_LANES), axis=-1, keepdims=True)
        idx_acc = jnp.where(lane == r, sel, idx_acc)
        top_acc = jnp.where(lane == r, m, top_acc)
        work = jnp.where(lane == sel, -jnp.inf, work)
    e = jnp.exp(top_acc - jnp.max(top_acc, axis=-1, keepdims=True))
    idx_ref[...] = idx_acc
    tw_ref[...] = e / jnp.sum(e, axis=-1, keepdims=True)


def _combine(yg_ref, tw_ref):
    tw = tw_ref[...]
    fx = tw[:, 0:1] * yg_ref[0].astype(F32)
    for k in range(1, TOP_K):
        fx = fx + tw[:, k:k + 1] * yg_ref[k].astype(F32)
    return fx


def _end_kernel(x1_ref, yg_ref, tw_ref, mod_ref, gpost_ref, modn_ref, gpren_ref, x2_ref, h_ref):
    x2 = x1_ref[...] + mod_ref[0, 5:6, :] * _rms(_combine(yg_ref, tw_ref), gpost_ref[...])
    x2_ref[...] = x2
    y = _rms(x2, gpren_ref[...])
    h_ref[...] = (y * (1.0 + modn_ref[0, 1:2, :]) + modn_ref[0, 0:1, :]).astype(h_ref.dtype)


def _final_kernel(x1_ref, yg_ref, tw_ref, mod_ref, gpost_ref, x2_ref):
    x2_ref[...] = x1_ref[...] + mod_ref[0, 5:6, :] * _rms(_combine(yg_ref, tw_ref), gpost_ref[...])


def _row_specs(tile, d, n_lat):
    row = lambda w: pl.BlockSpec((tile, w), lambda i: (i, 0))
    mod = pl.BlockSpec((1, 8, d), lambda i: (jnp.minimum(i // (n_lat // tile), 1), 0, 0))
    vec = lambda w: pl.BlockSpec((1, w), lambda i: (0, 0))
    return row, mod, vec


def _pre(x, mods, g, n_lat):
    rows, d = x.shape
    row, mod, vec = _row_specs(ROW_TILE, d, n_lat)
    return pl.pallas_call(
        _pre_kernel, out_shape=jax.ShapeDtypeStruct((rows, d), BF16),
        grid=(rows // ROW_TILE,), in_specs=[row(d), mod, vec(d)], out_specs=row(d),
        compiler_params=_cparams(ROW_TILE * d * 24, 1), name="pre_norm",
    )(x, mods, g.reshape(1, d))


def _mid(x, y, mods, gpost, gpre, w_router, b_router, n_lat):
    rows, d = x.shape
    row, mod, vec = _row_specs(ROW_TILE, d, n_lat)
    wr = jnp.zeros((d, V7X_LANES), F32).at[:, :N_EXPERTS].set(w_router)
    br = jnp.full((1, V7X_LANES), -jnp.inf, F32).at[0, :N_EXPERTS].set(b_router)
    return pl.pallas_call(
        _mid_kernel,
        out_shape=(jax.ShapeDtypeStruct((rows, d), F32), jax.ShapeDtypeStruct((rows, d // 2), jnp.uint32),
                   jax.ShapeDtypeStruct((rows, V7X_LANES), I32), jax.ShapeDtypeStruct((rows, V7X_LANES), F32)),
        grid=(rows // ROW_TILE,),
        in_specs=[row(d), row(d), mod, vec(d), vec(d),
                  pl.BlockSpec((d, V7X_LANES), lambda i: (0, 0)), vec(V7X_LANES)],
        out_specs=(row(d), row(d // 2), row(V7X_LANES), row(V7X_LANES)),
        compiler_params=_cparams(ROW_TILE * d * 48 + d * V7X_LANES * 8, 1), name="post_mix_router",
    )(x, y, mods, gpost.reshape(1, d), gpre.reshape(1, d), wr, br)


def _end(x1, yg, top_w, mods, gpost, mods_next, gpre_next, n_lat):
    rows, d = x1.shape
    t = COMBINE_TILE
    row, mod, vec = _row_specs(t, d, n_lat)
    return pl.pallas_call(
        _end_kernel,
        out_shape=(jax.ShapeDtypeStruct((rows, d), F32), jax.ShapeDtypeStruct((rows, d), BF16)),
        grid=(rows // t,),
        in_specs=[row(d), pl.BlockSpec((TOP_K, t, d), lambda i: (0, i, 0)), row(V7X_LANES),
                  mod, vec(d), mod, vec(d)],
        out_specs=(row(d), row(d)),
        compiler_params=_cparams(t * d * (2 * 4 * (TOP_K + 2) + 2 * 2 + 16), 1), name="post_ffn_pre_norm",
    )(x1, yg, top_w, mods, gpost.reshape(1, d), mods_next, gpre_next.reshape(1, d))


def _final(x1, yg, top_w, mods, gpost, n_lat):
    d = x1.shape[1]
    t = COMBINE_TILE
    row, mod, vec = _row_specs(t, d, n_lat)
    return pl.pallas_call(
        _final_kernel, out_shape=jax.ShapeDtypeStruct((n_lat, d), F32),
        grid=(n_lat // t,),
        in_specs=[row(d), pl.BlockSpec((TOP_K, t, d), lambda i: (0, i, 0)), row(V7X_LANES), mod, vec(d)],
        out_specs=row(d),
        compiler_params=_cparams(t * d * (2 * 4 * (TOP_K + 2) + 16), 1), name="post_ffn",
    )(x1, yg, top_w, mods, gpost.reshape(1, d))


def _mm_kernel(a_ref, b_ref, o_ref, bscr):
    @pl.when(pl.program_id(1) == 0)
    def _():
        bscr[...] = b_ref[0].astype(BF16)

    o_ref[...] = _dot(a_ref[...], bscr[...]).astype(o_ref.dtype)


def _pick_tile(n, prefs):
    for t in prefs:
        if n % t == 0:
            return t
    return n


def _matmul(a, w3, layer, out_dtype):
    m, k = a.shape
    n = w3.shape[2]
    tm = _pick_tile(m, (384, 256, 128))
    tn = _pick_tile(n, (1024, 512, 256, 128))
    vm = 2 * (tm * k * 2 + k * tn * 4 + tm * tn * 4) + k * tn * 2 + tm * tn * 4
    return pl.pallas_call(
        _mm_kernel, out_shape=jax.ShapeDtypeStruct((m, n), out_dtype),
        grid=(n // tn, m // tm),
        in_specs=[pl.BlockSpec((tm, k), lambda j, i: (i, 0)),
                  pl.BlockSpec((1, k, tn), lambda j, i: (layer, 0, j))],
        out_specs=pl.BlockSpec((tm, tn), lambda j, i: (i, j)),
        scratch_shapes=[pltpu.VMEM((k, tn), BF16)],
        compiler_params=_cparams(vm, 2), name="matmul",
    )(a, w3)


def _dft_cos_sin(n):
    i = jnp.arange(n, dtype=I32)
    ang = ((i[:, None] * i[None, :]) % n).astype(F32) * (2.0 * np.pi / n)
    s = 1.0 / np.sqrt(n)
    return jnp.cos(ang) * s, jnp.sin(ang) * s


def _fft1_kernel(x_ref, f_ref, twr_ref, twi_ref, o_ref):
    n1 = f_ref.shape[1]
    tb, _, w = o_ref.shape[1:]
    xt = pltpu.einshape("abw->baw", x_ref[...])
    for j in range(tb):
        res = _dot(f_ref[...], xt[j])
        ar = res[:n1]
        ai = res[n1:]
        wr = jnp.tile(twr_ref[j], (1, w // V7X_LANES))
        wi = jnp.tile(twi_ref[j], (1, w // V7X_LANES))
        o_ref[0, j] = (ar * wr - ai * wi).astype(o_ref.dtype)
        o_ref[1, j] = (ar * wi + ai * wr).astype(o_ref.dtype)


def _fft2_kernel(m_ref, a_ref, o_ref, sr, si):
    n2, tc = a_ref.shape[1:3]
    ar = pltpu.einshape("bcw->cbw", a_ref[0])
    ai = pltpu.einshape("bcw->cbw", a_ref[1])
    for c in range(tc):
        res = _dot(m_ref[...], jnp.concatenate([ar[c], ai[c]], axis=0))
        sr[c] = res[:n2].astype(sr.dtype)
        si[c] = res[n2:].astype(si.dtype)
    o_ref[0] = pltpu.einshape("cdw->dcw", sr[...])
    o_ref[1] = pltpu.einshape("cdw->dcw", si[...])


def _dft_dense_kernel(m_ref, a_ref, o_ref):
    o_ref[...] = _dot(m_ref[...], a_ref[...]).astype(o_ref.dtype)


def _fft3_kernel(x_ref, cb_ref, sb_ref, wf_ref, bf_ref, o_ref):
    z = _dot(x_ref[0], cb_ref[...]) + _dot(x_ref[1], sb_ref[...])
    o_ref[...] = (_dot(z.astype(BF16), wf_ref[...]) + bf_ref[...]).astype(o_ref.dtype)


def _fourier_mix(p, w, w_fnet, b_fnet, n_lat):
    rows, cols = p.shape
    n_ctx = rows - n_lat
    n1 = FFT_N1
    n2 = n_lat // n1
    tb = BF16_SUBLANES
    c1, s1 = _dft_cos_sin(n1)
    f1 = jnp.concatenate([c1, -s1], axis=0).astype(BF16)
    bi = jnp.arange(n2, dtype=I32)[:, None]
    ci = jnp.arange(n1, dtype=I32)[None, :]
    tang = ((bi * ci) % n_lat).astype(F32) * (2.0 * np.pi / n_lat)
    twr = jnp.broadcast_to(jnp.cos(tang)[:, :, None], (n2, n1, V7X_LANES))
    twi = jnp.broadcast_to(-jnp.sin(tang)[:, :, None], (n2, n1, V7X_LANES))
    a = pl.pallas_call(
        _fft1_kernel, out_shape=jax.ShapeDtypeStruct((2, n2, n1, w), BF16),
        grid=(n2 // tb,),
        in_specs=[pl.BlockSpec((n1, tb, w), lambda j: (0, j, 0)),
                  pl.BlockSpec((2 * n1, n1), lambda j: (0, 0)),
                  pl.BlockSpec((tb, n1, V7X_LANES), lambda j: (j, 0, 0)),
                  pl.BlockSpec((tb, n1, V7X_LANES), lambda j: (j, 0, 0))],
        out_specs=pl.BlockSpec((2, tb, n1, w), lambda j: (0, j, 0, 0)),
        compiler_params=_cparams(2 * 3 * n1 * tb * w * 2 + 8 * n1 * w * 4, 1), name="fft_stage1",
    )(p.reshape(rows // n2, n2, cols), f1, twr, twi)
    c2, s2 = _dft_cos_sin(n2)
    m2 = jnp.concatenate([jnp.concatenate([c2, s2], axis=1),
                          jnp.concatenate([-s2, c2], axis=1)], axis=0).astype(BF16)
    tc = BF16_SUBLANES
    xl = pl.pallas_call(
        _fft2_kernel, out_shape=jax.ShapeDtypeStruct((2, n2, n1, w), BF16),
        grid=(n1 // tc,),
        in_specs=[pl.BlockSpec((2 * n2, 2 * n2), lambda j: (0, 0)),
                  pl.BlockSpec((2, n2, tc, w), lambda j: (0, 0, j, 0))],
        out_specs=pl.BlockSpec((2, n2, tc, w), lambda j: (0, 0, j, 0)),
        scratch_shapes=[pltpu.VMEM((tc, n2, w), BF16), pltpu.VMEM((tc, n2, w), BF16)],
        compiler_params=_cparams(5 * 2 * n2 * tc * w * 2 + 8 * n2 * w * 4, 1), name="fft_stage2",
    )(m2, a).reshape(2, n_lat, w)
    cc, sc = _dft_cos_sin(n_ctx)
    mc = jnp.concatenate([cc, -sc], axis=0).astype(BF16)
    xc = pl.pallas_call(
        _dft_dense_kernel, out_shape=jax.ShapeDtypeStruct((2 * n_ctx, w), BF16),
        grid=(1,),
        in_specs=[pl.BlockSpec((2 * n_ctx, n_ctx), lambda j: (0, 0)),
                  pl.BlockSpec((n_ctx, w), lambda j: (n_lat // n_ctx, 0))],
        out_specs=pl.BlockSpec((2 * n_ctx, w), lambda j: (0, 0)),
        compiler_params=_cparams(16 * n_ctx * w, 1), name="dft_context",
    )(mc, p).reshape(2, n_ctx, w)
    xall = jnp.concatenate([xl, xc], axis=1)
    cg, sg = _dft_cos_sin(HEAD_DIM)
    eye = jnp.eye(w // HEAD_DIM, dtype=F32)
    cb = jnp.kron(eye, cg).astype(BF16)
    sb = jnp.kron(eye, sg).astype(BF16)
    full = lambda r, c: pl.BlockSpec((r, c), lambda i: (0, 0))
    return pl.pallas_call(
        _fft3_kernel, out_shape=jax.ShapeDtypeStruct((rows, w), BF16),
        grid=(rows // ROW_TILE,),
        in_specs=[pl.BlockSpec((2, ROW_TILE, w), lambda i: (0, i, 0)),
                  full(w, w), full(w, w), full(w, w), full(1, w)],
        out_specs=pl.BlockSpec((ROW_TILE, w), lambda i: (i, 0)),
        compiler_params=_cparams(2 * (3 * w * w * 2 + 3 * ROW_TILE * w * 2) + ROW_TILE * w * 12, 1),
        name="fft_channel_linear",
    )(xall, cb, sb, w_fnet.astype(BF16), b_fnet.reshape(1, w))


def _na_tables(rows):
    kr = min(NA_KR, rows)
    nb = min(kr + NA_QROWS - 1, rows)
    nqb = rows // NA_QROWS
    r0 = np.arange(nqb) * NA_QROWS
    band0 = np.minimum(np.clip(r0 - kr // 2, 0, rows - kr), rows - nb)
    band_rows = band0[:, None] + np.arange(nb)[None, :]
    q_row = np.repeat(r0[:, None] + np.arange(NA_QROWS)[None, :], GRID_W, axis=1)
    q_col = np.tile(np.arange(GRID_W), NA_QROWS)
    k_row = np.repeat(band_rows, GRID_W, axis=1)
    k_col = np.tile(np.arange(GRID_W), nb)
    win_r = np.clip(q_row - kr // 2, 0, rows - kr)[:, :, None]
    win_c = np.clip(q_col - NA_KC // 2, 0, GRID_W - NA_KC)[:, None]
    kro = k_row[:, None, :]
    col_ok = (k_col[None, :] >= win_c) & (k_col[None, :] < win_c + NA_KC)
    mask = (kro >= win_r) & (kro < win_r + kr) & col_ok[None]
    d_row = np.clip(kro - q_row[:, :, None] + NA_KR - 1, 0, 2 * NA_KR - 2)
    d_col = np.clip(k_col[None, :] - q_col[:, None] + NA_KC - 1, 0, 2 * NA_KC - 2)
    pats, pid = [], np.zeros(nqb, np.int32)
    for n in range(nqb):
        for p, (m0, d0) in enumerate(pats):
            if np.array_equal(m0, mask[n]) and np.array_equal(d0, d_row[n]):
                pid[n] = p
                break
        else:
            pid[n] = len(pats)
            pats.append((mask[n], d_row[n]))
    pmask = np.stack([p[0] for p in pats])
    pdrow = np.stack([p[1] for p in pats])
    return (band0 * GRID_W).astype(np.int32), pid, pmask, pdrow, d_col, nb


def _na_bias(rpb, pmask, pdrow, d_col, nb, n_ctx):
    n_heads = rpb.shape[0]
    n_pat, qb, nk = pmask.shape
    tiles = rpb.astype(F32)[:, :, d_col[:GRID_W, :GRID_W]]
    drow_small = pdrow[:, ::GRID_W, ::GRID_W]
    b6 = tiles[:, drow_small]
    bias = b6.transpose(0, 1, 2, 4, 3, 5).reshape(n_heads, n_pat, qb, nk)
    return jnp.pad(jnp.where(pmask[None], bias, MASK_VALUE), ((0, 0), (0, 0), (0, 0), (0, n_ctx)))


def _na_kernel(pid_ref, st_ref, q_ref, k_ref, v_ref, cos_ref, sin_ref, bias_ref, o_ref, qs, ks,
               *, n_lat, n_ctx, nqb, qb, nk):
    scale = HEAD_DIM ** -0.5
    rt = 512 if n_lat % 512 == 0 else qb

    def rope_body(i, carry):
        r = pl.multiple_of(i * rt, rt)
        c = cos_ref[pl.ds(r, rt), :]
        s = sin_ref[pl.ds(r, rt), :]
        even = (lax.broadcasted_iota(I32, (rt, HEAD_DIM), 1) % 2) == 0

        def rot(x):
            xs = jnp.where(even, pltpu.roll(x, HEAD_DIM - 1, 1), pltpu.roll(x, 1, 1))
            return x * c + xs * s

        qs[pl.ds(r, rt), :] = (rot(q_ref[pl.ds(r, rt), :].astype(F32)) * scale).astype(BF16)
        ks[pl.ds(r, rt), :] = rot(k_ref[pl.ds(r, rt), :].astype(F32)).astype(BF16)
        return carry

    lax.fori_loop(0, n_lat // rt, rope_body, 0)
    kc = k_ref[n_lat:n_lat + n_ctx, :]
    vc = v_ref[n_lat:n_lat + n_ctx, :]

    def block_body(n, carry):
        r = pl.multiple_of(n * qb, qb)
        st = pl.multiple_of(st_ref[n], GRID_W)
        k_all = jnp.concatenate([ks[pl.ds(st, nk), :], kc], axis=0)
        v_all = jnp.concatenate([v_ref[pl.ds(st, nk), :], vc], axis=0)
        s = _dot_nt(qs[pl.ds(r, qb), :], k_all) + bias_ref[0, pid_ref[n]]
        p = jnp.exp(s - jnp.max(s, axis=-1, keepdims=True))
        o = _dot(p.astype(BF16), v_all) / jnp.sum(p, axis=-1, keepdims=True)
        o_ref[pl.ds(r, qb), :] = o.astype(o_ref.dtype)
        return carry

    lax.fori_loop(0, nqb, block_body, 0, unroll=NA_UNROLL)
    s = _dot_nt(q_ref[n_lat:n_lat + n_ctx, :], kc) * scale
    p = jnp.exp(s - jnp.max(s, axis=-1, keepdims=True))
    o = _dot(p.astype(BF16), vc) / jnp.sum(p, axis=-1, keepdims=True)
    o_ref[n_lat:n_lat + n_ctx, :] = o.astype(o_ref.dtype)


def _na(p, col0, n_heads, rpb, cos_rep, sin_sgn, n_lat):
    rows = p.shape[0]
    n_ctx = rows - n_lat
    grid_rows = n_lat // GRID_W
    st, pid, pmask, pdrow, d_col, nb = _na_tables(grid_rows)
    n_pat, qb, nk = pmask.shape
    nqb = grid_rows // NA_QROWS
    bias = _na_bias(rpb, pmask, pdrow, d_col, nb, n_ctx)
    head = lambda off: pl.BlockSpec((rows, HEAD_DIM), lambda h, *_: (0, col0 + off * n_heads + h))
    kern = functools.partial(_na_kernel, n_lat=n_lat, n_ctx=n_ctx, nqb=nqb, qb=qb, nk=nk)
    vm = 2 * (4 * rows * HEAD_DIM * 2 + n_pat * qb * nk * 4) + 2 * n_lat * HEAD_DIM * 4 \
        + 2 * n_lat * HEAD_DIM * 2 + 8 * qb * (nk + n_ctx) * 4
    return pl.pallas_call(
        kern, out_shape=jax.ShapeDtypeStruct((rows, n_heads * HEAD_DIM), BF16),
        grid_spec=pltpu.PrefetchScalarGridSpec(
            num_scalar_prefetch=2, grid=(n_heads,),
            in_specs=[head(0), head(1), head(2),
                      pl.BlockSpec(memory_space=pltpu.VMEM), pl.BlockSpec(memory_space=pltpu.VMEM),
                      pl.BlockSpec((1, n_pat, qb, nk + n_ctx), lambda h, *_: (h, 0, 0, 0))],
            out_specs=pl.BlockSpec((rows, HEAD_DIM), lambda h, *_: (0, h)),
            scratch_shapes=[pltpu.VMEM((n_lat, HEAD_DIM), BF16), pltpu.VMEM((n_lat, HEAD_DIM), BF16)]),
        compiler_params=_cparams(vm, 1), name="neighborhood_attention",
    )(jnp.asarray(pid), jnp.asarray(st), p, p, p, cos_rep, sin_sgn, bias)


def _hgrn_gates(z, lbv):
    sg = jax.nn.sigmoid(z)
    f = lbv + (1.0 - lbv) * sg
    lf = jnp.log(jnp.maximum(f, GATE_FLOOR))
    kk = (1.0 - lbv) * (1.0 - sg)
    return lf, kk


def _hgrn_kernel(hq_ref, hi_ref, hf_ref, hb_ref, hg_ref, lb_ref, gn_ref, o_ref, acc, stf, stb,
                 *, n_lat, n_ctx):
    c_rows, sub = HGRN_CHUNK, HGRN_SUB
    n_sub = c_rows // sub
    dk = HEAD_DIM
    rows = n_lat + n_ctx
    lbf = lb_ref[0, 0:1, :]
    lbb = lb_ref[0, 1:2, :]

    def decay_body(i, mn):
        r = pl.multiple_of(i * c_rows, c_rows)
        lf_f, _ = _hgrn_gates(hf_ref[pl.ds(r, c_rows), :].astype(F32), lbf)
        lf_b, _ = _hgrn_gates(hb_ref[pl.ds(r, c_rows), :].astype(F32), lbb)
        for j in range(n_sub):
            mn = jnp.minimum(mn, jnp.sum(lf_f[j * sub:(j + 1) * sub], axis=0, keepdims=True))
            mn = jnp.minimum(mn, jnp.sum(lf_b[j * sub:(j + 1) * sub], axis=0, keepdims=True))
        return mn

    mn = lax.fori_loop(0, rows // c_rows, decay_body, jnp.zeros((1, dk), F32))
    safe = jnp.min(mn) >= -HGRN_SAFE_DECAY

    acc[...] = jnp.zeros(acc.shape, F32)
    stf[...] = jnp.zeros(stf.shape, F32)
    stb[...] = jnp.zeros(stb.shape, F32)

    row_i = lax.broadcasted_iota(I32, (c_rows, dk), 0)
    t_i = lax.broadcasted_iota(I32, (c_rows, c_rows), 0)
    s_i = lax.broadcasted_iota(I32, (c_rows, c_rows), 1)

    def chunk(r0, z_ref, lbv, st_ref, rev):
        lf, kk = _hgrn_gates(z_ref[pl.ds(r0, c_rows), :].astype(F32), lbv)
        q = _silu(hq_ref[pl.ds(r0, c_rows), :].astype(F32))
        v = hi_ref[pl.ds(r0, c_rows), :]
        b = lf
        step = 1
        while step < c_rows:
            if rev:
                b = b + jnp.where(row_i < c_rows - step, pltpu.roll(b, c_rows - step, 0), 0.0)
            else:
                b = b + jnp.where(row_i >= step, pltpu.roll(b, step, 0), 0.0)
            step *= 2
        zero = jnp.zeros((1, dk), F32)
        refs = []
        for i in range(n_sub):
            if rev:
                refs.append(b[(i + 1) * sub:(i + 1) * sub + 1] if i < n_sub - 1 else zero)
            else:
                refs.append(b[i * sub - 1:i * sub] if i > 0 else zero)
        qd = jnp.concatenate([(q[i * sub:(i + 1) * sub] * jnp.exp(b[i * sub:(i + 1) * sub] - refs[i])).astype(BF16)
                              for i in range(n_sub)], axis=0)
        ks = []
        for i in range(n_sub):
            lo, hi = (i * sub, c_rows) if rev else (0, (i + 1) * sub)
            part = (kk[lo:hi] * jnp.exp(refs[i] - b[lo:hi])).astype(BF16)
            if hi - lo < c_rows:
                pad = jnp.zeros((c_rows - (hi - lo), dk), BF16)
                part = jnp.concatenate([pad, part] if rev else [part, pad], axis=0)
            ks.append(part)
        scores = _dot_nt(qd, jnp.concatenate(ks, axis=0))
        own = jnp.concatenate([scores[i * sub:(i + 1) * sub, i * c_rows:(i + 1) * c_rows]
                               for i in range(n_sub)], axis=0)
        pm = jnp.where((s_i >= t_i) if rev else (s_i <= t_i), own, 0.0).astype(BF16)
        st = st_ref[...]
        o = _dot(pm, v) + _dot_nt((q * jnp.exp(b)).astype(BF16), st.astype(BF16))
        blast = b[0:1] if rev else b[c_rows - 1:c_rows]
        ke = (kk * jnp.exp(blast - b)).astype(BF16)
        st_ref[...] = st * jnp.exp(blast) + _dot_tn(v, ke)
        return o

    def fast_path():
        def run(base, n):
            def body(c, carry):
                rf = pl.multiple_of(base + c * c_rows, c_rows)
                rb = pl.multiple_of(base + (n - 1 - c) * c_rows, c_rows)
                o_f = chunk(rf, hf_ref, lbf, stf, False)
                acc[pl.ds(rf, c_rows), :] = acc[pl.ds(rf, c_rows), :] + o_f
                o_b = chunk(rb, hb_ref, lbb, stb, True)
                acc[pl.ds(rb, c_rows), :] = acc[pl.ds(rb, c_rows), :] + o_b
                return carry
            lax.fori_loop(0, n, body, 0, unroll=HGRN_UNROLL)
        run(n_lat, n_ctx // c_rows)
        run(0, n_lat // c_rows)

    def slow_path():
        pack = BF16_SUBLANES
        sub_i = lax.broadcasted_iota(I32, (pack, dk), 0)
        row8 = lax.broadcasted_iota(I32, (8, dk), 0)

        def load_row(ref, t):
            r = pl.multiple_of((t // pack) * pack, pack)
            blk = ref[pl.ds(r, pack), :].astype(F32)
            return jnp.sum(jnp.where(sub_i == t - r, blk, 0.0), axis=0, keepdims=True)

        def run(base, n, z_ref, lbv, st_ref, rev):
            def body(i, carry):
                t = base + ((n - 1 - i) if rev else i)
                lf, kk = _hgrn_gates(load_row(z_ref, t), lbv)
                q = _silu(load_row(hq_ref, t))
                v = load_row(hi_ref, t)
                v8 = jnp.where(row8 == 0, v, 0.0).astype(BF16)
                k8 = jnp.where(row8 == 0, kk, 0.0).astype(BF16)
                st = st_ref[...] * jnp.exp(lf) + _dot_tn(v8, k8)
                st_ref[...] = st
                q8 = jnp.broadcast_to(q, (8, dk)).astype(BF16)
                o = _dot_nt(q8, st.astype(BF16))
                acc[pl.ds(t, 1), :] = acc[pl.ds(t, 1), :] + o[0:1]
                return carry
            lax.fori_loop(0, n, body, 0)
        run(n_lat, n_ctx, hf_ref, lbf, stf, False)
        run(0, n_lat, hf_ref, lbf, stf, False)
        run(n_lat, n_ctx, hb_ref, lbb, stb, True)
        run(0, n_lat, hb_ref, lbb, stb, True)

    lax.cond(safe, fast_path, slow_path)

    rt = ROW_TILE

    def readout(i, carry):
        r = pl.multiple_of(i * rt, rt)
        y = _rms(acc[pl.ds(r, rt), :], gn_ref[...])
        o_ref[pl.ds(r, rt), :] = (y * _silu(hg_ref[pl.ds(r, rt), :].astype(F32))).astype(o_ref.dtype)
        return carry

    lax.fori_loop(0, rows // rt, readout, 0)


def _hgrn(p, col0, n_heads, lb, g_norm, n_lat):
    rows = p.shape[0]
    n_ctx = rows - n_lat
    head = lambda off: pl.BlockSpec((rows, HEAD_DIM), lambda h: (0, col0 + off * n_heads + h))
    lbh = lb.reshape(2, n_heads, HEAD_DIM).transpose(1, 0, 2)
    kern = functools.partial(_hgrn_kernel, n_lat=n_lat, n_ctx=n_ctx)
    n_keys = HGRN_CHUNK * HGRN_CHUNK // HGRN_SUB
    vm = 2 * 6 * rows * HEAD_DIM * 2 + rows * HEAD_DIM * 4 + 6 * HGRN_UNROLL * HGRN_CHUNK * n_keys * 4
    return pl.pallas_call(
        kern, out_shape=jax.ShapeDtypeStruct((rows, n_heads * HEAD_DIM), BF16),
        grid=(n_heads,),
        in_specs=[head(0), head(1), head(2), head(3), head(4),
                  pl.BlockSpec((1, 2, HEAD_DIM), lambda h: (h, 0, 0)),
                  pl.BlockSpec((1, HEAD_DIM), lambda h: (0, 0))],
        out_specs=pl.BlockSpec((rows, HEAD_DIM), lambda h: (0, h)),
        scratch_shapes=[pltpu.VMEM((rows, HEAD_DIM), F32), pltpu.VMEM((HEAD_DIM, HEAD_DIM), F32),
                        pltpu.VMEM((HEAD_DIM, HEAD_DIM), F32)],
        compiler_params=_cparams(vm, 1), name="hgrn2_bidir",
    )(p, p, p, p, p, lbh, g_norm.reshape(1, HEAD_DIM))


STEP_LOAD, STEP_COMPUTE, STEP_IDLE, STEP_ZERO = 0, 1, 2, 3


def _deinterleave_perm():
    g = V7X_MXU_DIM
    src = np.arange(g)
    dst = np.where(src % 2 == 0, src // 2, g // 2 + src // 2)
    perm = np.zeros((g, g), np.float32)
    perm[src, dst] = 1.0
    return jnp.asarray(perm, BF16)


def _expert_kernel(kind_ref, wexp_ref, wchunk_ref, wslot_ref, blk_ref, cexp_ref, cslot_ref, nv_ref,
                   tok_cur, tok_nxt, h_hbm, wgu_ref, wdn_ref, bgu_ref, bdn_ref, perm_ref, y_ref,
                   wgu_s, wdn_s, xbuf, sems):
    s = pl.program_id(0)
    kind = kind_ref[s]
    tm = xbuf.shape[1]

    @pl.when(kind == STEP_LOAD)
    def _():
        c = wchunk_ref[s]
        sl = wslot_ref[s]
        rg = wgu_ref.shape[0]
        rd = wdn_ref.shape[0]
        wgu_s[sl, pl.ds(pl.multiple_of(c * rg, rg), rg), :] = wgu_ref[...].astype(BF16)
        wdn_s[sl, pl.ds(pl.multiple_of(c * rd, rd), rd), :] = wdn_ref[...].astype(BF16)

    @pl.when(kind == STEP_ZERO)
    def _():
        y_ref[...] = jnp.zeros(y_ref.shape, y_ref.dtype)

    @pl.when(kind == STEP_COMPUTE)
    def _():
        b = blk_ref[s]
        n_valid = nv_ref[0]
        slot = b % 2
        cs = cslot_ref[s]

        def row_copy(tok_ref, r, dst_slot):
            return pltpu.make_async_copy(h_hbm.at[pl.ds(tok_ref[0, 0, r], 1), :],
                                         xbuf.at[dst_slot, pl.ds(r, 1), :], sems.at[dst_slot])

        def block_wait(dst_slot):
            pltpu.make_async_copy(h_hbm.at[pl.ds(0, tm), :], xbuf.at[dst_slot], sems.at[dst_slot]).wait()

        @pl.when(b == 0)
        def _():
            def body(r, carry):
                row_copy(tok_cur, r, 0).start()
                return carry
            lax.fori_loop(0, tm, body, 0, unroll=8)

        block_wait(slot)
        g, h = V7X_MXU_DIM, V7X_LANES
        half = tm // 2
        perm = perm_ref[...]
        dp = xbuf.shape[2]
        k_chunks = 8
        kc = dp // k_chunks
        per_chunk = tm // k_chunks

        def unpack(words):
            lo16 = lax.bitcast_convert_type(words << 16, F32).astype(BF16)
            hi16 = lax.bitcast_convert_type(words & jnp.uint32(0xFFFF0000), F32).astype(BF16)
            return lo16, hi16

        for r in (0, half):
            gu = bgu_ref[0]
            for c in range(k_chunks):
                if r == 0:
                    for i in range(per_chunk):
                        row = c * per_chunk + i
                        row_copy(tok_nxt, row, 1 - slot).start(priority=row % 2)
                x_lo, x_hi = unpack(xbuf[slot, r:r + half, c * kc:(c + 1) * kc])
                gu = gu + _dot(x_lo, wgu_s[cs, c * kc:(c + 1) * kc, :]) \
                    + _dot(x_hi, wgu_s[cs, dp + c * kc:dp + (c + 1) * kc, :])
            hi = gu.astype(BF16)
            lo = (gu - hi.astype(F32)).astype(BF16)
            parts = [_dot(hi[:, j * g:(j + 1) * g], perm) + _dot(lo[:, j * g:(j + 1) * g], perm)
                     for j in range(gu.shape[1] // g)]
            gate = jnp.minimum(jnp.concatenate([p[:, :h] for p in parts], axis=1), SWIGLU_LIMIT)
            up = jnp.clip(jnp.concatenate([p[:, h:] for p in parts], axis=1), -SWIGLU_LIMIT, SWIGLU_LIMIT)
            act = gate * jax.nn.sigmoid(SWIGLU_ALPHA * gate) * (up + 1.0)
            y_ref[r:r + half, :] = (_dot(act.astype(BF16), wdn_s[cs]) + bdn_ref[0]).astype(y_ref.dtype)

        @pl.when(b + 1 >= n_valid)
        def _():
            block_wait(1 - slot)


def _expert_program(counts, block_e, n_valid, n_blocks):
    n_e, n_ch = N_EXPERTS, MOE_WCHUNKS
    n_steps = n_blocks + n_ch * n_e
    nb = (counts + MOE_TILE - 1) // MOE_TILE
    active = nb > 0
    bstart = jnp.cumsum(nb) - nb
    e_idx = jnp.arange(n_e, dtype=I32)
    at_or_after = lax.cummin(jnp.where(active, e_idx, n_e)[::-1], axis=0)[::-1]
    next_active = jnp.concatenate([at_or_after[1:], jnp.full((1,), n_e, I32)])
    first_active = at_or_after[0]
    nl = jnp.where(active & (next_active < n_e), n_ch, 0)
    seg_len = jnp.where(active, nb + nl, 0)
    seg_start = n_ch + jnp.cumsum(seg_len) - seg_len
    slot = (jnp.cumsum(active.astype(I32)) - active.astype(I32)) % 2
    j = jnp.arange(n_ch, dtype=I32)
    lpos = jnp.where((nl > 0)[:, None], seg_start[:, None] + j[None, :] + jnp.minimum(j[None, :], nb[:, None]),
                     n_steps).reshape(-1)
    lpos = jnp.concatenate([j, lpos])
    lexp = jnp.concatenate([jnp.broadcast_to(first_active, (n_ch,)),
                            jnp.broadcast_to(jnp.minimum(next_active, n_e - 1)[:, None], (n_e, n_ch)).reshape(-1)])
    lslot = jnp.concatenate([jnp.zeros((n_ch,), I32),
                             jnp.broadcast_to((1 - slot)[:, None], (n_e, n_ch)).reshape(-1)])
    lchunk = jnp.concatenate([j, jnp.broadcast_to(j[None, :], (n_e, n_ch)).reshape(-1)])
    b = jnp.arange(n_blocks, dtype=I32)
    jb = b - bstart[block_e]
    cpos = jnp.where(b < n_valid[0], seg_start[block_e] + jb + jnp.minimum(jb + 1, nl[block_e]), n_steps)
    fill = lambda pos, val: jnp.zeros((n_steps + 1,), I32).at[pos].set(val)[:n_steps]
    kind = jnp.full((n_steps + 1,), STEP_IDLE, I32).at[lpos].set(STEP_LOAD).at[cpos].set(STEP_COMPUTE)[:n_steps]
    step = jnp.arange(n_steps, dtype=I32)
    last_load = lax.cummax(jnp.where(kind == STEP_LOAD, step, 0), axis=0)
    last_comp = lax.cummax(jnp.where(kind == STEP_COMPUTE, step, -1), axis=0)
    wexp = fill(lpos, lexp)[last_load]
    wchunk = fill(lpos, lchunk)[last_load]
    wslot = fill(lpos, lslot)[last_load]
    blk = jnp.where(last_comp >= 0, fill(cpos, b)[jnp.maximum(last_comp, 0)], 0)
    n_used = n_ch + jnp.sum(seg_len)
    spare = n_valid[0] + step - n_used
    kind = jnp.where((step >= n_used) & (spare < n_blocks), STEP_ZERO, kind)
    blk = jnp.where(step >= n_used, jnp.minimum(spare, n_blocks - 1), blk)
    cexp = block_e[blk]
    return kind, wexp, wchunk, wslot, blk, cexp, slot[cexp]


def _experts(h2, row_tok, counts, block_e, n_valid, w_gate_up, b_gate_up, w_down, b_down, layer):
    d = w_down.shape[3]
    n_blocks = row_tok.shape[0] // MOE_TILE
    ff = w_down.shape[2]
    n_ch = MOE_WCHUNKS
    prog = _expert_program(counts, block_e, n_valid, n_blocks)
    n_steps = prog[0].shape[0]
    tok3 = row_tok.reshape(n_blocks, 1, MOE_TILE)
    tok_spec = lambda off: pl.BlockSpec(
        (1, 1, MOE_TILE), lambda s, kind, we, wc, ws, blk, *_: (jnp.minimum(blk[s] + off, n_blocks - 1), 0, 0),
        memory_space=pltpu.SMEM)
    wspec = lambda r, c: pl.BlockSpec((None, None, r, c), lambda s, kind, we, wc, *_: (layer, we[s], wc[s], 0))
    bspec = lambda c: pl.BlockSpec((1, 1, c), lambda s, kind, we, wc, ws, blk, ce, *_: (ce[s], 0, 0))
    vm = 2 * (d * 2 * ff * 2 + ff * d * 2) + 2 * (d * 2 * ff + ff * d) * 4 // n_ch \
        + 2 * MOE_TILE * d * 2 + 2 * MOE_TILE * d * 2 + MOE_TILE * (2 * ff * 10 + d * 6)
    return pl.pallas_call(
        _expert_kernel, out_shape=jax.ShapeDtypeStruct((n_blocks * MOE_TILE, d), BF16),
        grid_spec=pltpu.PrefetchScalarGridSpec(
            num_scalar_prefetch=8, grid=(n_steps,),
            in_specs=[tok_spec(0), tok_spec(1),
                      pl.BlockSpec(memory_space=pl.ANY),
                      wspec(d // n_ch, 2 * ff), wspec(ff // n_ch, d), bspec(2 * ff), bspec(d),
                      pl.BlockSpec((V7X_MXU_DIM, V7X_MXU_DIM), lambda s, *_: (0, 0))],
            out_specs=pl.BlockSpec((MOE_TILE, d), lambda s, kind, we, wc, ws, blk, *_: (blk[s], 0)),
            scratch_shapes=[pltpu.VMEM((2, d, 2 * ff), BF16), pltpu.VMEM((2, ff, d), BF16),
                            pltpu.VMEM((2, MOE_TILE, d // 2), jnp.uint32), pltpu.SemaphoreType.DMA((2,))]),
        compiler_params=_cparams(vm, 1), name="expert_ffn",
    )(*prog, n_valid, tok3, tok3, h2, w_gate_up, w_down,
      b_gate_up[layer].reshape(N_EXPERTS, 1, 2 * ff), b_down[layer].reshape(N_EXPERTS, 1, d), _deinterleave_perm())


def _route(top_idx, n_tok):
    flat_e = top_idx[:n_tok, :TOP_K].reshape(-1)
    n_pairs = flat_e.shape[0]
    onehot = (flat_e[:, None] == jnp.arange(N_EXPERTS, dtype=I32)[None, :]).astype(I32)
    csum = jnp.cumsum(onehot, axis=0)
    rank = jnp.sum(onehot * csum, axis=1) - 1
    counts = csum[-1]
    padded = (counts + MOE_TILE - 1) // MOE_TILE * MOE_TILE
    pend = jnp.cumsum(padded)
    dest = jnp.sum(onehot * (pend - padded)[None, :], axis=1) + rank
    n_blocks = -(-(n_pairs + N_EXPERTS * (MOE_TILE - 1)) // MOE_TILE)
    row_tok = jnp.zeros((n_blocks * MOE_TILE,), I32).at[dest].set(jnp.arange(n_pairs, dtype=I32) // TOP_K)
    n_valid = (pend[-1] // MOE_TILE).astype(I32).reshape(1)
    blk = jnp.minimum(jnp.arange(n_blocks, dtype=I32), n_valid[0] - 1) * MOE_TILE
    block_e = jnp.minimum(jnp.sum((pend[None, :] <= blk[:, None]).astype(I32), axis=1), N_EXPERTS - 1)
    return dest.reshape(n_tok, TOP_K), row_tok, counts, block_e, n_valid


def _moe(h2, top_idx, n_tok, w_gate_up, b_gate_up, w_down, b_down, layer):
    dest, row_tok, counts, block_e, n_valid = _route(top_idx, n_tok)
    y = _experts(h2, row_tok, counts, block_e, n_valid, w_gate_up, b_gate_up, w_down, b_down, layer)
    yg = y.at[dest.T.reshape(-1)].get(mode='promise_in_bounds')
    return yg.reshape(TOP_K, n_tok, -1)


def _rope_tables(n_tok):
    t = jnp.arange(n_tok, dtype=I32)
    row = (t // GRID_W).astype(F32)
    col = (t % GRID_W).astype(F32)
    pairs = HEAD_DIM // 4
    inv_freq = ROPE_THETA ** (-jnp.arange(pairs, dtype=F32) / pairs)
    ang = jnp.concatenate([row[:, None] * inv_freq, col[:, None] * inv_freq], axis=-1)
    cos_rep = jnp.repeat(jnp.cos(ang), 2, axis=-1)
    sin_sgn = jnp.stack([-jnp.sin(ang), jnp.sin(ang)], axis=-1).reshape(n_tok, HEAD_DIM)
    return cos_rep, sin_sgn


def kernel(x, c, ctx, c_ctx, w_ada, b_ada, norm_pre_mix, norm_post_mix, norm_pre_ffn, norm_post_ffn,
           w_in, w_fnet, b_fnet, na_rpb, hgrn_lb_logits, hgrn_out_norm, w_out,
           w_router, b_router, w_gate_up, b_gate_up, w_down, b_down):
    batch, n_lat, d = x.shape
    n_ctx = ctx.shape[1]
    assert batch == 1 and c.shape[0] == 1
    depth = w_ada.shape[0]
    fnet_w = w_fnet.shape[1]
    n_heads = na_rpb.shape[1]
    ff = w_down.shape[2]
    fnet_blocks = fnet_w // HEAD_DIM

    cvec = jnp.zeros((8, d), F32).at[0].set(c[0]).at[1].set(c_ctx)
    ada = _ada(cvec, w_ada, b_ada)
    mods = jnp.pad(ada[:, :2].reshape(depth, 2, 6, d), ((0, 0), (0, 0), (0, 2), (0, 0)))
    cos_rep, sin_sgn = _rope_tables(n_lat)
    p_lb = jax.nn.softmax(hgrn_lb_logits.astype(F32), axis=0)
    lower_bounds = jnp.cumsum(p_lb, axis=0) - p_lb[0]
    groups = 2 * ff // V7X_MXU_DIM

    xs = jnp.concatenate([x[0], ctx[0]], axis=0)
    h = _pre(xs, mods[0], norm_pre_mix[0], n_lat)
    for layer in range(depth):
        last = layer == depth - 1
        p = _matmul(h, w_in, layer, BF16)
        yf = _fourier_mix(p, fnet_w, w_fnet[layer], b_fnet[layer], n_lat)
        na = _na(p, fnet_blocks, n_heads, na_rpb[layer], cos_rep, sin_sgn, n_lat)
        hg = _hgrn(p, fnet_blocks + 3 * n_heads, n_heads, lower_bounds[layer], hgrn_out_norm[layer], n_lat)
        mix = jnp.concatenate([yf, na, hg], axis=1)
        y = _matmul(mix, w_out, layer, BF16)
        x1, h2, top_idx, top_w = _mid(xs, y, mods[layer], norm_post_mix[layer], norm_pre_ffn[layer],
                                      w_router[layer], b_router[layer], n_lat)
        yg = _moe(h2, top_idx, n_lat if last else n_lat + n_ctx, w_gate_up, b_gate_up, w_down, b_down, layer)
        if last:
            xs = _final(x1, yg, top_w, mods[layer], norm_post_ffn[layer], n_lat)
        else:
            xs, h = _end(x1, yg, top_w, mods[layer], norm_post_ffn[layer], mods[layer + 1],
                         norm_pre_mix[layer + 1], n_lat)
    return xs[:n_lat].reshape(batch, n_lat, d)
```

```python
import functools

import numpy as np
import jax
import jax.numpy as jnp
from jax import lax
from jax.experimental import pallas as pl
from jax.experimental.pallas import tpu as pltpu

F32 = jnp.float32
BF16 = jnp.bfloat16
I32 = jnp.int32

GRID_W = 64
HEAD_DIM = 128
NA_KR = 8
NA_KC = 16
NA_QROWS = 2
ROPE_THETA = 10000.0
N_EXPERTS = 32
TOP_K = 4
SWIGLU_LIMIT = 7.0
SWIGLU_ALPHA = 1.702
RMS_EPS = 1e-6
MASK_VALUE = -1e30
GATE_FLOOR = 1e-30

V7X_VMEM_BYTES = 64 * 1024 * 1024
V7X_LANES = 128
V7X_MXU_DIM = 256
BF16_SUBLANES = 16

ROW_TILE = 256
COMBINE_TILE = 128
MOE_TILE = 256
MOE_WCHUNKS = 8
HGRN_CHUNK = 128
HGRN_SUB = 16
HGRN_SAFE_DECAY = 80.0
HGRN_UNROLL = 4
NA_UNROLL = 4
FFT_N1 = 64


def _cparams(vmem_bytes, n_grid):
    limit = int(min(max(vmem_bytes * 5 // 4 + (4 << 20), 32 << 20), V7X_VMEM_BYTES - (4 << 20)))
    return pltpu.CompilerParams(dimension_semantics=("arbitrary",) * n_grid, vmem_limit_bytes=limit)


def _dot(a, b):
    return jnp.dot(a, b, preferred_element_type=F32)


def _dot_nt(a, b):
    return lax.dot_general(a, b, (((1,), (1,)), ((), ())), preferred_element_type=F32)


def _dot_tn(a, b):
    return lax.dot_general(a, b, (((0,), (0,)), ((), ())), preferred_element_type=F32)


def _silu(x):
    return x * jax.nn.sigmoid(x)


def _rms(x, g):
    return x * lax.rsqrt(jnp.mean(x * x, axis=-1, keepdims=True) + RMS_EPS) * g


def _ada_kernel(c_ref, w_ref, b_ref, o_ref):
    a = _silu(c_ref[...])
    a_hi = a.astype(BF16)
    a_lo = (a - a_hi.astype(F32)).astype(BF16)
    w = w_ref[0].astype(BF16)
    o_ref[0] = _dot(a_hi, w) + _dot(a_lo, w) + b_ref[0]


def _ada(cvec, w_ada, b_ada):
    depth, d, n = w_ada.shape
    tn = 512
    return pl.pallas_call(
        _ada_kernel,
        out_shape=jax.ShapeDtypeStruct((depth, 8, n), F32),
        grid=(depth, n // tn),
        in_specs=[pl.BlockSpec((8, d), lambda l, j: (0, 0)),
                  pl.BlockSpec((1, d, tn), lambda l, j: (l, 0, j)),
                  pl.BlockSpec((1, 1, tn), lambda l, j: (l, 0, j))],
        out_specs=pl.BlockSpec((1, 8, tn), lambda l, j: (l, 0, j)),
        compiler_params=_cparams(2 * d * tn * 4 + d * tn * 2, 2),
        name="adaln",
    )(cvec, w_ada, b_ada.reshape(depth, 1, n))


def _pre_kernel(x_ref, mod_ref, g_ref, h_ref):
    y = _rms(x_ref[...], g_ref[...])
    h_ref[...] = (y * (1.0 + mod_ref[0, 1:2, :]) + mod_ref[0, 0:1, :]).astype(h_ref.dtype)


def _mid_kernel(x_ref, y_ref, mod_ref, gpost_ref, gpre_ref, wr_ref, br_ref,
                x1_ref, h2_ref, idx_ref, tw_ref):
    x1 = x_ref[...] + mod_ref[0, 2:3, :] * _rms(y_ref[...].astype(F32), gpost_ref[...])
    x1_ref[...] = x1
    h2 = _rms(x1, gpre_ref[...]) * (1.0 + mod_ref[0, 4:5, :]) + mod_ref[0, 3:4, :]
    bits = lax.bitcast_convert_type(h2.astype(BF16).astype(F32), jnp.uint32)
    half = bits.shape[1] // 2
    h2_ref[...] = (bits[:, :half] >> 16) | (bits[:, half:] & jnp.uint32(0xFFFF0000))
    logits = jnp.dot(h2, wr_ref[...], precision=lax.Precision.HIGHEST,
                     preferred_element_type=F32) + br_ref[...]
    lane = lax.broadcasted_iota(I32, logits.shape, 1)
    idx_acc = jnp.zeros(logits.shape, I32)
    top_acc = jnp.full(logits.shape, -jnp.inf, F32)
    work = logits
    for r in range(TOP_K):
        m = jnp.max(work, axis=-1, keepdims=True)
        sel = jnp.min(jnp.where(work == m, lane, V7X_LANES), axis=-1, keepdims=True)
        idx_acc = jnp.where(lane == r, sel, idx_acc)
        top_acc = jnp.where(lane == r, m, top_acc)
        work = jnp.where(lane == sel, -jnp.inf, work)
    e = jnp.exp(top_acc - jnp.max(top_acc, axis=-1, keepdims=True))
    idx_ref[...] = idx_acc
    tw_ref[...] = e / jnp.sum(e, axis=-1, keepdims=True)


def _combine(yg_ref, tw_ref):
    tw = tw_ref[...]
    fx = tw[:, 0:1] * yg_ref[0].astype(F32)
    for k in range(1, TOP_K):
        fx = fx + tw[:, k:k + 1] * yg_ref[k].astype(F32)
    return fx


def _end_kernel(x1_ref, yg_ref, tw_ref, mod_ref, gpost_ref, modn_ref, gpren_ref, x2_ref, h_ref):
    x2 = x1_ref[...] + mod_ref[0, 5:6, :] * _rms(_combine(yg_ref, tw_ref), gpost_ref[...])
    x2_ref[...] = x2
    y = _rms(x2, gpren_ref[...])
    h_ref[...] = (y * (1.0 + modn_ref[0, 1:2, :]) + modn_ref[0, 0:1, :]).astype(h_ref.dtype)


def _final_kernel(x1_ref, yg_ref, tw_ref, mod_ref, gpost_ref, x2_ref):
    x2_ref[...] = x1_ref[...] + mod_ref[0, 5:6, :] * _rms(_combine(yg_ref, tw_ref), gpost_ref[...])


def _row_specs(tile, d, n_lat):
    row = lambda w: pl.BlockSpec((tile, w), lambda i: (i, 0))
    mod = pl.BlockSpec((1, 8, d), lambda i: (jnp.minimum(i // (n_lat // tile), 1), 0, 0))
    vec = lambda w: pl.BlockSpec((1, w), lambda i: (0, 0))
    return row, mod, vec


def _pre(x, mods, g, n_lat):
    rows, d = x.shape
    row, mod, vec = _row_specs(ROW_TILE, d, n_lat)
    return pl.pallas_call(
        _pre_kernel, out_shape=jax.ShapeDtypeStruct((rows, d), BF16),
        grid=(rows // ROW_TILE,), in_specs=[row(d), mod, vec(d)], out_specs=row(d),
        compiler_params=_cparams(ROW_TILE * d * 24, 1), name="pre_norm",
    )(x, mods, g.reshape(1, d))


def _mid(x, y, mods, gpost, gpre, w_router, b_router, n_lat):
    rows, d = x.shape
    row, mod, vec = _row_specs(ROW_TILE, d, n_lat)
    wr = jnp.zeros((d, V7X_LANES), F32).at[:, :N_EXPERTS].set(w_router)
    br = jnp.full((1, V7X_LANES), -jnp.inf, F32).at[0, :N_EXPERTS].set(b_router)
    return pl.pallas_call(
        _mid_kernel,
        out_shape=(jax.ShapeDtypeStruct((rows, d), F32), jax.ShapeDtypeStruct((rows, d // 2), jnp.uint32),
                   jax.ShapeDtypeStruct((rows, V7X_LANES), I32), jax.ShapeDtypeStruct((rows, V7X_LANES), F32)),
        grid=(rows // ROW_TILE,),
        in_specs=[row(d), row(d), mod, vec(d), vec(d),
                  pl.BlockSpec((d, V7X_LANES), lambda i: (0, 0)), vec(V7X_LANES)],
        out_specs=(row(d), row(d // 2), row(V7X_LANES), row(V7X_LANES)),
        compiler_params=_cparams(ROW_TILE * d * 48 + d * V7X_LANES * 8, 1), name="post_mix_router",
    )(x, y, mods, gpost.reshape(1, d), gpre.reshape(1, d), wr, br)


def _end(x1, yg, top_w, mods, gpost, mods_next, gpre_next, n_lat):
    rows, d = x1.shape
    t = COMBINE_TILE
    row, mod, vec = _row_specs(t, d, n_lat)
    return pl.pallas_call(
        _end_kernel,
        out_shape=(jax.ShapeDtypeStruct((rows, d), F32), jax.ShapeDtypeStruct((rows, d), BF16)),
        grid=(rows // t,),
        in_specs=[row(d), pl.BlockSpec((TOP_K, t, d), lambda i: (0, i, 0)), row(V7X_LANES),
                  mod, vec(d), mod, vec(d)],
        out_specs=(row(d), row(d)),
        compiler_params=_cparams(t * d * (2 * 4 * (TOP_K + 2) + 2 * 2 + 16), 1), name="post_ffn_pre_norm",
    )(x1, yg, top_w, mods, gpost.reshape(1, d), mods_next, gpre_next.reshape(1, d))


def _final(x1, yg, top_w, mods, gpost, n_lat):
    d = x1.shape[1]
    t = COMBINE_TILE
    row, mod, vec = _row_specs(t, d, n_lat)
    return pl.pallas_call(
        _final_kernel, out_shape=jax.ShapeDtypeStruct((n_lat, d), F32),
        grid=(n_lat // t,),
        in_specs=[row(d), pl.BlockSpec((TOP_K, t, d), lambda i: (0, i, 0)), row(V7X_LANES), mod, vec(d)],
        out_specs=row(d),
        compiler_params=_cparams(t * d * (2 * 4 * (TOP_K + 2) + 16), 1), name="post_ffn",
    )(x1, yg, top_w, mods, gpost.reshape(1, d))


def _mm_kernel(a_ref, b_ref, o_ref, bscr):
    @pl.when(pl.program_id(1) == 0)
    def _():
        bscr[...] = b_ref[0].astype(BF16)

    o_ref[...] = _dot(a_ref[...], bscr[...]).astype(o_ref.dtype)


def _pick_tile(n, prefs):
    for t in prefs:
        if n % t == 0:
            return t
    return n


def _matmul(a, w3, layer, out_dtype):
    m, k = a.shape
    n = w3.shape[2]
    tm = _pick_tile(m, (384, 256, 128))
    tn = _pick_tile(n, (1024, 512, 256, 128))
    vm = 2 * (tm * k * 2 + k * tn * 4 + tm * tn * 4) + k * tn * 2 + tm * tn * 4
    return pl.pallas_call(
        _mm_kernel, out_shape=jax.ShapeDtypeStruct((m, n), out_dtype),
        grid=(n // tn, m // tm),
        in_specs=[pl.BlockSpec((tm, k), lambda j, i: (i, 0)),
                  pl.BlockSpec((1, k, tn), lambda j, i: (layer, 0, j))],
        out_specs=pl.BlockSpec((tm, tn), lambda j, i: (i, j)),
        scratch_shapes=[pltpu.VMEM((k, tn), BF16)],
        compiler_params=_cparams(vm, 2), name="matmul",
    )(a, w3)


def _dft_cos_sin(n):
    i = jnp.arange(n, dtype=I32)
    ang = ((i[:, None] * i[None, :]) % n).astype(F32) * (2.0 * np.pi / n)
    s = 1.0 / np.sqrt(n)
    return jnp.cos(ang) * s, jnp.sin(ang) * s


def _fft1_kernel(x_ref, f_ref, twr_ref, twi_ref, o_ref):
    n1 = f_ref.shape[1]
    tb, _, w = o_ref.shape[1:]
    xt = pltpu.einshape("abw->baw", x_ref[...])
    for j in range(tb):
        res = _dot(f_ref[...], xt[j])
        ar = res[:n1]
        ai = res[n1:]
        wr = jnp.tile(twr_ref[j], (1, w // V7X_LANES))
        wi = jnp.tile(twi_ref[j], (1, w // V7X_LANES))
        o_ref[0, j] = (ar * wr - ai * wi).astype(o_ref.dtype)
        o_ref[1, j] = (ar * wi + ai * wr).astype(o_ref.dtype)


def _fft2_kernel(m_ref, a_ref, o_ref, sr, si):
    n2, tc = a_ref.shape[1:3]
    ar = pltpu.einshape("bcw->cbw", a_ref[0])
    ai = pltpu.einshape("bcw->cbw", a_ref[1])
    for c in range(tc):
        res = _dot(m_ref[...], jnp.concatenate([ar[c], ai[c]], axis=0))
        sr[c] = res[:n2].astype(sr.dtype)
        si[c] = res[n2:].astype(si.dtype)
    o_ref[0] = pltpu.einshape("cdw->dcw", sr[...])
    o_ref[1] = pltpu.einshape("cdw->dcw", si[...])


def _dft_dense_kernel(m_ref, a_ref, o_ref):
    o_ref[...] = _dot(m_ref[...], a_ref[...]).astype(o_ref.dtype)


def _fft3_kernel(x_ref, cb_ref, sb_ref, wf_ref, bf_ref, o_ref):
    z = _dot(x_ref[0], cb_ref[...]) + _dot(x_ref[1], sb_ref[...])
    o_ref[...] = (_dot(z.astype(BF16), wf_ref[...]) + bf_ref[...]).astype(o_ref.dtype)


def _fourier_mix(p, w, w_fnet, b_fnet, n_lat):
    rows, cols = p.shape
    n_ctx = rows - n_lat
    n1 = FFT_N1
    n2 = n_lat // n1
    tb = BF16_SUBLANES
    c1, s1 = _dft_cos_sin(n1)
    f1 = jnp.concatenate([c1, -s1], axis=0).astype(BF16)
    bi = jnp.arange(n2, dtype=I32)[:, None]
    ci = jnp.arange(n1, dtype=I32)[None, :]
    tang = ((bi * ci) % n_lat).astype(F32) * (2.0 * np.pi / n_lat)
    twr = jnp.broadcast_to(jnp.cos(tang)[:, :, None], (n2, n1, V7X_LANES))
    twi = jnp.broadcast_to(-jnp.sin(tang)[:, :, None], (n2, n1, V7X_LANES))
    a = pl.pallas_call(
        _fft1_kernel, out_shape=jax.ShapeDtypeStruct((2, n2, n1, w), BF16),
        grid=(n2 // tb,),
        in_specs=[pl.BlockSpec((n1, tb, w), lambda j: (0, j, 0)),
                  pl.BlockSpec((2 * n1, n1), lambda j: (0, 0)),
                  pl.BlockSpec((tb, n1, V7X_LANES), lambda j: (j, 0, 0)),
                  pl.BlockSpec((tb, n1, V7X_LANES), lambda j: (j, 0, 0))],
        out_specs=pl.BlockSpec((2, tb, n1, w), lambda j: (0, j, 0, 0)),
        compiler_params=_cparams(2 * 3 * n1 * tb * w * 2 + 8 * n1 * w * 4, 1), name="fft_stage1",
    )(p.reshape(rows // n2, n2, cols), f1, twr, twi)
    c2, s2 = _dft_cos_sin(n2)
    m2 = jnp.concatenate([jnp.concatenate([c2, s2], axis=1),
                          jnp.concatenate([-s2, c2], axis=1)], axis=0).astype(BF16)
    tc = BF16_SUBLANES
    xl = pl.pallas_call(
        _fft2_kernel, out_shape=jax.ShapeDtypeStruct((2, n2, n1, w), BF16),
        grid=(n1 // tc,),
        in_specs=[pl.BlockSpec((2 * n2, 2 * n2), lambda j: (0, 0)),
                  pl.BlockSpec((2, n2, tc, w), lambda j: (0, 0, j, 0))],
        out_specs=pl.BlockSpec((2, n2, tc, w), lambda j: (0, 0, j, 0)),
        scratch_shapes=[pltpu.VMEM((tc, n2, w), BF16), pltpu.VMEM((tc, n2, w), BF16)],
        compiler_params=_cparams(5 * 2 * n2 * tc * w * 2 + 8 * n2 * w * 4, 1), name="fft_stage2",
    )(m2, a).reshape(2, n_lat, w)
    cc, sc = _dft_cos_sin(n_ctx)
    mc = jnp.concatenate([cc, -sc], axis=0).astype(BF16)
    xc = pl.pallas_call(
        _dft_dense_kernel, out_shape=jax.ShapeDtypeStruct((2 * n_ctx, w), BF16),
        grid=(1,),
        in_specs=[pl.BlockSpec((2 * n_ctx, n_ctx), lambda j: (0, 0)),
                  pl.BlockSpec((n_ctx, w), lambda j: (n_lat // n_ctx, 0))],
        out_specs=pl.BlockSpec((2 * n_ctx, w), lambda j: (0, 0)),
        compiler_params=_cparams(16 * n_ctx * w, 1), name="dft_context",
    )(mc, p).reshape(2, n_ctx, w)
    xall = jnp.concatenate([xl, xc], axis=1)
    cg, sg = _dft_cos_sin(HEAD_DIM)
    eye = jnp.eye(w // HEAD_DIM, dtype=F32)
    cb = jnp.kron(eye, cg).astype(BF16)
    sb = jnp.kron(eye, sg).astype(BF16)
    full = lambda r, c: pl.BlockSpec((r, c), lambda i: (0, 0))
    return pl.pallas_call(
        _fft3_kernel, out_shape=jax.ShapeDtypeStruct((rows, w), BF16),
        grid=(rows // ROW_TILE,),
        in_specs=[pl.BlockSpec((2, ROW_TILE, w), lambda i: (0, i, 0)),
                  full(w, w), full(w, w), full(w, w), full(1, w)],
        out_specs=pl.BlockSpec((ROW_TILE, w), lambda i: (i, 0)),
        compiler_params=_cparams(2 * (3 * w * w * 2 + 3 * ROW_TILE * w * 2) + ROW_TILE * w * 12, 1),
        name="fft_channel_linear",
    )(xall, cb, sb, w_fnet.astype(BF16), b_fnet.reshape(1, w))


def _na_tables(rows):
    kr = min(NA_KR, rows)
    nb = min(kr + NA_QROWS - 1, rows)
    nqb = rows // NA_QROWS
    r0 = np.arange(nqb) * NA_QROWS
    band0 = np.minimum(np.clip(r0 - kr // 2, 0, rows - kr), rows - nb)
    band_rows = band0[:, None] + np.arange(nb)[None, :]
    q_row = np.repeat(r0[:, None] + np.arange(NA_QROWS)[None, :], GRID_W, axis=1)
    q_col = np.tile(np.arange(GRID_W), NA_QROWS)
    k_row = np.repeat(band_rows, GRID_W, axis=1)
    k_col = np.tile(np.arange(GRID_W), nb)
    win_r = np.clip(q_row - kr // 2, 0, rows - kr)[:, :, None]
    win_c = np.clip(q_col - NA_KC // 2, 0, GRID_W - NA_KC)[:, None]
    kro = k_row[:, None, :]
    col_ok = (k_col[None, :] >= win_c) & (k_col[None, :] < win_c + NA_KC)
    mask = (kro >= win_r) & (kro < win_r + kr) & col_ok[None]
    d_row = np.clip(kro - q_row[:, :, None] + NA_KR - 1, 0, 2 * NA_KR - 2)
    d_col = np.clip(k_col[None, :] - q_col[:, None] + NA_KC - 1, 0, 2 * NA_KC - 2)
    pats, pid = [], np.zeros(nqb, np.int32)
    for n in range(nqb):
        for p, (m0, d0) in enumerate(pats):
            if np.array_equal(m0, mask[n]) and np.array_equal(d0, d_row[n]):
                pid[n] = p
                break
        else:
            pid[n] = len(pats)
            pats.append((mask[n], d_row[n]))
    pmask = np.stack([p[0] for p in pats])
    pdrow = np.stack([p[1] for p in pats])
    return (band0 * GRID_W).astype(np.int32), pid, pmask, pdrow, d_col, nb


def _na_bias(rpb, pmask, pdrow, d_col, nb, n_ctx):
    n_heads = rpb.shape[0]
    n_pat, qb, nk = pmask.shape
    tiles = rpb.astype(F32)[:, :, d_col[:GRID_W, :GRID_W]]
    drow_small = pdrow[:, ::GRID_W, ::GRID_W]
    b6 = tiles[:, drow_small]
    bias = b6.transpose(0, 1, 2, 4, 3, 5).reshape(n_heads, n_pat, qb, nk)
    return jnp.pad(jnp.where(pmask[None], bias, MASK_VALUE), ((0, 0), (0, 0), (0, 0), (0, n_ctx)))


def _na_kernel(pid_ref, st_ref, q_ref, k_ref, v_ref, cos_ref, sin_ref, bias_ref, o_ref, qs, ks,
               *, n_lat, n_ctx, nqb, qb, nk):
    scale = HEAD_DIM ** -0.5
    rt = 512 if n_lat % 512 == 0 else qb

    def rope_body(i, carry):
        r = pl.multiple_of(i * rt, rt)
        c = cos_ref[pl.ds(r, rt), :]
        s = sin_ref[pl.ds(r, rt), :]
        even = (lax.broadcasted_iota(I32, (rt, HEAD_DIM), 1) % 2) == 0

        def rot(x):
            xs = jnp.where(even, pltpu.roll(x, HEAD_DIM - 1, 1), pltpu.roll(x, 1, 1))
            return x * c + xs * s

        qs[pl.ds(r, rt), :] = (rot(q_ref[pl.ds(r, rt), :].astype(F32)) * scale).astype(BF16)
        ks[pl.ds(r, rt), :] = rot(k_ref[pl.ds(r, rt), :].astype(F32)).astype(BF16)
        return carry

    lax.fori_loop(0, n_lat // rt, rope_body, 0)
    kc = k_ref[n_lat:n_lat + n_ctx, :]
    vc = v_ref[n_lat:n_lat + n_ctx, :]

    def block_body(n, carry):
        r = pl.multiple_of(n * qb, qb)
        st = pl.multiple_of(st_ref[n], GRID_W)
        k_all = jnp.concatenate([ks[pl.ds(st, nk), :], kc], axis=0)
        v_all = jnp.concatenate([v_ref[pl.ds(st, nk), :], vc], axis=0)
        s = _dot_nt(qs[pl.ds(r, qb), :], k_all) + bias_ref[0, pid_ref[n]]
        p = jnp.exp(s - jnp.max(s, axis=-1, keepdims=True))
        o = _dot(p.astype(BF16), v_all) / jnp.sum(p, axis=-1, keepdims=True)
        o_ref[pl.ds(r, qb), :] = o.astype(o_ref.dtype)
        return carry

    lax.fori_loop(0, nqb, block_body, 0, unroll=NA_UNROLL)
    s = _dot_nt(q_ref[n_lat:n_lat + n_ctx, :], kc) * scale
    p = jnp.exp(s - jnp.max(s, axis=-1, keepdims=True))
    o = _dot(p.astype(BF16), vc) / jnp.sum(p, axis=-1, keepdims=True)
    o_ref[n_lat:n_lat + n_ctx, :] = o.astype(o_ref.dtype)


def _na(p, col0, n_heads, rpb, cos_rep, sin_sgn, n_lat):
    rows = p.shape[0]
    n_ctx = rows - n_lat
    grid_rows = n_lat // GRID_W
    st, pid, pmask, pdrow, d_col, nb = _na_tables(grid_rows)
    n_pat, qb, nk = pmask.shape
    nqb = grid_rows // NA_QROWS
    bias = _na_bias(rpb, pmask, pdrow, d_col, nb, n_ctx)
    head = lambda off: pl.BlockSpec((rows, HEAD_DIM), lambda h, *_: (0, col0 + off * n_heads + h))
    kern = functools.partial(_na_kernel, n_lat=n_lat, n_ctx=n_ctx, nqb=nqb, qb=qb, nk=nk)
    vm = 2 * (4 * rows * HEAD_DIM * 2 + n_pat * qb * nk * 4) + 2 * n_lat * HEAD_DIM * 4 \
        + 2 * n_lat * HEAD_DIM * 2 + 8 * qb * (nk + n_ctx) * 4
    return pl.pallas_call(
        kern, out_shape=jax.ShapeDtypeStruct((rows, n_heads * HEAD_DIM), BF16),
        grid_spec=pltpu.PrefetchScalarGridSpec(
            num_scalar_prefetch=2, grid=(n_heads,),
            in_specs=[head(0), head(1), head(2),
                      pl.BlockSpec(memory_space=pltpu.VMEM), pl.BlockSpec(memory_space=pltpu.VMEM),
                      pl.BlockSpec((1, n_pat, qb, nk + n_ctx), lambda h, *_: (h, 0, 0, 0))],
            out_specs=pl.BlockSpec((rows, HEAD_DIM), lambda h, *_: (0, h)),
            scratch_shapes=[pltpu.VMEM((n_lat, HEAD_DIM), BF16), pltpu.VMEM((n_lat, HEAD_DIM), BF16)]),
        compiler_params=_cparams(vm, 1), name="neighborhood_attention",
    )(jnp.asarray(pid), jnp.asarray(st), p, p, p, cos_rep, sin_sgn, bias)


def _hgrn_gates(z, lbv):
    sg = jax.nn.sigmoid(z)
    f = lbv + (1.0 - lbv) * sg
    lf = jnp.log(jnp.maximum(f, GATE_FLOOR))
    kk = (1.0 - lbv) * (1.0 - sg)
    return lf, kk


def _hgrn_kernel(hq_ref, hi_ref, hf_ref, hb_ref, hg_ref, lb_ref, gn_ref, o_ref, acc, stf, stb,
                 *, n_lat, n_ctx):
    c_rows, sub = HGRN_CHUNK, HGRN_SUB
    n_sub = c_rows // sub
    dk = HEAD_DIM
    rows = n_lat + n_ctx
    lbf = lb_ref[0, 0:1, :]
    lbb = lb_ref[0, 1:2, :]

    def decay_body(i, mn):
        r = pl.multiple_of(i * c_rows, c_rows)
        lf_f, _ = _hgrn_gates(hf_ref[pl.ds(r, c_rows), :].astype(F32), lbf)
        lf_b, _ = _hgrn_gates(hb_ref[pl.ds(r, c_rows), :].astype(F32), lbb)
        for j in range(n_sub):
            mn = jnp.minimum(mn, jnp.sum(lf_f[j * sub:(j + 1) * sub], axis=0, keepdims=True))
            mn = jnp.minimum(mn, jnp.sum(lf_b[j * sub:(j + 1) * sub], axis=0, keepdims=True))
        return mn

    mn = lax.fori_loop(0, rows // c_rows, decay_body, jnp.zeros((1, dk), F32))
    safe = jnp.min(mn) >= -HGRN_SAFE_DECAY

    acc[...] = jnp.zeros(acc.shape, F32)
    stf[...] = jnp.zeros(stf.shape, F32)
    stb[...] = jnp.zeros(stb.shape, F32)

    row_i = lax.broadcasted_iota(I32, (c_rows, dk), 0)
    t_i = lax.broadcasted_iota(I32, (c_rows, c_rows), 0)
    s_i = lax.broadcasted_iota(I32, (c_rows, c_rows), 1)

    def chunk(r0, z_ref, lbv, st_ref, rev):
        lf, kk = _hgrn_gates(z_ref[pl.ds(r0, c_rows), :].astype(F32), lbv)
        q = _silu(hq_ref[pl.ds(r0, c_rows), :].astype(F32))
        v = hi_ref[pl.ds(r0, c_rows), :]
        b = lf
        step = 1
        while step < c_rows:
            if rev:
                b = b + jnp.where(row_i < c_rows - step, pltpu.roll(b, c_rows - step, 0), 0.0)
            else:
                b = b + jnp.where(row_i >= step, pltpu.roll(b, step, 0), 0.0)
            step *= 2
        zero = jnp.zeros((1, dk), F32)
        refs = []
        for i in range(n_sub):
            if rev:
                refs.append(b[(i + 1) * sub:(i + 1) * sub + 1] if i < n_sub - 1 else zero)
            else:
                refs.append(b[i * sub - 1:i * sub] if i > 0 else zero)
        qd = jnp.concatenate([(q[i * sub:(i + 1) * sub] * jnp.exp(b[i * sub:(i + 1) * sub] - refs[i])).astype(BF16)
                              for i in range(n_sub)], axis=0)
        ks = []
        for i in range(n_sub):
            lo, hi = (i * sub, c_rows) if rev else (0, (i + 1) * sub)
            part = (kk[lo:hi] * jnp.exp(refs[i] - b[lo:hi])).astype(BF16)
            if hi - lo < c_rows:
                pad = jnp.zeros((c_rows - (hi - lo), dk), BF16)
                part = jnp.concatenate([pad, part] if rev else [part, pad], axis=0)
            ks.append(part)
        scores = _dot_nt(qd, jnp.concatenate(ks, axis=0))
        own = jnp.concatenate([scores[i * sub:(i + 1) * sub, i * c_rows:(i + 1) * c_rows]
                               for i in range(n_sub)], axis=0)
        pm = jnp.where((s_i >= t_i) if rev else (s_i <= t_i), own, 0.0).astype(BF16)
        st = st_ref[...]
        o = _dot(pm, v) + _dot_nt((q * jnp.exp(b)).astype(BF16), st.astype(BF16))
        blast = b[0:1] if rev else b[c_rows - 1:c_rows]
        ke = (kk * jnp.exp(blast - b)).astype(BF16)
        st_ref[...] = st * jnp.exp(blast) + _dot_tn(v, ke)
        return o

    def fast_path():
        def run(base, n):
            def body(c, carry):
                rf = pl.multiple_of(base + c * c_rows, c_rows)
                rb = pl.multiple_of(base + (n - 1 - c) * c_rows, c_rows)
                o_f = chunk(rf, hf_ref, lbf, stf, False)
                acc[pl.ds(rf, c_rows), :] = acc[pl.ds(rf, c_rows), :] + o_f
                o_b = chunk(rb, hb_ref, lbb, stb, True)
                acc[pl.ds(rb, c_rows), :] = acc[pl.ds(rb, c_rows), :] + o_b
                return carry
            lax.fori_loop(0, n, body, 0, unroll=HGRN_UNROLL)
        run(n_lat, n_ctx // c_rows)
        run(0, n_lat // c_rows)

    def slow_path():
        pack = BF16_SUBLANES
        sub_i = lax.broadcasted_iota(I32, (pack, dk), 0)
        row8 = lax.broadcasted_iota(I32, (8, dk), 0)

        def load_row(ref, t):
            r = pl.multiple_of((t // pack) * pack, pack)
            blk = ref[pl.ds(r, pack), :].astype(F32)
            return jnp.sum(jnp.where(sub_i == t - r, blk, 0.0), axis=0, keepdims=True)

        def run(base, n, z_ref, lbv, st_ref, rev):
            def body(i, carry):
                t = base + ((n - 1 - i) if rev else i)
                lf, kk = _hgrn_gates(load_row(z_ref, t), lbv)
                q = _silu(load_row(hq_ref, t))
                v = load_row(hi_ref, t)
                v8 = jnp.where(row8 == 0, v, 0.0).astype(BF16)
                k8 = jnp.where(row8 == 0, kk, 0.0).astype(BF16)
                st = st_ref[...] * jnp.exp(lf) + _dot_tn(v8, k8)
                st_ref[...] = st
                q8 = jnp.broadcast_to(q, (8, dk)).astype(BF16)
                o = _dot_nt(q8, st.astype(BF16))
                acc[pl.ds(t, 1), :] = acc[pl.ds(t, 1), :] + o[0:1]
                return carry
            lax.fori_loop(0, n, body, 0)
        run(n_lat, n_ctx, hf_ref, lbf, stf, False)
        run(0, n_lat, hf_ref, lbf, stf, False)
        run(n_lat, n_ctx, hb_ref, lbb, stb, True)
        run(0, n_lat, hb_ref, lbb, stb, True)

    lax.cond(safe, fast_path, slow_path)

    rt = ROW_TILE

    def readout(i, carry):
        r = pl.multiple_of(i * rt, rt)
        y = _rms(acc[pl.ds(r, rt), :], gn_ref[...])
        o_ref[pl.ds(r, rt), :] = (y * _silu(hg_ref[pl.ds(r, rt), :].astype(F32))).astype(o_ref.dtype)
        return carry

    lax.fori_loop(0, rows // rt, readout, 0)


def _hgrn(p, col0, n_heads, lb, g_norm, n_lat):
    rows = p.shape[0]
    n_ctx = rows - n_lat
    head = lambda off: pl.BlockSpec((rows, HEAD_DIM), lambda h: (0, col0 + off * n_heads + h))
    lbh = lb.reshape(2, n_heads, HEAD_DIM).transpose(1, 0, 2)
    kern = functools.partial(_hgrn_kernel, n_lat=n_lat, n_ctx=n_ctx)
    n_keys = HGRN_CHUNK * HGRN_CHUNK // HGRN_SUB
    vm = 2 * 6 * rows * HEAD_DIM * 2 + rows * HEAD_DIM * 4 + 6 * HGRN_UNROLL * HGRN_CHUNK * n_keys * 4
    return pl.pallas_call(
        kern, out_shape=jax.ShapeDtypeStruct((rows, n_heads * HEAD_DIM), BF16),
        grid=(n_heads,),
        in_specs=[head(0), head(1), head(2), head(3), head(4),
                  pl.BlockSpec((1, 2, HEAD_DIM), lambda h: (h, 0, 0)),
                  pl.BlockSpec((1, HEAD_DIM), lambda h: (0, 0))],
        out_specs=pl.BlockSpec((rows, HEAD_DIM), lambda h: (0, h)),
        scratch_shapes=[pltpu.VMEM((rows, HEAD_DIM), F32), pltpu.VMEM((HEAD_DIM, HEAD_DIM), F32),
                        pltpu.VMEM((HEAD_DIM, HEAD_DIM), F32)],
        compiler_params=_cparams(vm, 1), name="hgrn2_bidir",
    )(p, p, p, p, p, lbh, g_norm.reshape(1, HEAD_DIM))


STEP_LOAD, STEP_COMPUTE, STEP_IDLE, STEP_ZERO = 0, 1, 2, 3


def _deinterleave_perm():
    g = V7X_MXU_DIM
    src = np.arange(g)
    dst = np.where(src % 2 == 0, src // 2, g // 2 + src // 2)
    perm = np.zeros((g, g), np.float32)
    perm[src, dst] = 1.0
    return jnp.asarray(perm, BF16)


def _expert_kernel(kind_ref, wexp_ref, wchunk_ref, wslot_ref, blk_ref, cexp_ref, cslot_ref, nv_ref,
                   tok_cur, tok_nxt, h_hbm, wgu_ref, wdn_ref, bgu_ref, bdn_ref, perm_ref, y_ref,
                   wgu_s, wdn_s, xbuf, sems):
    s = pl.program_id(0)
    kind = kind_ref[s]
    tm = xbuf.shape[1]

    def cast_chunk():
        c = wchunk_ref[s]
        sl = wslot_ref[s]
        rg = wgu_ref.shape[0]
        rd = wdn_ref.shape[0]
        wgu_s[sl, pl.ds(pl.multiple_of(c * rg, rg), rg), :] = wgu_ref[...].astype(BF16)
        wdn_s[sl, pl.ds(pl.multiple_of(c * rd, rd), rd), :] = wdn_ref[...].astype(BF16)

    @pl.when(kind == STEP_LOAD)
    def _():
        cast_chunk()

    @pl.when(kind == STEP_ZERO)
    def _():
        y_ref[...] = jnp.zeros(y_ref.shape, y_ref.dtype)

    @pl.when(kind == STEP_COMPUTE)
    def _():
        b = blk_ref[s]
        n_valid = nv_ref[0]
        slot = b % 2
        cs = cslot_ref[s]

        def row_copy(tok_ref, r, dst_slot):
            return pltpu.make_async_copy(h_hbm.at[pl.ds(tok_ref[0, 0, r], 1), :],
                                         xbuf.at[dst_slot, pl.ds(r, 1), :], sems.at[dst_slot])

        def block_wait(dst_slot):
            pltpu.make_async_copy(h_hbm.at[pl.ds(0, tm), :], xbuf.at[dst_slot], sems.at[dst_slot]).wait()

        @pl.when(b == 0)
        def _():
            def body(r, carry):
                row_copy(tok_cur, r, 0).start()
                return carry
            lax.fori_loop(0, tm, body, 0, unroll=8)

        block_wait(slot)
        cast_chunk()
        g, h = V7X_MXU_DIM, V7X_LANES
        half = tm // 2
        perm = perm_ref[...]
        dp = xbuf.shape[2]
        k_chunks = 8
        kc = dp // k_chunks
        per_chunk = tm // k_chunks

        def unpack(words):
            lo16 = lax.bitcast_convert_type(words << 16, F32).astype(BF16)
            hi16 = lax.bitcast_convert_type(words & jnp.uint32(0xFFFF0000), F32).astype(BF16)
            return lo16, hi16

        for r in (0, half):
            gu = bgu_ref[0]
            for c in range(k_chunks):
                if r == 0:
                    for i in range(per_chunk):
                        row = c * per_chunk + i
                        row_copy(tok_nxt, row, 1 - slot).start(priority=row % 2)
                x_lo, x_hi = unpack(xbuf[slot, r:r + half, c * kc:(c + 1) * kc])
                gu = gu + _dot(x_lo, wgu_s[cs, c * kc:(c + 1) * kc, :]) \
                    + _dot(x_hi, wgu_s[cs, dp + c * kc:dp + (c + 1) * kc, :])
            hi = gu.astype(BF16)
            lo = (gu - hi.astype(F32)).astype(BF16)
            parts = [_dot(hi[:, j * g:(j + 1) * g], perm) + _dot(lo[:, j * g:(j + 1) * g], perm)
                     for j in range(gu.shape[1] // g)]
            gate = jnp.minimum(jnp.concatenate([p[:, :h] for p in parts], axis=1), SWIGLU_LIMIT)
            up = jnp.clip(jnp.concatenate([p[:, h:] for p in parts], axis=1), -SWIGLU_LIMIT, SWIGLU_LIMIT)
            act = gate * jax.nn.sigmoid(SWIGLU_ALPHA * gate) * (up + 1.0)
            y_ref[r:r + half, :] = (_dot(act.astype(BF16), wdn_s[cs]) + bdn_ref[0]).astype(y_ref.dtype)

        @pl.when(b + 1 >= n_valid)
        def _():
            block_wait(1 - slot)


def _expert_program(counts, block_e, n_valid, n_blocks):
    n_e, n_ch = N_EXPERTS, MOE_WCHUNKS
    n_steps = n_blocks + n_ch * n_e
    nb = (counts + MOE_TILE - 1) // MOE_TILE
    active = nb > 0
    bstart = jnp.cumsum(nb) - nb
    e_idx = jnp.arange(n_e, dtype=I32)
    at_or_after = lax.cummin(jnp.where(active, e_idx, n_e)[::-1], axis=0)[::-1]
    next_active = jnp.concatenate([at_or_after[1:], jnp.full((1,), n_e, I32)])
    first_active = at_or_after[0]
    nl = jnp.where(active & (next_active < n_e), n_ch, 0)
    seg_len = jnp.where(active, jnp.maximum(nb, nl), 0)
    seg_start = n_ch + jnp.cumsum(seg_len) - seg_len
    slot = (jnp.cumsum(active.astype(I32)) - active.astype(I32)) % 2
    j = jnp.arange(n_ch, dtype=I32)
    lpos = jnp.where((nl > 0)[:, None], seg_start[:, None] + j[None, :], n_steps).reshape(-1)
    lpos = jnp.concatenate([j, lpos])
    lexp = jnp.concatenate([jnp.broadcast_to(first_active, (n_ch,)),
                            jnp.broadcast_to(jnp.minimum(next_active, n_e - 1)[:, None], (n_e, n_ch)).reshape(-1)])
    lslot = jnp.concatenate([jnp.zeros((n_ch,), I32),
                             jnp.broadcast_to((1 - slot)[:, None], (n_e, n_ch)).reshape(-1)])
    lchunk = jnp.concatenate([j, jnp.broadcast_to(j[None, :], (n_e, n_ch)).reshape(-1)])
    b = jnp.arange(n_blocks, dtype=I32)
    jb = b - bstart[block_e]
    cpos = jnp.where(b < n_valid[0], seg_start[block_e] + jb, n_steps)
    fill = lambda pos, val: jnp.zeros((n_steps + 1,), I32).at[pos].set(val)[:n_steps]
    kind = jnp.full((n_steps + 1,), STEP_IDLE, I32).at[lpos].set(STEP_LOAD).at[cpos].set(STEP_COMPUTE)[:n_steps]
    step = jnp.arange(n_steps, dtype=I32)
    last_load = lax.cummax(jnp.where(fill(lpos, jnp.ones_like(lpos)) > 0, step, 0), axis=0)
    last_comp = lax.cummax(jnp.where(kind == STEP_COMPUTE, step, -1), axis=0)
    wexp = fill(lpos, lexp)[last_load]
    wchunk = fill(lpos, lchunk)[last_load]
    wslot = fill(lpos, lslot)[last_load]
    blk = jnp.where(last_comp >= 0, fill(cpos, b)[jnp.maximum(last_comp, 0)], 0)
    n_used = n_ch + jnp.sum(seg_len)
    spare = n_valid[0] + step - n_used
    kind = jnp.where((step >= n_used) & (spare < n_blocks), STEP_ZERO, kind)
    blk = jnp.where(step >= n_used, jnp.minimum(spare, n_blocks - 1), blk)
    cexp = block_e[blk]
    return kind, wexp, wchunk, wslot, blk, cexp, slot[cexp]


def _experts(h2, row_tok, counts, block_e, n_valid, w_gate_up, b_gate_up, w_down, b_down, layer):
    d = w_down.shape[3]
    n_blocks = row_tok.shape[0] // MOE_TILE
    ff = w_down.shape[2]
    n_ch = MOE_WCHUNKS
    prog = _expert_program(counts, block_e, n_valid, n_blocks)
    n_steps = prog[0].shape[0]
    tok3 = row_tok.reshape(n_blocks, 1, MOE_TILE)
    tok_spec = lambda off: pl.BlockSpec(
        (1, 1, MOE_TILE), lambda s, kind, we, wc, ws, blk, *_: (jnp.minimum(blk[s] + off, n_blocks - 1), 0, 0),
        memory_space=pltpu.SMEM)
    wspec = lambda r, c: pl.BlockSpec((None, None, r, c), lambda s, kind, we, wc, *_: (layer, we[s], wc[s], 0))
    bspec = lambda c: pl.BlockSpec((1, 1, c), lambda s, kind, we, wc, ws, blk, ce, *_: (ce[s], 0, 0))
    vm = 2 * (d * 2 * ff * 2 + ff * d * 2) + 2 * (d * 2 * ff + ff * d) * 4 // n_ch \
        + 2 * MOE_TILE * d * 2 + 2 * MOE_TILE * d * 2 + MOE_TILE * (2 * ff * 10 + d * 6)
    return pl.pallas_call(
        _expert_kernel, out_shape=jax.ShapeDtypeStruct((n_blocks * MOE_TILE, d), BF16),
        grid_spec=pltpu.PrefetchScalarGridSpec(
            num_scalar_prefetch=8, grid=(n_steps,),
            in_specs=[tok_spec(0), tok_spec(1),
                      pl.BlockSpec(memory_space=pl.ANY),
                      wspec(d // n_ch, 2 * ff), wspec(ff // n_ch, d), bspec(2 * ff), bspec(d),
                      pl.BlockSpec((V7X_MXU_DIM, V7X_MXU_DIM), lambda s, *_: (0, 0))],
            out_specs=pl.BlockSpec((MOE_TILE, d), lambda s, kind, we, wc, ws, blk, *_: (blk[s], 0)),
            scratch_shapes=[pltpu.VMEM((2, d, 2 * ff), BF16), pltpu.VMEM((2, ff, d), BF16),
                            pltpu.VMEM((2, MOE_TILE, d // 2), jnp.uint32), pltpu.SemaphoreType.DMA((2,))]),
        compiler_params=_cparams(vm, 1), name="expert_ffn",
    )(*prog, n_valid, tok3, tok3, h2, w_gate_up, w_down,
      b_gate_up[layer].reshape(N_EXPERTS, 1, 2 * ff), b_down[layer].reshape(N_EXPERTS, 1, d), _deinterleave_perm())


def _route(top_idx, n_tok):
    flat_e = top_idx[:n_tok, :TOP_K].reshape(-1)
    n_pairs = flat_e.shape[0]
    onehot = (flat_e[:, None] == jnp.arange(N_EXPERTS, dtype=I32)[None, :]).astype(I32)
    csum = jnp.cumsum(onehot, axis=0)
    rank = jnp.sum(onehot * csum, axis=1) - 1
    counts = csum[-1]
    padded = (counts + MOE_TILE - 1) // MOE_TILE * MOE_TILE
    pend = jnp.cumsum(padded)
    dest = jnp.sum(onehot * (pend - padded)[None, :], axis=1) + rank
    n_blocks = -(-(n_pairs + N_EXPERTS * (MOE_TILE - 1)) // MOE_TILE)
    row_tok = jnp.zeros((n_blocks * MOE_TILE,), I32).at[dest].set(jnp.arange(n_pairs, dtype=I32) // TOP_K)
    n_valid = (pend[-1] // MOE_TILE).astype(I32).reshape(1)
    blk = jnp.minimum(jnp.arange(n_blocks, dtype=I32), n_valid[0] - 1) * MOE_TILE
    block_e = jnp.minimum(jnp.sum((pend[None, :] <= blk[:, None]).astype(I32), axis=1), N_EXPERTS - 1)
    return dest.reshape(n_tok, TOP_K), row_tok, counts, block_e, n_valid


def _moe(h2, top_idx, n_tok, w_gate_up, b_gate_up, w_down, b_down, layer):
    dest, row_tok, counts, block_e, n_valid = _route(top_idx, n_tok)
    y = _experts(h2, row_tok, counts, block_e, n_valid, w_gate_up, b_gate_up, w_down, b_down, layer)
    yg = y.at[dest.T.reshape(-1)].get(mode='promise_in_bounds')
    return yg.reshape(TOP_K, n_tok, -1)


def _rope_tables(n_tok):
    t = jnp.arange(n_tok, dtype=I32)
    row = (t // GRID_W).astype(F32)
    col = (t % GRID_W).astype(F32)
    pairs = HEAD_DIM // 4
    inv_freq = ROPE_THETA ** (-jnp.arange(pairs, dtype=F32) / pairs)
    ang = jnp.concatenate([row[:, None] * inv_freq, col[:, None] * inv_freq], axis=-1)
    cos_rep = jnp.repeat(jnp.cos(ang), 2, axis=-1)
    sin_sgn = jnp.stack([-jnp.sin(ang), jnp.sin(ang)], axis=-1).reshape(n_tok, HEAD_DIM)
    return cos_rep, sin_sgn


def kernel(x, c, ctx, c_ctx, w_ada, b_ada, norm_pre_mix, norm_post_mix, norm_pre_ffn, norm_post_ffn,
           w_in, w_fnet, b_fnet, na_rpb, hgrn_lb_logits, hgrn_out_norm, w_out,
           w_router, b_router, w_gate_up, b_gate_up, w_down, b_down):
    batch, n_lat, d = x.shape
    n_ctx = ctx.shape[1]
    assert batch == 1 and c.shape[0] == 1
    depth = w_ada.shape[0]
    fnet_w = w_fnet.shape[1]
    n_heads = na_rpb.shape[1]
    ff = w_down.shape[2]
    fnet_blocks = fnet_w // HEAD_DIM

    cvec = jnp.zeros((8, d), F32).at[0].set(c[0]).at[1].set(c_ctx)
    ada = _ada(cvec, w_ada, b_ada)
    mods = jnp.pad(ada[:, :2].reshape(depth, 2, 6, d), ((0, 0), (0, 0), (0, 2), (0, 0)))
    cos_rep, sin_sgn = _rope_tables(n_lat)
    p_lb = jax.nn.softmax(hgrn_lb_logits.astype(F32), axis=0)
    lower_bounds = jnp.cumsum(p_lb, axis=0) - p_lb[0]
    groups = 2 * ff // V7X_MXU_DIM

    xs = jnp.concatenate([x[0], ctx[0]], axis=0)
    h = _pre(xs, mods[0], norm_pre_mix[0], n_lat)
    for layer in range(depth):
        last = layer == depth - 1
        p = _matmul(h, w_in, layer, BF16)
        yf = _fourier_mix(p, fnet_w, w_fnet[layer], b_fnet[layer], n_lat)
        na = _na(p, fnet_blocks, n_heads, na_rpb[layer], cos_rep, sin_sgn, n_lat)
        hg = _hgrn(p, fnet_blocks + 3 * n_heads, n_heads, lower_bounds[layer], hgrn_out_norm[layer], n_lat)
        mix = jnp.concatenate([yf, na, hg], axis=1)
        y = _matmul(mix, w_out, layer, BF16)
        x1, h2, top_idx, top_w = _mid(xs, y, mods[layer], norm_post_mix[layer], norm_pre_ffn[layer],
                                      w_router[layer], b_router[layer], n_lat)
        yg = _moe(h2, top_idx, n_lat if last else n_lat + n_ctx, w_gate_up, b_gate_up, w_down, b_down, layer)
        if last:
            xs = _final(x1, yg, top_w, mods[layer], norm_post_ffn[layer], n_lat)
        else:
            xs, h = _end(x1, yg, top_w, mods[layer], norm_post_ffn[layer], mods[layer + 1],
                         norm_pre_mix[layer + 1], n_lat)
    return xs[:n_lat].reshape(batch, n_lat, d)
```

```python
import functools

import numpy as np
import jax
import jax.numpy as jnp
from jax import lax
from jax.experimental import pallas as pl
from jax.experimental.pallas import tpu as pltpu

F32 = jnp.float32
BF16 = jnp.bfloat16
I32 = jnp.int32

GRID_W = 64
HEAD_DIM = 128
NA_KR = 8
NA_KC = 16
NA_QROWS = 2
ROPE_THETA = 10000.0
N_EXPERTS = 32
TOP_K = 4
SWIGLU_LIMIT = 7.0
SWIGLU_ALPHA = 1.702
RMS_EPS = 1e-6
MASK_VALUE = -1e30
GATE_FLOOR = 1e-30

V7X_VMEM_BYTES = 64 * 1024 * 1024
V7X_LANES = 128
V7X_MXU_DIM = 256
BF16_SUBLANES = 16

ROW_TILE = 256
COMBINE_TILE = 128
MOE_TILE = 256
MOE_WCHUNKS = 8
HGRN_CHUNK = 128
HGRN_SUB = 16
HGRN_SAFE_DECAY = 80.0
HGRN_UNROLL = 8
NA_UNROLL = 8
FFT_N1 = 64


def _cparams(vmem_bytes, n_grid):
    limit = int(min(max(vmem_bytes * 5 // 4 + (4 << 20), 32 << 20), V7X_VMEM_BYTES - (4 << 20)))
    return pltpu.CompilerParams(dimension_semantics=("arbitrary",) * n_grid, vmem_limit_bytes=limit)


def _dot(a, b):
    return jnp.dot(a, b, preferred_element_type=F32)


def _dot_nt(a, b):
    return lax.dot_general(a, b, (((1,), (1,)), ((), ())), preferred_element_type=F32)


def _dot_tn(a, b):
    return lax.dot_general(a, b, (((0,), (0,)), ((), ())), preferred_element_type=F32)


def _silu(x):
    return x * jax.nn.sigmoid(x)


def _rms(x, g):
    return x * lax.rsqrt(jnp.mean(x * x, axis=-1, keepdims=True) + RMS_EPS) * g


def _ada_kernel(c_ref, w_ref, b_ref, o_ref):
    a = _silu(c_ref[...])
    a_hi = a.astype(BF16)
    a_lo = (a - a_hi.astype(F32)).astype(BF16)
    w = w_ref[0].astype(BF16)
    o_ref[0] = _dot(a_hi, w) + _dot(a_lo, w) + b_ref[0]


def _ada(cvec, w_ada, b_ada):
    depth, d, n = w_ada.shape
    tn = 512
    return pl.pallas_call(
        _ada_kernel,
        out_shape=jax.ShapeDtypeStruct((depth, 8, n), F32),
        grid=(depth, n // tn),
        in_specs=[pl.BlockSpec((8, d), lambda l, j: (0, 0)),
                  pl.BlockSpec((1, d, tn), lambda l, j: (l, 0, j)),
                  pl.BlockSpec((1, 1, tn), lambda l, j: (l, 0, j))],
        out_specs=pl.BlockSpec((1, 8, tn), lambda l, j: (l, 0, j)),
        compiler_params=_cparams(2 * d * tn * 4 + d * tn * 2, 2),
        name="adaln",
    )(cvec, w_ada, b_ada.reshape(depth, 1, n))


def _pre_kernel(x_ref, mod_ref, g_ref, h_ref):
    y = _rms(x_ref[...], g_ref[...])
    h_ref[...] = (y * (1.0 + mod_ref[0, 1:2, :]) + mod_ref[0, 0:1, :]).astype(h_ref.dtype)


def _mid_kernel(x_ref, y_ref, mod_ref, gpost_ref, gpre_ref, wr_ref, br_ref,
                x1_ref, h2_ref, idx_ref, tw_ref):
    x1 = x_ref[...] + mod_ref[0, 2:3, :] * _rms(y_ref[...].astype(F32), gpost_ref[...])
    x1_ref[...] = x1
    h2 = _rms(x1, gpre_ref[...]) * (1.0 + mod_ref[0, 4:5, :]) + mod_ref[0, 3:4, :]
    bits = lax.bitcast_convert_type(h2.astype(BF16).astype(F32), jnp.uint32)
    half = bits.shape[1] // 2
    h2_ref[...] = (bits[:, :half] >> 16) | (bits[:, half:] & jnp.uint32(0xFFFF0000))
    logits = jnp.dot(h2, wr_ref[...], precision=lax.Precision.HIGHEST,
                     preferred_element_type=F32) + br_ref[...]
    lane = lax.broadcasted_iota(I32, logits.shape, 1)
    idx_acc = jnp.zeros(logits.shape, I32)
    top_acc = jnp.full(logits.shape, -jnp.inf, F32)
    work = logits
    for r in range(TOP_K):
        m = jnp.max(work, axis=-1, keepdims=True)
        sel = jnp.min(jnp.where(work == m, lane, V7X_LANES), axis=-1, keepdims=True)
        idx_acc = jnp.where(lane == r, sel, idx_acc)
        top_acc = jnp.where(lane == r, m, top_acc)
        work = jnp.where(lane == sel, -jnp.inf, work)
    e = jnp.exp(top_acc - jnp.max(top_acc, axis=-1, keepdims=True))
    idx_ref[...] = idx_acc
    tw_ref[...] = e / jnp.sum(e, axis=-1, keepdims=True)


def _combine(yg_ref, tw_ref):
    tw = tw_ref[...]
    fx = tw[:, 0:1] * yg_ref[0].astype(F32)
    for k in range(1, TOP_K):
        fx = fx + tw[:, k:k + 1] * yg_ref[k].astype(F32)
    return fx


def _end_kernel(x1_ref, yg_ref, tw_ref, mod_ref, gpost_ref, modn_ref, gpren_ref, x2_ref, h_ref):
    x2 = x1_ref[...] + mod_ref[0, 5:6, :] * _rms(_combine(yg_ref, tw_ref), gpost_ref[...])
    x2_ref[...] = x2
    y = _rms(x2, gpren_ref[...])
    h_ref[...] = (y * (1.0 + modn_ref[0, 1:2, :]) + modn_ref[0, 0:1, :]).astype(h_ref.dtype)


def _final_kernel(x1_ref, yg_ref, tw_ref, mod_ref, gpost_ref, x2_ref):
    x2_ref[...] = x1_ref[...] + mod_ref[0, 5:6, :] * _rms(_combine(yg_ref, tw_ref), gpost_ref[...])


def _row_specs(tile, d, n_lat):
    row = lambda w: pl.BlockSpec((tile, w), lambda i: (i, 0))
    mod = pl.BlockSpec((1, 8, d), lambda i: (jnp.minimum(i // (n_lat // tile), 1), 0, 0))
    vec = lambda w: pl.BlockSpec((1, w), lambda i: (0, 0))
    return row, mod, vec


def _pre(x, mods, g, n_lat):
    rows, d = x.shape
    row, mod, vec = _row_specs(ROW_TILE, d, n_lat)
    return pl.pallas_call(
        _pre_kernel, out_shape=jax.ShapeDtypeStruct((rows, d), BF16),
        grid=(rows // ROW_TILE,), in_specs=[row(d), mod, vec(d)], out_specs=row(d),
        compiler_params=_cparams(ROW_TILE * d * 24, 1), name="pre_norm",
    )(x, mods, g.reshape(1, d))


def _mid(x, y, mods, gpost, gpre, w_router, b_router, n_lat):
    rows, d = x.shape
    row, mod, vec = _row_specs(ROW_TILE, d, n_lat)
    wr = jnp.zeros((d, V7X_LANES), F32).at[:, :N_EXPERTS].set(w_router)
    br = jnp.full((1, V7X_LANES), -jnp.inf, F32).at[0, :N_EXPERTS].set(b_router)
    return pl.pallas_call(
        _mid_kernel,
        out_shape=(jax.ShapeDtypeStruct((rows, d), F32), jax.ShapeDtypeStruct((rows, d // 2), jnp.uint32),
                   jax.ShapeDtypeStruct((rows, V7X_LANES), I32), jax.ShapeDtypeStruct((rows, V7X_LANES), F32)),
        grid=(rows // ROW_TILE,),
        in_specs=[row(d), row(d), mod, vec(d), vec(d),
                  pl.BlockSpec((d, V7X_LANES), lambda i: (0, 0)), vec(V7X_LANES)],
        out_specs=(row(d), row(d // 2), row(V7X_LANES), row(V7X_LANES)),
        compiler_params=_cparams(ROW_TILE * d * 48 + d * V7X_LANES * 8, 1), name="post_mix_router",
    )(x, y, mods, gpost.reshape(1, d), gpre.reshape(1, d), wr, br)


def _end(x1, yg, top_w, mods, gpost, mods_next, gpre_next, n_lat):
    rows, d = x1.shape
    t = COMBINE_TILE
    row, mod, vec = _row_specs(t, d, n_lat)
    return pl.pallas_call(
        _end_kernel,
        out_shape=(jax.ShapeDtypeStruct((rows, d), F32), jax.ShapeDtypeStruct((rows, d), BF16)),
        grid=(rows // t,),
        in_specs=[row(d), pl.BlockSpec((TOP_K, t, d), lambda i: (0, i, 0)), row(V7X_LANES),
                  mod, vec(d), mod, vec(d)],
        out_specs=(row(d), row(d)),
        compiler_params=_cparams(t * d * (2 * 4 * (TOP_K + 2) + 2 * 2 + 16), 1), name="post_ffn_pre_norm",
    )(x1, yg, top_w, mods, gpost.reshape(1, d), mods_next, gpre_next.reshape(1, d))


def _final(x1, yg, top_w, mods, gpost, n_lat):
    d = x1.shape[1]
    t = COMBINE_TILE
    row, mod, vec = _row_specs(t, d, n_lat)
    return pl.pallas_call(
        _final_kernel, out_shape=jax.ShapeDtypeStruct((n_lat, d), F32),
        grid=(n_lat // t,),
        in_specs=[row(d), pl.BlockSpec((TOP_K, t, d), lambda i: (0, i, 0)), row(V7X_LANES), mod, vec(d)],
        out_specs=row(d),
        compiler_params=_cparams(t * d * (2 * 4 * (TOP_K + 2) + 16), 1), name="post_ffn",
    )(x1, yg, top_w, mods, gpost.reshape(1, d))


def _mm_kernel(a_ref, b_ref, o_ref, bscr):
    @pl.when(pl.program_id(1) == 0)
    def _():
        bscr[...] = b_ref[0].astype(BF16)

    o_ref[...] = _dot(a_ref[...], bscr[...]).astype(o_ref.dtype)


def _pick_tile(n, prefs):
    for t in prefs:
        if n % t == 0:
            return t
    return n


def _matmul(a, w3, layer, out_dtype):
    m, k = a.shape
    n = w3.shape[2]
    tm = _pick_tile(m, (384, 256, 128))
    tn = _pick_tile(n, (1024, 512, 256, 128))
    vm = 2 * (tm * k * 2 + k * tn * 4 + tm * tn * 4) + k * tn * 2 + tm * tn * 4
    return pl.pallas_call(
        _mm_kernel, out_shape=jax.ShapeDtypeStruct((m, n), out_dtype),
        grid=(n // tn, m // tm),
        in_specs=[pl.BlockSpec((tm, k), lambda j, i: (i, 0)),
                  pl.BlockSpec((1, k, tn), lambda j, i: (layer, 0, j))],
        out_specs=pl.BlockSpec((tm, tn), lambda j, i: (i, j)),
        scratch_shapes=[pltpu.VMEM((k, tn), BF16)],
        compiler_params=_cparams(vm, 2), name="matmul",
    )(a, w3)


def _dft_cos_sin(n):
    i = jnp.arange(n, dtype=I32)
    ang = ((i[:, None] * i[None, :]) % n).astype(F32) * (2.0 * np.pi / n)
    s = 1.0 / np.sqrt(n)
    return jnp.cos(ang) * s, jnp.sin(ang) * s


def _fft1_kernel(x_ref, f_ref, twr_ref, twi_ref, o_ref):
    n1 = f_ref.shape[1]
    tb, _, w = o_ref.shape[1:]
    xt = pltpu.einshape("abw->baw", x_ref[...])
    for j in range(tb):
        res = _dot(f_ref[...], xt[j])
        ar = res[:n1]
        ai = res[n1:]
        wr = jnp.tile(twr_ref[j], (1, w // V7X_LANES))
        wi = jnp.tile(twi_ref[j], (1, w // V7X_LANES))
        o_ref[0, j] = (ar * wr - ai * wi).astype(o_ref.dtype)
        o_ref[1, j] = (ar * wi + ai * wr).astype(o_ref.dtype)


def _fft2_kernel(m_ref, a_ref, o_ref, sr, si):
    n2, tc = a_ref.shape[1:3]
    ar = pltpu.einshape("bcw->cbw", a_ref[0])
    ai = pltpu.einshape("bcw->cbw", a_ref[1])
    for c in range(tc):
        res = _dot(m_ref[...], jnp.concatenate([ar[c], ai[c]], axis=0))
        sr[c] = res[:n2].astype(sr.dtype)
        si[c] = res[n2:].astype(si.dtype)
    o_ref[0] = pltpu.einshape("cdw->dcw", sr[...])
    o_ref[1] = pltpu.einshape("cdw->dcw", si[...])


def _dft_dense_kernel(m_ref, a_ref, o_ref):
    o_ref[...] = _dot(m_ref[...], a_ref[...]).astype(o_ref.dtype)


def _fft3_kernel(x_ref, cb_ref, sb_ref, wf_ref, bf_ref, o_ref):
    z = _dot(x_ref[0], cb_ref[...]) + _dot(x_ref[1], sb_ref[...])
    o_ref[...] = (_dot(z.astype(BF16), wf_ref[...]) + bf_ref[...]).astype(o_ref.dtype)


def _fourier_mix(p, w, w_fnet, b_fnet, n_lat):
    rows, cols = p.shape
    n_ctx = rows - n_lat
    n1 = FFT_N1
    n2 = n_lat // n1
    tb = BF16_SUBLANES
    c1, s1 = _dft_cos_sin(n1)
    f1 = jnp.concatenate([c1, -s1], axis=0).astype(BF16)
    bi = jnp.arange(n2, dtype=I32)[:, None]
    ci = jnp.arange(n1, dtype=I32)[None, :]
    tang = ((bi * ci) % n_lat).astype(F32) * (2.0 * np.pi / n_lat)
    twr = jnp.broadcast_to(jnp.cos(tang)[:, :, None], (n2, n1, V7X_LANES))
    twi = jnp.broadcast_to(-jnp.sin(tang)[:, :, None], (n2, n1, V7X_LANES))
    a = pl.pallas_call(
        _fft1_kernel, out_shape=jax.ShapeDtypeStruct((2, n2, n1, w), BF16),
        grid=(n2 // tb,),
        in_specs=[pl.BlockSpec((n1, tb, w), lambda j: (0, j, 0)),
                  pl.BlockSpec((2 * n1, n1), lambda j: (0, 0)),
                  pl.BlockSpec((tb, n1, V7X_LANES), lambda j: (j, 0, 0)),
                  pl.BlockSpec((tb, n1, V7X_LANES), lambda j: (j, 0, 0))],
        out_specs=pl.BlockSpec((2, tb, n1, w), lambda j: (0, j, 0, 0)),
        compiler_params=_cparams(2 * 3 * n1 * tb * w * 2 + 8 * n1 * w * 4, 1), name="fft_stage1",
    )(p.reshape(rows // n2, n2, cols), f1, twr, twi)
    c2, s2 = _dft_cos_sin(n2)
    m2 = jnp.concatenate([jnp.concatenate([c2, s2], axis=1),
                          jnp.concatenate([-s2, c2], axis=1)], axis=0).astype(BF16)
    tc = BF16_SUBLANES
    xl = pl.pallas_call(
        _fft2_kernel, out_shape=jax.ShapeDtypeStruct((2, n2, n1, w), BF16),
        grid=(n1 // tc,),
        in_specs=[pl.BlockSpec((2 * n2, 2 * n2), lambda j: (0, 0)),
                  pl.BlockSpec((2, n2, tc, w), lambda j: (0, 0, j, 0))],
        out_specs=pl.BlockSpec((2, n2, tc, w), lambda j: (0, 0, j, 0)),
        scratch_shapes=[pltpu.VMEM((tc, n2, w), BF16), pltpu.VMEM((tc, n2, w), BF16)],
        compiler_params=_cparams(5 * 2 * n2 * tc * w * 2 + 8 * n2 * w * 4, 1), name="fft_stage2",
    )(m2, a).reshape(2, n_lat, w)
    cc, sc = _dft_cos_sin(n_ctx)
    mc = jnp.concatenate([cc, -sc], axis=0).astype(BF16)
    xc = pl.pallas_call(
        _dft_dense_kernel, out_shape=jax.ShapeDtypeStruct((2 * n_ctx, w), BF16),
        grid=(1,),
        in_specs=[pl.BlockSpec((2 * n_ctx, n_ctx), lambda j: (0, 0)),
                  pl.BlockSpec((n_ctx, w), lambda j: (n_lat // n_ctx, 0))],
        out_specs=pl.BlockSpec((2 * n_ctx, w), lambda j: (0, 0)),
        compiler_params=_cparams(16 * n_ctx * w, 1), name="dft_context",
    )(mc, p).reshape(2, n_ctx, w)
    xall = jnp.concatenate([xl, xc], axis=1)
    cg, sg = _dft_cos_sin(HEAD_DIM)
    eye = jnp.eye(w // HEAD_DIM, dtype=F32)
    cb = jnp.kron(eye, cg).astype(BF16)
    sb = jnp.kron(eye, sg).astype(BF16)
    full = lambda r, c: pl.BlockSpec((r, c), lambda i: (0, 0))
    return pl.pallas_call(
        _fft3_kernel, out_shape=jax.ShapeDtypeStruct((rows, w), BF16),
        grid=(rows // ROW_TILE,),
        in_specs=[pl.BlockSpec((2, ROW_TILE, w), lambda i: (0, i, 0)),
                  full(w, w), full(w, w), full(w, w), full(1, w)],
        out_specs=pl.BlockSpec((ROW_TILE, w), lambda i: (i, 0)),
        compiler_params=_cparams(2 * (3 * w * w * 2 + 3 * ROW_TILE * w * 2) + ROW_TILE * w * 12, 1),
        name="fft_channel_linear",
    )(xall, cb, sb, w_fnet.astype(BF16), b_fnet.reshape(1, w))


def _na_tables(rows):
    kr = min(NA_KR, rows)
    nb = min(kr + NA_QROWS - 1, rows)
    nqb = rows // NA_QROWS
    r0 = np.arange(nqb) * NA_QROWS
    band0 = np.minimum(np.clip(r0 - kr // 2, 0, rows - kr), rows - nb)
    band_rows = band0[:, None] + np.arange(nb)[None, :]
    q_row = np.repeat(r0[:, None] + np.arange(NA_QROWS)[None, :], GRID_W, axis=1)
    q_col = np.tile(np.arange(GRID_W), NA_QROWS)
    k_row = np.repeat(band_rows, GRID_W, axis=1)
    k_col = np.tile(np.arange(GRID_W), nb)
    win_r = np.clip(q_row - kr // 2, 0, rows - kr)[:, :, None]
    win_c = np.clip(q_col - NA_KC // 2, 0, GRID_W - NA_KC)[:, None]
    kro = k_row[:, None, :]
    col_ok = (k_col[None, :] >= win_c) & (k_col[None, :] < win_c + NA_KC)
    mask = (kro >= win_r) & (kro < win_r + kr) & col_ok[None]
    d_row = np.clip(kro - q_row[:, :, None] + NA_KR - 1, 0, 2 * NA_KR - 2)
    d_col = np.clip(k_col[None, :] - q_col[:, None] + NA_KC - 1, 0, 2 * NA_KC - 2)
    pats, pid = [], np.zeros(nqb, np.int32)
    for n in range(nqb):
        for p, (m0, d0) in enumerate(pats):
            if np.array_equal(m0, mask[n]) and np.array_equal(d0, d_row[n]):
                pid[n] = p
                break
        else:
            pid[n] = len(pats)
            pats.append((mask[n], d_row[n]))
    pmask = np.stack([p[0] for p in pats])
    pdrow = np.stack([p[1] for p in pats])
    return (band0 * GRID_W).astype(np.int32), pid, pmask, pdrow, d_col, nb


def _na_bias(rpb, pmask, pdrow, d_col, nb, n_ctx):
    n_heads = rpb.shape[0]
    n_pat, qb, nk = pmask.shape
    tiles = rpb.astype(F32)[:, :, d_col[:GRID_W, :GRID_W]]
    drow_small = pdrow[:, ::GRID_W, ::GRID_W]
    b6 = tiles[:, drow_small]
    bias = b6.transpose(0, 1, 2, 4, 3, 5).reshape(n_heads, n_pat, qb, nk)
    return jnp.pad(jnp.where(pmask[None], bias, MASK_VALUE), ((0, 0), (0, 0), (0, 0), (0, n_ctx)))


def _na_kernel(pid_ref, st_ref, q_ref, k_ref, v_ref, cos_ref, sin_ref, bias_ref, o_ref, qs, ks,
               *, n_lat, n_ctx, nqb, qb, nk):
    scale = HEAD_DIM ** -0.5
    rt = 512 if n_lat % 512 == 0 else qb

    def rope_body(i, carry):
        r = pl.multiple_of(i * rt, rt)
        c = cos_ref[pl.ds(r, rt), :]
        s = sin_ref[pl.ds(r, rt), :]
        even = (lax.broadcasted_iota(I32, (rt, HEAD_DIM), 1) % 2) == 0

        def rot(x):
            xs = jnp.where(even, pltpu.roll(x, HEAD_DIM - 1, 1), pltpu.roll(x, 1, 1))
            return x * c + xs * s

        qs[pl.ds(r, rt), :] = (rot(q_ref[pl.ds(r, rt), :].astype(F32)) * scale).astype(BF16)
        ks[pl.ds(r, rt), :] = rot(k_ref[pl.ds(r, rt), :].astype(F32)).astype(BF16)
        return carry

    lax.fori_loop(0, n_lat // rt, rope_body, 0)
    kc = k_ref[n_lat:n_lat + n_ctx, :]
    vc = v_ref[n_lat:n_lat + n_ctx, :]

    def block_body(n, carry):
        r = pl.multiple_of(n * qb, qb)
        st = pl.multiple_of(st_ref[n], GRID_W)
        k_all = jnp.concatenate([ks[pl.ds(st, nk), :], kc], axis=0)
        v_all = jnp.concatenate([v_ref[pl.ds(st, nk), :], vc], axis=0)
        s = _dot_nt(qs[pl.ds(r, qb), :], k_all) + bias_ref[0, pid_ref[n]]
        p = jnp.exp(s - jnp.max(s, axis=-1, keepdims=True))
        o = _dot(p.astype(BF16), v_all) / jnp.sum(p, axis=-1, keepdims=True)
        o_ref[pl.ds(r, qb), :] = o.astype(o_ref.dtype)
        return carry

    lax.fori_loop(0, nqb, block_body, 0, unroll=NA_UNROLL)
    s = _dot_nt(q_ref[n_lat:n_lat + n_ctx, :], kc) * scale
    p = jnp.exp(s - jnp.max(s, axis=-1, keepdims=True))
    o = _dot(p.astype(BF16), vc) / jnp.sum(p, axis=-1, keepdims=True)
    o_ref[n_lat:n_lat + n_ctx, :] = o.astype(o_ref.dtype)


def _na(p, col0, n_heads, rpb, cos_rep, sin_sgn, n_lat):
    rows = p.shape[0]
    n_ctx = rows - n_lat
    grid_rows = n_lat // GRID_W
    st, pid, pmask, pdrow, d_col, nb = _na_tables(grid_rows)
    n_pat, qb, nk = pmask.shape
    nqb = grid_rows // NA_QROWS
    bias = _na_bias(rpb, pmask, pdrow, d_col, nb, n_ctx)
    head = lambda off: pl.BlockSpec((rows, HEAD_DIM), lambda h, *_: (0, col0 + off * n_heads + h))
    kern = functools.partial(_na_kernel, n_lat=n_lat, n_ctx=n_ctx, nqb=nqb, qb=qb, nk=nk)
    vm = 2 * (4 * rows * HEAD_DIM * 2 + n_pat * qb * nk * 4) + 2 * n_lat * HEAD_DIM * 4 \
        + 2 * n_lat * HEAD_DIM * 2 + 8 * qb * (nk + n_ctx) * 4
    return pl.pallas_call(
        kern, out_shape=jax.ShapeDtypeStruct((rows, n_heads * HEAD_DIM), BF16),
        grid_spec=pltpu.PrefetchScalarGridSpec(
            num_scalar_prefetch=2, grid=(n_heads,),
            in_specs=[head(0), head(1), head(2),
                      pl.BlockSpec(memory_space=pltpu.VMEM), pl.BlockSpec(memory_space=pltpu.VMEM),
                      pl.BlockSpec((1, n_pat, qb, nk + n_ctx), lambda h, *_: (h, 0, 0, 0))],
            out_specs=pl.BlockSpec((rows, HEAD_DIM), lambda h, *_: (0, h)),
            scratch_shapes=[pltpu.VMEM((n_lat, HEAD_DIM), BF16), pltpu.VMEM((n_lat, HEAD_DIM), BF16)]),
        compiler_params=_cparams(vm, 1), name="neighborhood_attention",
    )(jnp.asarray(pid), jnp.asarray(st), p, p, p, cos_rep, sin_sgn, bias)


def _hgrn_gates(z, lbv):
    sg = jax.nn.sigmoid(z)
    f = lbv + (1.0 - lbv) * sg
    lf = jnp.log(jnp.maximum(f, GATE_FLOOR))
    kk = (1.0 - lbv) * (1.0 - sg)
    return lf, kk


def _hgrn_kernel(hq_ref, hi_ref, hf_ref, hb_ref, hg_ref, lb_ref, gn_ref, o_ref, acc, stf, stb,
                 *, n_lat, n_ctx):
    c_rows, sub = HGRN_CHUNK, HGRN_SUB
    n_sub = c_rows // sub
    dk = HEAD_DIM
    rows = n_lat + n_ctx
    lbf = lb_ref[0, 0:1, :]
    lbb = lb_ref[0, 1:2, :]

    def decay_body(i, mn):
        r = pl.multiple_of(i * c_rows, c_rows)
        lf_f, _ = _hgrn_gates(hf_ref[pl.ds(r, c_rows), :].astype(F32), lbf)
        lf_b, _ = _hgrn_gates(hb_ref[pl.ds(r, c_rows), :].astype(F32), lbb)
        for j in range(n_sub):
            mn = jnp.minimum(mn, jnp.sum(lf_f[j * sub:(j + 1) * sub], axis=0, keepdims=True))
            mn = jnp.minimum(mn, jnp.sum(lf_b[j * sub:(j + 1) * sub], axis=0, keepdims=True))
        return mn

    mn = lax.fori_loop(0, rows // c_rows, decay_body, jnp.zeros((1, dk), F32))
    safe = jnp.min(mn) >= -HGRN_SAFE_DECAY

    acc[...] = jnp.zeros(acc.shape, F32)
    stf[...] = jnp.zeros(stf.shape, F32)
    stb[...] = jnp.zeros(stb.shape, F32)

    row_i = lax.broadcasted_iota(I32, (c_rows, dk), 0)
    t_i = lax.broadcasted_iota(I32, (c_rows, c_rows), 0)
    s_i = lax.broadcasted_iota(I32, (c_rows, c_rows), 1)

    def chunk(r0, z_ref, lbv, st_ref, rev):
        lf, kk = _hgrn_gates(z_ref[pl.ds(r0, c_rows), :].astype(F32), lbv)
        q = _silu(hq_ref[pl.ds(r0, c_rows), :].astype(F32))
        v = hi_ref[pl.ds(r0, c_rows), :]
        b = lf
        step = 1
        while step < c_rows:
            if rev:
                b = b + jnp.where(row_i < c_rows - step, pltpu.roll(b, c_rows - step, 0), 0.0)
            else:
                b = b + jnp.where(row_i >= step, pltpu.roll(b, step, 0), 0.0)
            step *= 2
        zero = jnp.zeros((1, dk), F32)
        refs = []
        for i in range(n_sub):
            if rev:
                refs.append(b[(i + 1) * sub:(i + 1) * sub + 1] if i < n_sub - 1 else zero)
            else:
                refs.append(b[i * sub - 1:i * sub] if i > 0 else zero)
        qd = jnp.concatenate([(q[i * sub:(i + 1) * sub] * jnp.exp(b[i * sub:(i + 1) * sub] - refs[i])).astype(BF16)
                              for i in range(n_sub)], axis=0)
        ks = []
        for i in range(n_sub):
            lo, hi = (i * sub, c_rows) if rev else (0, (i + 1) * sub)
            part = (kk[lo:hi] * jnp.exp(refs[i] - b[lo:hi])).astype(BF16)
            if hi - lo < c_rows:
                pad = jnp.zeros((c_rows - (hi - lo), dk), BF16)
                part = jnp.concatenate([pad, part] if rev else [part, pad], axis=0)
            ks.append(part)
        scores = _dot_nt(qd, jnp.concatenate(ks, axis=0))
        own = jnp.concatenate([scores[i * sub:(i + 1) * sub, i * c_rows:(i + 1) * c_rows]
                               for i in range(n_sub)], axis=0)
        pm = jnp.where((s_i >= t_i) if rev else (s_i <= t_i), own, 0.0).astype(BF16)
        st = st_ref[...]
        o = _dot(pm, v) + _dot_nt((q * jnp.exp(b)).astype(BF16), st.astype(BF16))
        blast = b[0:1] if rev else b[c_rows - 1:c_rows]
        ke = (kk * jnp.exp(blast - b)).astype(BF16)
        st_ref[...] = st * jnp.exp(blast) + _dot_tn(v, ke)
        return o

    def fast_path():
        def run(base, n):
            def body(c, carry):
                rf = pl.multiple_of(base + c * c_rows, c_rows)
                rb = pl.multiple_of(base + (n - 1 - c) * c_rows, c_rows)
                o_f = chunk(rf, hf_ref, lbf, stf, False)
                acc[pl.ds(rf, c_rows), :] = acc[pl.ds(rf, c_rows), :] + o_f
                o_b = chunk(rb, hb_ref, lbb, stb, True)
                acc[pl.ds(rb, c_rows), :] = acc[pl.ds(rb, c_rows), :] + o_b
                return carry
            lax.fori_loop(0, n, body, 0, unroll=HGRN_UNROLL)
        run(n_lat, n_ctx // c_rows)
        run(0, n_lat // c_rows)

    def slow_path():
        pack = BF16_SUBLANES
        sub_i = lax.broadcasted_iota(I32, (pack, dk), 0)
        row8 = lax.broadcasted_iota(I32, (8, dk), 0)

        def load_row(ref, t):
            r = pl.multiple_of((t // pack) * pack, pack)
            blk = ref[pl.ds(r, pack), :].astype(F32)
            return jnp.sum(jnp.where(sub_i == t - r, blk, 0.0), axis=0, keepdims=True)

        def run(base, n, z_ref, lbv, st_ref, rev):
            def body(i, carry):
                t = base + ((n - 1 - i) if rev else i)
                lf, kk = _hgrn_gates(load_row(z_ref, t), lbv)
                q = _silu(load_row(hq_ref, t))
                v = load_row(hi_ref, t)
                v8 = jnp.where(row8 == 0, v, 0.0).astype(BF16)
                k8 = jnp.where(row8 == 0, kk, 0.0).astype(BF16)
                st = st_ref[...] * jnp.exp(lf) + _dot_tn(v8, k8)
                st_ref[...] = st
                q8 = jnp.broadcast_to(q, (8, dk)).astype(BF16)
                o = _dot_nt(q8, st.astype(BF16))
                acc[pl.ds(t, 1), :] = acc[pl.ds(t, 1), :] + o[0:1]
                return carry
            lax.fori_loop(0, n, body, 0)
        run(n_lat, n_ctx, hf_ref, lbf, stf, False)
        run(0, n_lat, hf_ref, lbf, stf, False)
        run(n_lat, n_ctx, hb_ref, lbb, stb, True)
        run(0, n_lat, hb_ref, lbb, stb, True)

    lax.cond(safe, fast_path, slow_path)

    rt = ROW_TILE

    def readout(i, carry):
        r = pl.multiple_of(i * rt, rt)
        y = _rms(acc[pl.ds(r, rt), :], gn_ref[...])
        o_ref[pl.ds(r, rt), :] = (y * _silu(hg_ref[pl.ds(r, rt), :].astype(F32))).astype(o_ref.dtype)
        return carry

    lax.fori_loop(0, rows // rt, readout, 0)


def _hgrn(p, col0, n_heads, lb, g_norm, n_lat):
    rows = p.shape[0]
    n_ctx = rows - n_lat
    head = lambda off: pl.BlockSpec((rows, HEAD_DIM), lambda h: (0, col0 + off * n_heads + h))
    lbh = lb.reshape(2, n_heads, HEAD_DIM).transpose(1, 0, 2)
    kern = functools.partial(_hgrn_kernel, n_lat=n_lat, n_ctx=n_ctx)
    n_keys = HGRN_CHUNK * HGRN_CHUNK // HGRN_SUB
    vm = 2 * 6 * rows * HEAD_DIM * 2 + rows * HEAD_DIM * 4 + 6 * HGRN_UNROLL * HGRN_CHUNK * n_keys * 4
    return pl.pallas_call(
        kern, out_shape=jax.ShapeDtypeStruct((rows, n_heads * HEAD_DIM), BF16),
        grid=(n_heads,),
        in_specs=[head(0), head(1), head(2), head(3), head(4),
                  pl.BlockSpec((1, 2, HEAD_DIM), lambda h: (h, 0, 0)),
                  pl.BlockSpec((1, HEAD_DIM), lambda h: (0, 0))],
        out_specs=pl.BlockSpec((rows, HEAD_DIM), lambda h: (0, h)),
        scratch_shapes=[pltpu.VMEM((rows, HEAD_DIM), F32), pltpu.VMEM((HEAD_DIM, HEAD_DIM), F32),
                        pltpu.VMEM((HEAD_DIM, HEAD_DIM), F32)],
        compiler_params=_cparams(vm, 1), name="hgrn2_bidir",
    )(p, p, p, p, p, lbh, g_norm.reshape(1, HEAD_DIM))


STEP_LOAD, STEP_COMPUTE, STEP_IDLE, STEP_ZERO = 0, 1, 2, 3


def _deinterleave_perm():
    g = V7X_MXU_DIM
    src = np.arange(g)
    dst = np.where(src % 2 == 0, src // 2, g // 2 + src // 2)
    perm = np.zeros((g, g), np.float32)
    perm[src, dst] = 1.0
    return jnp.asarray(perm, BF16)


def _expert_kernel(kind_ref, wexp_ref, wchunk_ref, wslot_ref, blk_ref, cexp_ref, cslot_ref, nv_ref,
                   tok_cur, tok_nxt, h_hbm, wgu_ref, wdn_ref, bgu_ref, bdn_ref, perm_ref, y_ref,
                   wgu_s, wdn_s, xbuf, sems):
    s = pl.program_id(0)
    kind = kind_ref[s]
    tm = xbuf.shape[1]

    def cast_chunk():
        c = wchunk_ref[s]
        sl = wslot_ref[s]
        rg = wgu_ref.shape[0]
        rd = wdn_ref.shape[0]
        wgu_s[sl, pl.ds(pl.multiple_of(c * rg, rg), rg), :] = wgu_ref[...].astype(BF16)
        wdn_s[sl, pl.ds(pl.multiple_of(c * rd, rd), rd), :] = wdn_ref[...].astype(BF16)

    @pl.when(kind == STEP_LOAD)
    def _():
        cast_chunk()

    @pl.when(kind == STEP_ZERO)
    def _():
        y_ref[...] = jnp.zeros(y_ref.shape, y_ref.dtype)

    @pl.when(kind == STEP_COMPUTE)
    def _():
        b = blk_ref[s]
        n_valid = nv_ref[0]
        slot = b % 2
        cs = cslot_ref[s]

        def row_copy(tok_ref, r, dst_slot):
            return pltpu.make_async_copy(h_hbm.at[pl.ds(tok_ref[0, 0, r], 1), :],
                                         xbuf.at[dst_slot, pl.ds(r, 1), :], sems.at[dst_slot])

        def block_wait(dst_slot):
            pltpu.make_async_copy(h_hbm.at[pl.ds(0, tm), :], xbuf.at[dst_slot], sems.at[dst_slot]).wait()

        @pl.when(b == 0)
        def _():
            def body(r, carry):
                row_copy(tok_cur, r, 0).start()
                return carry
            lax.fori_loop(0, tm, body, 0, unroll=8)

        block_wait(slot)
        cast_chunk()
        g, h = V7X_MXU_DIM, V7X_LANES
        half = tm // 2
        perm = perm_ref[...]
        dp = xbuf.shape[2]
        k_chunks = 8
        kc = dp // k_chunks
        per_chunk = tm // k_chunks

        def unpack(words):
            lo16 = lax.bitcast_convert_type(words << 16, F32).astype(BF16)
            hi16 = lax.bitcast_convert_type(words & jnp.uint32(0xFFFF0000), F32).astype(BF16)
            return lo16, hi16

        for r in (0, half):
            gu = bgu_ref[0]
            for c in range(k_chunks):
                if r == 0:
                    for i in range(per_chunk):
                        row = c * per_chunk + i
                        row_copy(tok_nxt, row, 1 - slot).start(priority=row % 2)
                x_lo, x_hi = unpack(xbuf[slot, r:r + half, c * kc:(c + 1) * kc])
                gu = gu + _dot(x_lo, wgu_s[cs, c * kc:(c + 1) * kc, :]) \
                    + _dot(x_hi, wgu_s[cs, dp + c * kc:dp + (c + 1) * kc, :])
            hi = gu.astype(BF16)
            lo = (gu - hi.astype(F32)).astype(BF16)
            parts = [_dot(hi[:, j * g:(j + 1) * g], perm) + _dot(lo[:, j * g:(j + 1) * g], perm)
                     for j in range(gu.shape[1] // g)]
            gate = jnp.minimum(jnp.concatenate([p[:, :h] for p in parts], axis=1), SWIGLU_LIMIT)
            up = jnp.clip(jnp.concatenate([p[:, h:] for p in parts], axis=1), -SWIGLU_LIMIT, SWIGLU_LIMIT)
            act = gate * jax.nn.sigmoid(SWIGLU_ALPHA * gate) * (up + 1.0)
            y_ref[r:r + half, :] = (_dot(act.astype(BF16), wdn_s[cs]) + bdn_ref[0]).astype(y_ref.dtype)

        @pl.when(b + 1 >= n_valid)
        def _():
            block_wait(1 - slot)


def _expert_program(counts, block_e, n_valid, n_blocks):
    n_e, n_ch = N_EXPERTS, MOE_WCHUNKS
    n_steps = n_blocks + n_ch * n_e
    nb = (counts + MOE_TILE - 1) // MOE_TILE
    active = nb > 0
    bstart = jnp.cumsum(nb) - nb
    e_idx = jnp.arange(n_e, dtype=I32)
    at_or_after = lax.cummin(jnp.where(active, e_idx, n_e)[::-1], axis=0)[::-1]
    next_active = jnp.concatenate([at_or_after[1:], jnp.full((1,), n_e, I32)])
    first_active = at_or_after[0]
    nl = jnp.where(active & (next_active < n_e), n_ch, 0)
    seg_len = jnp.where(active, jnp.maximum(nb, nl), 0)
    seg_start = n_ch + jnp.cumsum(seg_len) - seg_len
    slot = (jnp.cumsum(active.astype(I32)) - active.astype(I32)) % 2
    j = jnp.arange(n_ch, dtype=I32)
    lpos = jnp.where((nl > 0)[:, None], seg_start[:, None] + j[None, :], n_steps).reshape(-1)
    lpos = jnp.concatenate([j, lpos])
    lexp = jnp.concatenate([jnp.broadcast_to(first_active, (n_ch,)),
                            jnp.broadcast_to(jnp.minimum(next_active, n_e - 1)[:, None], (n_e, n_ch)).reshape(-1)])
    lslot = jnp.concatenate([jnp.zeros((n_ch,), I32),
                             jnp.broadcast_to((1 - slot)[:, None], (n_e, n_ch)).reshape(-1)])
    lchunk = jnp.concatenate([j, jnp.broadcast_to(j[None, :], (n_e, n_ch)).reshape(-1)])
    b = jnp.arange(n_blocks, dtype=I32)
    jb = b - bstart[block_e]
    cpos = jnp.where(b < n_valid[0], seg_start[block_e] + jb, n_steps)
    fill = lambda pos, val: jnp.zeros((n_steps + 1,), I32).at[pos].set(val)[:n_steps]
    kind = jnp.full((n_steps + 1,), STEP_IDLE, I32).at[lpos].set(STEP_LOAD).at[cpos].set(STEP_COMPUTE)[:n_steps]
    step = jnp.arange(n_steps, dtype=I32)
    last_load = lax.cummax(jnp.where(fill(lpos, jnp.ones_like(lpos)) > 0, step, 0), axis=0)
    last_comp = lax.cummax(jnp.where(kind == STEP_COMPUTE, step, -1), axis=0)
    wexp = fill(lpos, lexp)[last_load]
    wchunk = fill(lpos, lchunk)[last_load]
    wslot = fill(lpos, lslot)[last_load]
    blk = jnp.where(last_comp >= 0, fill(cpos, b)[jnp.maximum(last_comp, 0)], 0)
    n_used = n_ch + jnp.sum(seg_len)
    spare = n_valid[0] + step - n_used
    kind = jnp.where((step >= n_used) & (spare < n_blocks), STEP_ZERO, kind)
    blk = jnp.where(step >= n_used, jnp.minimum(spare, n_blocks - 1), blk)
    cexp = block_e[blk]
    return kind, wexp, wchunk, wslot, blk, cexp, slot[cexp]


def _experts(h2, row_tok, counts, block_e, n_valid, w_gate_up, b_gate_up, w_down, b_down, layer):
    d = w_down.shape[3]
    n_blocks = row_tok.shape[0] // MOE_TILE
    ff = w_down.shape[2]
    n_ch = MOE_WCHUNKS
    prog = _expert_program(counts, block_e, n_valid, n_blocks)
    n_steps = prog[0].shape[0]
    tok3 = row_tok.reshape(n_blocks, 1, MOE_TILE)
    tok_spec = lambda off: pl.BlockSpec(
        (1, 1, MOE_TILE), lambda s, kind, we, wc, ws, blk, *_: (jnp.minimum(blk[s] + off, n_blocks - 1), 0, 0),
        memory_space=pltpu.SMEM)
    wspec = lambda r, c: pl.BlockSpec((None, None, r, c), lambda s, kind, we, wc, *_: (layer, we[s], wc[s], 0))
    bspec = lambda c: pl.BlockSpec((1, 1, c), lambda s, kind, we, wc, ws, blk, ce, *_: (ce[s], 0, 0))
    vm = 2 * (d * 2 * ff * 2 + ff * d * 2) + 2 * (d * 2 * ff + ff * d) * 4 // n_ch \
        + 2 * MOE_TILE * d * 2 + 2 * MOE_TILE * d * 2 + MOE_TILE * (2 * ff * 10 + d * 6)
    return pl.pallas_call(
        _expert_kernel, out_shape=jax.ShapeDtypeStruct((n_blocks * MOE_TILE, d), BF16),
        grid_spec=pltpu.PrefetchScalarGridSpec(
            num_scalar_prefetch=8, grid=(n_steps,),
            in_specs=[tok_spec(0), tok_spec(1),
                      pl.BlockSpec(memory_space=pl.ANY),
                      wspec(d // n_ch, 2 * ff), wspec(ff // n_ch, d), bspec(2 * ff), bspec(d),
                      pl.BlockSpec((V7X_MXU_DIM, V7X_MXU_DIM), lambda s, *_: (0, 0))],
            out_specs=pl.BlockSpec((MOE_TILE, d), lambda s, kind, we, wc, ws, blk, *_: (blk[s], 0)),
            scratch_shapes=[pltpu.VMEM((2, d, 2 * ff), BF16), pltpu.VMEM((2, ff, d), BF16),
                            pltpu.VMEM((2, MOE_TILE, d // 2), jnp.uint32), pltpu.SemaphoreType.DMA((2,))]),
        compiler_params=_cparams(vm, 1), name="expert_ffn",
    )(*prog, n_valid, tok3, tok3, h2, w_gate_up, w_down,
      b_gate_up[layer].reshape(N_EXPERTS, 1, 2 * ff), b_down[layer].reshape(N_EXPERTS, 1, d), _deinterleave_perm())


def _route(top_idx, n_tok):
    flat_e = top_idx[:n_tok, :TOP_K].reshape(-1)
    n_pairs = flat_e.shape[0]
    onehot = (flat_e[:, None] == jnp.arange(N_EXPERTS, dtype=I32)[None, :]).astype(I32)
    csum = jnp.cumsum(onehot, axis=0)
    rank = jnp.sum(onehot * csum, axis=1) - 1
    counts = csum[-1]
    padded = (counts + MOE_TILE - 1) // MOE_TILE * MOE_TILE
    pend = jnp.cumsum(padded)
    dest = jnp.sum(onehot * (pend - padded)[None, :], axis=1) + rank
    n_blocks = -(-(n_pairs + N_EXPERTS * (MOE_TILE - 1)) // MOE_TILE)
    row_tok = jnp.zeros((n_blocks * MOE_TILE,), I32).at[dest].set(jnp.arange(n_pairs, dtype=I32) // TOP_K)
    n_valid = (pend[-1] // MOE_TILE).astype(I32).reshape(1)
    blk = jnp.minimum(jnp.arange(n_blocks, dtype=I32), n_valid[0] - 1) * MOE_TILE
    block_e = jnp.minimum(jnp.sum((pend[None, :] <= blk[:, None]).astype(I32), axis=1), N_EXPERTS - 1)
    return dest.reshape(n_tok, TOP_K), row_tok, counts, block_e, n_valid


def _moe(h2, top_idx, n_tok, w_gate_up, b_gate_up, w_down, b_down, layer):
    dest, row_tok, counts, block_e, n_valid = _route(top_idx, n_tok)
    y = _experts(h2, row_tok, counts, block_e, n_valid, w_gate_up, b_gate_up, w_down, b_down, layer)
    yg = y.at[dest.T.reshape(-1)].get(mode='promise_in_bounds')
    return yg.reshape(TOP_K, n_tok, -1)


def _rope_tables(n_tok):
    t = jnp.arange(n_tok, dtype=I32)
    row = (t // GRID_W).astype(F32)
    col = (t % GRID_W).astype(F32)
    pairs = HEAD_DIM // 4
    inv_freq = ROPE_THETA ** (-jnp.arange(pairs, dtype=F32) / pairs)
    ang = jnp.concatenate([row[:, None] * inv_freq, col[:, None] * inv_freq], axis=-1)
    cos_rep = jnp.repeat(jnp.cos(ang), 2, axis=-1)
    sin_sgn = jnp.stack([-jnp.sin(ang), jnp.sin(ang)], axis=-1).reshape(n_tok, HEAD_DIM)
    return cos_rep, sin_sgn


def kernel(x, c, ctx, c_ctx, w_ada, b_ada, norm_pre_mix, norm_post_mix, norm_pre_ffn, norm_post_ffn,
           w_in, w_fnet, b_fnet, na_rpb, hgrn_lb_logits, hgrn_out_norm, w_out,
           w_router, b_router, w_gate_up, b_gate_up, w_down, b_down):
    batch, n_lat, d = x.shape
    n_ctx = ctx.shape[1]
    assert batch == 1 and c.shape[0] == 1
    depth = w_ada.shape[0]
    fnet_w = w_fnet.shape[1]
    n_heads = na_rpb.shape[1]
    ff = w_down.shape[2]
    fnet_blocks = fnet_w // HEAD_DIM

    cvec = jnp.zeros((8, d), F32).at[0].set(c[0]).at[1].set(c_ctx)
    ada = _ada(cvec, w_ada, b_ada)
    mods = jnp.pad(ada[:, :2].reshape(depth, 2, 6, d), ((0, 0), (0, 0), (0, 2), (0, 0)))
    cos_rep, sin_sgn = _rope_tables(n_lat)
    p_lb = jax.nn.softmax(hgrn_lb_logits.astype(F32), axis=0)
    lower_bounds = jnp.cumsum(p_lb, axis=0) - p_lb[0]
    groups = 2 * ff // V7X_MXU_DIM

    xs = jnp.concatenate([x[0], ctx[0]], axis=0)
    h = _pre(xs, mods[0], norm_pre_mix[0], n_lat)
    for layer in range(depth):
        last = layer == depth - 1
        p = _matmul(h, w_in, layer, BF16)
        yf = _fourier_mix(p, fnet_w, w_fnet[layer], b_fnet[layer], n_lat)
        na = _na(p, fnet_blocks, n_heads, na_rpb[layer], cos_rep, sin_sgn, n_lat)
        hg = _hgrn(p, fnet_blocks + 3 * n_heads, n_heads, lower_bounds[layer], hgrn_out_norm[layer], n_lat)
        mix = jnp.concatenate([yf, na, hg], axis=1)
        y = _matmul(mix, w_out, layer, BF16)
        x1, h2, top_idx, top_w = _mid(xs, y, mods[layer], norm_post_mix[layer], norm_pre_ffn[layer],
                                      w_router[layer], b_router[layer], n_lat)
        yg = _moe(h2, top_idx, n_lat if last else n_lat + n_ctx, w_gate_up, b_gate_up, w_down, b_down, layer)
        if last:
            xs = _final(x1, yg, top_w, mods[layer], norm_post_ffn[layer], n_lat)
        else:
            xs, h = _end(x1, yg, top_w, mods[layer], norm_post_ffn[layer], mods[layer + 1],
                         norm_pre_mix[layer + 1], n_lat)
    return xs[:n_lat].reshape(batch, n_lat, d)
```
